```python
import jax, jax.numpy as jnp
from jax import lax
import numpy as np

D_MODEL = 2048
BATCH = 16
SEQ = 2048
DEPTH = 4
DEC_BATCH = 4
DEC_SEQ = 8192
PAST_LEN = 128

GRID_W = 64
ROPE_THETA = 10000.0
NORM_EPS = 1e-6
Q_BLOCK = 128

N_MIXERS = 3
N_GQA_LAYERS = (DEPTH + 2) // 3
N_GLA_LAYERS = (DEPTH + 1) // 3
N_MLA_LAYERS = DEPTH // 3

GQA_HEAD_DIM = 128
GQA_Q_HEADS = D_MODEL // GQA_HEAD_DIM
GQA_KV_HEADS = 4
GQA_GROUP = GQA_Q_HEADS // GQA_KV_HEADS

GLA_HEADS = 4
GLA_DK = D_MODEL // 2
GLA_DV = D_MODEL
GLA_DK_HEAD = GLA_DK // GLA_HEADS
GLA_DV_HEAD = GLA_DV // GLA_HEADS
GLA_GATE_RANK = 16
GLA_GATE_TAU = 16.0
GLA_CHUNK = 64

MLA_HEADS = 16
MLA_Q_RANK = 512
MLA_KV_RANK = 256
MLA_NOPE_DIM = 128
MLA_ROPE_DIM = 64
MLA_QK_DIM = MLA_NOPE_DIM + MLA_ROPE_DIM
MLA_V_DIM = 128

D_FF = 4 * D_MODEL

kernel_name = 'hybrid_gqa_gla_mla_adaln_encoder'


def rms_norm(x, g):
    xf = x.astype(jnp.float32)
    y = xf * lax.rsqrt(jnp.mean(xf * xf, axis=-1, keepdims=True) + NORM_EPS)
    return (y * g.astype(jnp.float32)).astype(x.dtype)


def axial_rope(n_tok, rot_dim):
    rows = n_tok // GRID_W
    row = jnp.repeat(jnp.arange(rows, dtype=jnp.float32), GRID_W)
    col = jnp.tile(jnp.arange(GRID_W, dtype=jnp.float32), rows)
    n_freq = rot_dim // 4
    inv = ROPE_THETA ** (-jnp.arange(n_freq, dtype=jnp.float32) / n_freq)
    ang = jnp.concatenate([row[:, None] * inv, col[:, None] * inv], axis=-1)
    return jnp.cos(ang), jnp.sin(ang)


def apply_rope(x, cos, sin):
    xf = x.astype(jnp.float32)
    x1 = xf[..., 0::2]
    x2 = xf[..., 1::2]
    c = cos[None, :, None, :]
    s = sin[None, :, None, :]
    out = jnp.stack([x1 * c - x2 * s, x1 * s + x2 * c], axis=-1).reshape(x.shape)
    return out.astype(x.dtype)


def block_attention(q, k, v, scale):
    B, T, Hkv, G, dh = q.shape
    dv = v.shape[-1]
    nb = T // Q_BLOCK
    qb = q.reshape(B, nb, Q_BLOCK, Hkv, G, dh).transpose(1, 0, 2, 3, 4, 5)

    def one_block(qblk):
        s = jnp.einsum('bqhgd,bkhd->bhgqk', qblk, k).astype(jnp.float32) * scale
        p = jax.nn.softmax(s, axis=-1).astype(v.dtype)
        return jnp.einsum('bhgqk,bkhe->bqhge', p, v)

    o = lax.map(one_block, qb)
    return o.transpose(1, 0, 2, 3, 4, 5).reshape(B, T, Hkv * G * dv)


def gqa_mixer(h, w_qkv, q_norm, k_norm, w_o, cos, sin):
    B, T, _ = h.shape
    qkv = h @ w_qkv
    nq = GQA_Q_HEADS * GQA_HEAD_DIM
    nk = GQA_KV_HEADS * GQA_HEAD_DIM
    q = qkv[..., :nq].reshape(B, T, GQA_Q_HEADS, GQA_HEAD_DIM)
    k = qkv[..., nq:nq + nk].reshape(B, T, GQA_KV_HEADS, GQA_HEAD_DIM)
    v = qkv[..., nq + nk:].reshape(B, T, GQA_KV_HEADS, GQA_HEAD_DIM)
    q = apply_rope(rms_norm(q, q_norm), cos, sin)
    k = apply_rope(rms_norm(k, k_norm), cos, sin)
    q = q.reshape(B, T, GQA_KV_HEADS, GQA_GROUP, GQA_HEAD_DIM)
    o = block_attention(q, k, v, GQA_HEAD_DIM ** -0.5)
    return o @ w_o


def gla_scan(q, k, v, log_a):
    B, T, H, dk = q.shape
    dv = v.shape[-1]
    n = T // GLA_CHUNK

    def chunks(x):
        return x.reshape(B, n, GLA_CHUNK, H, x.shape[-1]).transpose(1, 0, 3, 2, 4)

    qc, kc, vc, ac = chunks(q), chunks(k), chunks(v), chunks(log_a)
    b = jnp.cumsum(ac, axis=3)
    b_last = b[:, :, :, -1:, :]
    qd = qc * jnp.exp(b)
    kd = kc * jnp.exp(-b)
    ke = kc * jnp.exp(b_last - b)
    de = jnp.exp(b_last[:, :, :, 0, :])
    mask = jnp.tril(jnp.ones((GLA_CHUNK, GLA_CHUNK), dtype=bool))

    def step(S, inp):
        qd_i, kd_i, ke_i, v_i, de_i = inp
        a = jnp.where(mask, jnp.einsum('bhld,bhmd->bhlm', qd_i, kd_i), 0.0)
        o = jnp.einsum('bhlm,bhme->bhle', a, v_i) + jnp.einsum('bhld,bhde->bhle', qd_i, S)
        S = de_i[..., None] * S + jnp.einsum('bhld,bhle->bhde', ke_i, v_i)
        return S, o

    S0 = jnp.zeros((B, H, dk, dv), jnp.float32)
    _, o = lax.scan(step, S0, (qd, kd, ke, vc, de))
    return o.transpose(1, 0, 3, 2, 4).reshape(B, T, H, dv)


def gla_mixer(h, w_in, w_g1, w_g2, b_g, o_norm, w_o):
    B, T, _ = h.shape
    proj = h @ w_in
    q = proj[..., :GLA_DK]
    k = proj[..., GLA_DK:2 * GLA_DK]
    v = proj[..., 2 * GLA_DK:2 * GLA_DK + GLA_DV]
    r = proj[..., 2 * GLA_DK + GLA_DV:]
    q = q.reshape(B, T, GLA_HEADS, GLA_DK_HEAD).astype(jnp.float32) * (GLA_DK_HEAD ** -0.5)
    k = k.reshape(B, T, GLA_HEADS, GLA_DK_HEAD).astype(jnp.float32)
    v = v.reshape(B, T, GLA_HEADS, GLA_DV_HEAD).astype(jnp.float32)
    low = jnp.einsum('btd,ndr->nbtr', h, w_g1)
    z = jnp.einsum('nbtr,nrk->nbtk', low, w_g2) + b_g[:, None, None, :]
    log_a = jax.nn.log_sigmoid(z.astype(jnp.float32)) / GLA_GATE_TAU
    log_a = log_a.reshape(2, B, T, GLA_HEADS, GLA_DK_HEAD)
    o_fwd = gla_scan(q, k, v, log_a[0])
    flip = lambda t: jnp.flip(t, axis=1)
    o_bwd = flip(gla_scan(flip(q), flip(k), flip(v), flip(log_a[1])))
    o = rms_norm(o_fwd + o_bwd, o_norm)
    o = o.reshape(B, T, GLA_DV) * jax.nn.silu(r.astype(jnp.float32))
    return o.astype(h.dtype) @ w_o


def mla_mixer(h, w_down, q_a_norm, kv_a_norm, w_uq, w_ukv, q_norm, k_nope_norm,
              k_rope_norm, w_o, cos, sin):
    B, T, _ = h.shape
    down = h @ w_down
    cq = down[..., :MLA_Q_RANK]
    ckv = down[..., MLA_Q_RANK:MLA_Q_RANK + MLA_KV_RANK]
    kr = down[..., MLA_Q_RANK + MLA_KV_RANK:]
    q = (rms_norm(cq, q_a_norm) @ w_uq).reshape(B, T, MLA_HEADS, MLA_QK_DIM)
    kv = (rms_norm(ckv, kv_a_norm) @ w_ukv).reshape(B, T, MLA_HEADS, MLA_NOPE_DIM + MLA_V_DIM)
    k_nope = rms_norm(kv[..., :MLA_NOPE_DIM], k_nope_norm)
    v = kv[..., MLA_NOPE_DIM:]
    q = rms_norm(q, q_norm)
    q = jnp.concatenate([q[..., :MLA_NOPE_DIM], apply_rope(q[..., MLA_NOPE_DIM:], cos, sin)], axis=-1)
    kr = apply_rope(rms_norm(kr, k_rope_norm)[:, :, None, :], cos, sin)
    k = jnp.concatenate([k_nope, jnp.broadcast_to(kr, (B, T, MLA_HEADS, MLA_ROPE_DIM))], axis=-1)
    o = block_attention(q[:, :, :, None, :], k, v, MLA_QK_DIM ** -0.5)
    return o @ w_o


def sq_relu_mlp(h, w1, w2):
    a = jax.nn.relu(h @ w1)
    return (a * a) @ w2


def trunk(x, c, p):
    T = x.shape[1]
    cos_a, sin_a = axial_rope(T, GQA_HEAD_DIM)
    cos_c, sin_c = axial_rope(T, MLA_ROPE_DIM)
    cond = jax.nn.silu(c.astype(jnp.float32)).astype(x.dtype)
    for i in range(DEPTH):
        mod = cond @ p['ada_w'][i] + p['ada_b'][i]
        sh1, sc1, g1, sh2, sc2, g2 = jnp.split(mod[:, None, :], 6, axis=-1)
        h = rms_norm(x, p['norm1_g'][i]) * (1.0 + sc1) + sh1
        kind = i % N_MIXERS
        j = i // N_MIXERS
        if kind == 0:
            y = gqa_mixer(h, p['gqa_w_qkv'][j], p['gqa_q_norm'][j], p['gqa_k_norm'][j],
                          p['gqa_w_o'][j], cos_a, sin_a)
        elif kind == 1:
            y = gla_mixer(h, p['gla_w_in'][j], p['gla_w_g1'][j], p['gla_w_g2'][j],
                          p['gla_b_g'][j], p['gla_o_norm'][j], p['gla_w_o'][j])
        else:
            y = mla_mixer(h, p['mla_w_down'][j], p['mla_q_a_norm'][j], p['mla_kv_a_norm'][j],
                          p['mla_w_uq'][j], p['mla_w_ukv'][j], p['mla_q_norm'][j],
                          p['mla_k_nope_norm'][j], p['mla_k_rope_norm'][j], p['mla_w_o'][j],
                          cos_c, sin_c)
        x = x + g1 * y
        h = rms_norm(x, p['norm2_g'][i]) * (1.0 + sc2) + sh2
        x = x + g2 * sq_relu_mlp(h, p['mlp_w1'][i], p['mlp_w2'][i])
    return x


def setup_inputs(seed: int = 0) -> dict:
    key = jax.random.key(seed)
    ks = iter(jax.random.split(key, 32))
    f32 = jnp.float32

    def nrm(shape, scale):
        return jax.random.normal(next(ks), shape, f32) * scale

    def gain(shape):
        return 1.0 + 0.05 * jax.random.normal(next(ks), shape, f32)

    D = D_MODEL
    nA, nB, nC = N_GQA_LAYERS, N_GLA_LAYERS, N_MLA_LAYERS
    return {
        'x_prompt': nrm((BATCH, SEQ, D), 1.0),
        'x_sample': nrm((DEC_BATCH, DEC_SEQ, D), 1.0),
        'c_prompt': nrm((BATCH, D), 1.0),
        'c_sample': nrm((DEC_BATCH, D), 1.0),
        'norm1_g': gain((DEPTH, D)),
        'norm2_g': gain((DEPTH, D)),
        'ada_w': nrm((DEPTH, D, 6 * D), 0.5 * D ** -0.5),
        'ada_b': nrm((DEPTH, 6 * D), 0.01),
        'gqa_w_qkv': nrm((nA, D, (GQA_Q_HEADS + 2 * GQA_KV_HEADS) * GQA_HEAD_DIM), D ** -0.5),
        'gqa_q_norm': gain((nA, GQA_HEAD_DIM)),
        'gqa_k_norm': gain((nA, GQA_HEAD_DIM)),
        'gqa_w_o': nrm((nA, GQA_Q_HEADS * GQA_HEAD_DIM, D), (GQA_Q_HEADS * GQA_HEAD_DIM) ** -0.5),
        'gla_w_in': nrm((nB, D, 2 * GLA_DK + 2 * GLA_DV), D ** -0.5),
        'gla_w_g1': nrm((nB, 2, D, GLA_GATE_RANK), D ** -0.5),
        'gla_w_g2': nrm((nB, 2, GLA_GATE_RANK, GLA_DK), GLA_GATE_RANK ** -0.5),
        'gla_b_g': nrm((nB, 2, GLA_DK), 0.01),
        'gla_o_norm': gain((nB, GLA_DV_HEAD)),
        'gla_w_o': nrm((nB, GLA_DV, D), GLA_DV ** -0.5),
        'mla_w_down': nrm((nC, D, MLA_Q_RANK + MLA_KV_RANK + MLA_ROPE_DIM), D ** -0.5),
        'mla_q_a_norm': gain((nC, MLA_Q_RANK)),
        'mla_kv_a_norm': gain((nC, MLA_KV_RANK)),
        'mla_w_uq': nrm((nC, MLA_Q_RANK, MLA_HEADS * MLA_QK_DIM), MLA_Q_RANK ** -0.5),
        'mla_w_ukv': nrm((nC, MLA_KV_RANK, MLA_HEADS * (MLA_NOPE_DIM + MLA_V_DIM)), MLA_KV_RANK ** -0.5),
        'mla_q_norm': gain((nC, MLA_QK_DIM)),
        'mla_k_nope_norm': gain((nC, MLA_NOPE_DIM)),
        'mla_k_rope_norm': gain((nC, MLA_ROPE_DIM)),
        'mla_w_o': nrm((nC, MLA_HEADS * MLA_V_DIM, D), (MLA_HEADS * MLA_V_DIM) ** -0.5),
        'mlp_w1': nrm((DEPTH, D, D_FF), D ** -0.5),
        'mlp_w2': nrm((DEPTH, D_FF, D), D_FF ** -0.5),
    }


def reference(x_prompt, x_sample, c_prompt, c_sample, norm1_g, norm2_g, ada_w, ada_b,
              gqa_w_qkv, gqa_q_norm, gqa_k_norm, gqa_w_o,
              gla_w_in, gla_w_g1, gla_w_g2, gla_b_g, gla_o_norm, gla_w_o,
              mla_w_down, mla_q_a_norm, mla_kv_a_norm, mla_w_uq, mla_w_ukv, mla_q_norm,
              mla_k_nope_norm, mla_k_rope_norm, mla_w_o, mlp_w1, mlp_w2):
    params = dict(
        norm1_g=norm1_g, norm2_g=norm2_g, ada_w=ada_w, ada_b=ada_b,
        gqa_w_qkv=gqa_w_qkv, gqa_q_norm=gqa_q_norm, gqa_k_norm=gqa_k_norm, gqa_w_o=gqa_w_o,
        gla_w_in=gla_w_in, gla_w_g1=gla_w_g1, gla_w_g2=gla_w_g2, gla_b_g=gla_b_g,
        gla_o_norm=gla_o_norm, gla_w_o=gla_w_o,
        mla_w_down=mla_w_down, mla_q_a_norm=mla_q_a_norm, mla_kv_a_norm=mla_kv_a_norm,
        mla_w_uq=mla_w_uq, mla_w_ukv=mla_w_ukv, mla_q_norm=mla_q_norm,
        mla_k_nope_norm=mla_k_nope_norm, mla_k_rope_norm=mla_k_rope_norm, mla_w_o=mla_w_o,
        mlp_w1=mlp_w1, mlp_w2=mlp_w2,
    )
    y_prompt = trunk(x_prompt, c_prompt, params)
    y_sample = trunk(x_sample, c_sample, params)
    return (y_prompt, y_sample)
```

```python
import functools
import math

import numpy as np
import jax
import jax.numpy as jnp
from jax import lax
from jax.experimental import pallas as pl
from jax.experimental.pallas import tpu as pltpu

F32 = jnp.float32
BF16 = jnp.bfloat16

D_MODEL = 2048
DEPTH = 4
GRID_W = 64
ROPE_THETA = 10000.0
NORM_EPS = 1e-6
N_MIXERS = 3

GQA_HEAD_DIM = 128
GQA_Q_HEADS = 16
GQA_KV_HEADS = 4
GQA_GROUP = 4

GLA_HEADS = 4
GLA_DK = 1024
GLA_DV = 2048
GLA_DK_HEAD = 256
GLA_DV_HEAD = 512
GLA_GATE_RANK = 16
GLA_GATE_TAU = 16.0
GLA_CHUNK = 64

MLA_HEADS = 16
MLA_Q_RANK = 512
MLA_KV_RANK = 256
MLA_NOPE_DIM = 128
MLA_ROPE_DIM = 64
MLA_QK_DIM = 192
MLA_V_DIM = 128
MLA_QPAD = 256
MLA_DOWN_PAD = 896

D_FF = 4 * D_MODEL

LANES = 128
VMEM_LIMIT_BYTES = 52 * 1024 * 1024


def _tiles(n_rows, seq):
    tm = min(512, seq)
    assert seq % tm == 0 and n_rows % tm == 0
    return tm


def _cparams(*sem):
    return pltpu.CompilerParams(dimension_semantics=sem, vmem_limit_bytes=VMEM_LIMIT_BYTES)


def _ada_kernel(c_ref, w_ref, b_ref, o_ref):
    c = c_ref[...]
    cond = (c * (1.0 / (1.0 + jnp.exp(-c)))).astype(BF16)
    o_ref[...] = jnp.dot(cond, w_ref[...].astype(BF16), preferred_element_type=F32) + b_ref[...]


def _ada_modulation(c_all, ada_w, ada_b):
    nb, d = c_all.shape
    depth, _, n6 = ada_w.shape
    tn = 1024
    return pl.pallas_call(
        _ada_kernel,
        grid=(depth, n6 // tn),
        in_specs=[
            pl.BlockSpec((nb, d), lambda i, j: (0, 0)),
            pl.BlockSpec((None, d, tn), lambda i, j: (i, 0, j)),
            pl.BlockSpec((None, 1, tn), lambda i, j: (i, 0, j)),
        ],
        out_specs=pl.BlockSpec((None, nb, tn), lambda i, j: (i, 0, j)),
        out_shape=jax.ShapeDtypeStruct((depth, nb, n6), F32),
        compiler_params=_cparams("parallel", "parallel"),
    )(c_all, ada_w, ada_b.reshape(depth, 1, n6))


def _rms(x, g, n):
    ms = jnp.sum(x * x, axis=-1, keepdims=True) * (1.0 / n)
    return x * lax.rsqrt(ms + NORM_EPS) * g


def _norm_mod_matmul_kernel(x_ref, g_ref, sc_ref, sh_ref, w_ref, o_ref, h_ref):
    @pl.when(pl.program_id(1) == 0)
    def _():
        x = x_ref[...].astype(F32)
        y = _rms(x, g_ref[...], x.shape[-1])
        h_ref[...] = (y * (1.0 + sc_ref[...]) + sh_ref[...]).astype(BF16)

    o_ref[...] = jnp.dot(h_ref[...], w_ref[...], preferred_element_type=F32).astype(o_ref.dtype)


def _norm_matmul_kernel(x_ref, g_ref, w_ref, o_ref, h_ref):
    @pl.when(pl.program_id(1) == 0)
    def _():
        x = x_ref[...].astype(F32)
        h_ref[...] = _rms(x, g_ref[...], x.shape[-1]).astype(BF16)

    o_ref[...] = jnp.dot(h_ref[...], w_ref[...], preferred_element_type=F32).astype(o_ref.dtype)


def _norm_mod_matmul(x, g, sc, sh, w, seq, out_dtype, tn=None):
    n, d = x.shape
    nout = w.shape[1]
    tm = _tiles(n, seq)
    tn = nout if tn is None else tn
    tpb = seq // tm
    return pl.pallas_call(
        _norm_mod_matmul_kernel,
        grid=(n // tm, nout // tn),
        in_specs=[
            pl.BlockSpec((tm, d), lambda i, j: (i, 0)),
            pl.BlockSpec((1, d), lambda i, j: (0, 0)),
            pl.BlockSpec((None, 1, d), lambda i, j: (i // tpb, 0, 0)),
            pl.BlockSpec((None, 1, d), lambda i, j: (i // tpb, 0, 0)),
            pl.BlockSpec((d, tn), lambda i, j: (0, j)),
        ],
        out_specs=pl.BlockSpec((tm, tn), lambda i, j: (i, j)),
        out_shape=jax.ShapeDtypeStruct((n, nout), out_dtype),
        scratch_shapes=[pltpu.VMEM((tm, d), BF16)],
        compiler_params=_cparams("parallel", "arbitrary"),
    )(x, g.reshape(1, d), sc, sh, w)


def _norm_matmul(x, col_block, width, g, w, seq, out_dtype):
    n = x.shape[0]
    nout = w.shape[1]
    tm = _tiles(n, seq)
    return pl.pallas_call(
        _norm_matmul_kernel,
        grid=(n // tm, 1),
        in_specs=[
            pl.BlockSpec((tm, width), lambda i, j: (i, col_block)),
            pl.BlockSpec((1, width), lambda i, j: (0, 0)),
            pl.BlockSpec((width, nout), lambda i, j: (0, 0)),
        ],
        out_specs=pl.BlockSpec((tm, nout), lambda i, j: (i, 0)),
        out_shape=jax.ShapeDtypeStruct((n, nout), out_dtype),
        scratch_shapes=[pltpu.VMEM((tm, width), BF16)],
        compiler_params=_cparams("parallel", "arbitrary"),
    )(x, g.reshape(1, width), w)


def _oproj_kernel(a_ref, w_ref, x_ref, gate_ref, o_ref):
    y = jnp.dot(a_ref[...], w_ref[...], preferred_element_type=F32)
    o_ref[...] = x_ref[...] + gate_ref[...] * y


def _oproj_residual(a, w, x, gate, seq):
    n, k = a.shape
    d = w.shape[1]
    tm = _tiles(n, seq)
    tpb = seq // tm
    return pl.pallas_call(
        _oproj_kernel,
        grid=(n // tm,),
        in_specs=[
            pl.BlockSpec((tm, k), lambda i: (i, 0)),
            pl.BlockSpec((k, d), lambda i: (0, 0)),
            pl.BlockSpec((tm, d), lambda i: (i, 0)),
            pl.BlockSpec((None, 1, d), lambda i: (i // tpb, 0, 0)),
        ],
        out_specs=pl.BlockSpec((tm, d), lambda i: (i, 0)),
        out_shape=jax.ShapeDtypeStruct((n, d), F32),
        compiler_params=_cparams("parallel"),
    )(a, w, x, gate)


def _mlp_kernel(x_ref, g_ref, sc_ref, sh_ref, gate_ref, w1_ref, w2_ref, o_ref, h_ref, acc_ref):
    j = pl.program_id(1)

    @pl.when(j == 0)
    def _():
        x = x_ref[...]
        y = _rms(x, g_ref[...], x.shape[-1])
        h_ref[...] = (y * (1.0 + sc_ref[...]) + sh_ref[...]).astype(BF16)
        acc_ref[...] = jnp.zeros_like(acc_ref)

    a = jnp.maximum(jnp.dot(h_ref[...], w1_ref[...], preferred_element_type=F32), 0.0)
    acc_ref[...] += jnp.dot((a * a).astype(BF16), w2_ref[...], preferred_element_type=F32)

    @pl.when(j == pl.num_programs(1) - 1)
    def _():
        o_ref[...] = x_ref[...] + gate_ref[...] * acc_ref[...]


def _mlp_residual(x, g, sc, sh, gate, w1, w2, seq):
    n, d = x.shape
    dff = w1.shape[1]
    tm = _tiles(n, seq)
    tf = min(1024, dff)
    tpb = seq // tm
    mod_spec = pl.BlockSpec((None, 1, d), lambda i, j: (i // tpb, 0, 0))
    return pl.pallas_call(
        _mlp_kernel,
        grid=(n // tm, dff // tf),
        in_specs=[
            pl.BlockSpec((tm, d), lambda i, j: (i, 0)),
            pl.BlockSpec((1, d), lambda i, j: (0, 0)),
            mod_spec, mod_spec, mod_spec,
            pl.BlockSpec((d, tf), lambda i, j: (0, j)),
            pl.BlockSpec((tf, d), lambda i, j: (j, 0)),
        ],
        out_specs=pl.BlockSpec((tm, d), lambda i, j: (i, 0)),
        out_shape=jax.ShapeDtypeStruct((n, d), F32),
        scratch_shapes=[pltpu.VMEM((tm, d), BF16), pltpu.VMEM((tm, d), F32)],
        compiler_params=_cparams("parallel", "arbitrary"),
    )(x, g.reshape(1, d), sc, sh, gate, w1, w2)


def _axial_angles(n_tok, rot_dim):
    rows = n_tok // GRID_W
    row = jnp.repeat(jnp.arange(rows, dtype=F32), GRID_W)
    col = jnp.tile(jnp.arange(GRID_W, dtype=F32), rows)
    n_freq = rot_dim // 4
    inv = ROPE_THETA ** (-jnp.arange(n_freq, dtype=F32) / n_freq)
    ang = jnp.concatenate([row[:, None] * inv, col[:, None] * inv], axis=-1)
    return jnp.cos(ang), jnp.sin(ang)


def _gqa_rope_tables(n_tok):
    cos, sin = _axial_angles(n_tok, GQA_HEAD_DIM)
    return jnp.concatenate([cos, cos], -1), jnp.concatenate([-sin, sin], -1)


def _mla_rope_tables(n_tok):
    cos, sin = _axial_angles(n_tok, MLA_ROPE_DIM)
    z = jnp.zeros_like(cos)
    return jnp.concatenate([cos, z, cos, z], -1), jnp.concatenate([-sin, z, sin, z], -1)


_HALF_SPLIT_128 = np.concatenate([np.arange(0, 128, 2), np.arange(1, 128, 2)])


def _mla_rot_layout(base):
    idx = np.zeros(128, np.int32)
    msk = np.zeros(128, np.float32)
    idx[0:32] = base + np.arange(0, 64, 2)
    idx[64:96] = base + np.arange(1, 64, 2)
    msk[0:32] = 1.0
    msk[64:96] = 1.0
    return idx, msk


def _mla_q_layout():
    ridx, rmsk = _mla_rot_layout(MLA_NOPE_DIM)
    idx = np.concatenate([np.arange(MLA_NOPE_DIM), ridx])
    msk = np.concatenate([np.ones(MLA_NOPE_DIM, np.float32), rmsk])
    return idx, msk


def _rope(x, cos, sin):
    return x * cos + pltpu.roll(x, 64, 1) * sin


def _flash(q, kn_ref, v_load, n_keys, tk):
    m_rows = q.shape[0]

    def body(c, carry):
        m, l, acc = carry
        start = pl.multiple_of(c * tk, tk)
        kc = kn_ref[pl.ds(start, tk), :]
        s = lax.dot_general(q, kc, (((1,), (1,)), ((), ())), preferred_element_type=F32)
        m_new = jnp.maximum(m, jnp.max(s, axis=-1, keepdims=True))
        p = jnp.exp(s - m_new)
        alpha = jnp.exp(m - m_new)
        l = alpha * l + jnp.sum(p, axis=-1, keepdims=True)
        acc = alpha * acc + jnp.dot(p.astype(BF16), v_load(start), preferred_element_type=F32)
        return m_new, l, acc

    init = (jnp.full((m_rows, 1), -jnp.inf, F32), jnp.zeros((m_rows, 1), F32),
            jnp.zeros((m_rows, LANES), F32))
    _, l, acc = lax.fori_loop(0, n_keys // tk, body, init)
    return acc * (1.0 / l)


def _gqa_attn_kernel(q_ref, k_ref, v_ref, cq_ref, sq_ref, ck_ref, sk_ref, qg_ref, kg_ref, o_ref, kn_ref,
                     *, tk, kprep):
    n_keys = k_ref.shape[0]
    tq = q_ref.shape[0]

    @pl.when(pl.program_id(2) == 0)
    def _():
        def prep(c, _):
            rows = pl.ds(pl.multiple_of(c * kprep, kprep), kprep)
            k = _rms(k_ref[rows, :].astype(F32), kg_ref[...], GQA_HEAD_DIM)
            kn_ref[rows, :] = _rope(k, ck_ref[rows, :], sk_ref[rows, :]).astype(BF16)
            return 0
        lax.fori_loop(0, n_keys // kprep, prep, 0)

    scale = GQA_HEAD_DIM ** -0.5
    qs = []
    for h in range(GQA_GROUP):
        q = _rms(q_ref[:, h * GQA_HEAD_DIM:(h + 1) * GQA_HEAD_DIM].astype(F32), qg_ref[...], GQA_HEAD_DIM)
        qs.append((_rope(q, cq_ref[...], sq_ref[...]) * scale).astype(BF16))
    q_all = jnp.concatenate(qs, axis=0)
    out = _flash(q_all, kn_ref, lambda start: v_ref[pl.ds(start, tk), :], n_keys, tk)
    for h in range(GQA_GROUP):
        o_ref[:, h * GQA_HEAD_DIM:(h + 1) * GQA_HEAD_DIM] = out[h * tq:(h + 1) * tq].astype(o_ref.dtype)


def _gqa_attention(qkv, cos, sin, q_gain, k_gain):
    b, t, _ = qkv.shape
    tq = min(128, t)
    tk = min(512, t)
    kprep = min(1024, t)
    gw = GQA_GROUP * GQA_HEAD_DIM
    k_col0 = GQA_Q_HEADS
    v_col0 = GQA_Q_HEADS + GQA_KV_HEADS
    tab_q = pl.BlockSpec((tq, LANES), lambda bi, g, qi: (qi, 0))
    tab_k = pl.BlockSpec((t, LANES), lambda bi, g, qi: (0, 0))
    gain = pl.BlockSpec((1, LANES), lambda bi, g, qi: (0, 0))
    return pl.pallas_call(
        functools.partial(_gqa_attn_kernel, tk=tk, kprep=kprep),
        grid=(b, GQA_KV_HEADS, t // tq),
        in_specs=[
            pl.BlockSpec((None, tq, gw), lambda bi, g, qi: (bi, qi, g)),
            pl.BlockSpec((None, t, GQA_HEAD_DIM), lambda bi, g, qi: (bi, 0, k_col0 + g)),
            pl.BlockSpec((None, t, GQA_HEAD_DIM), lambda bi, g, qi: (bi, 0, v_col0 + g)),
            tab_q, tab_q, tab_k, tab_k, gain, gain,
        ],
        out_specs=pl.BlockSpec((None, tq, gw), lambda bi, g, qi: (bi, qi, g)),
        out_shape=jax.ShapeDtypeStruct((b, t, GQA_Q_HEADS * GQA_HEAD_DIM), BF16),
        scratch_shapes=[pltpu.VMEM((t, GQA_HEAD_DIM), BF16)],
        compiler_params=_cparams("parallel", "parallel", "arbitrary"),
    )(qkv, qkv, qkv, cos, sin, cos, sin, q_gain.reshape(1, LANES), k_gain.reshape(1, LANES))


def _mla_attn_kernel(q_ref, kv_ref, kr_ref, cq_ref, sq_ref, ck_ref, sk_ref, qg_ref, kng_ref, krg_ref,
                     o_ref, kn_ref, *, tk, kprep):
    n_keys = kv_ref.shape[0]

    @pl.when(pl.program_id(2) == 0)
    def _():
        def prep(c, _):
            rows = pl.ds(pl.multiple_of(c * kprep, kprep), kprep)
            kn = _rms(kv_ref[rows, 0:MLA_NOPE_DIM].astype(F32), kng_ref[...], MLA_NOPE_DIM)
            kn_ref[rows, 0:MLA_NOPE_DIM] = kn.astype(BF16)
            kr = _rms(kr_ref[rows, :].astype(F32), krg_ref[...], MLA_ROPE_DIM)
            kn_ref[rows, MLA_NOPE_DIM:MLA_QPAD] = _rope(kr, ck_ref[rows, :], sk_ref[rows, :]).astype(BF16)
            return 0
        lax.fori_loop(0, n_keys // kprep, prep, 0)

    scale = MLA_QK_DIM ** -0.5
    q = q_ref[...].astype(F32)
    g = qg_ref[...]
    inv = lax.rsqrt(jnp.sum(q * q, axis=-1, keepdims=True) * (1.0 / MLA_QK_DIM) + NORM_EPS)
    q_nope = q[:, 0:MLA_NOPE_DIM] * inv * g[:, 0:MLA_NOPE_DIM]
    q_rot = _rope(q[:, MLA_NOPE_DIM:MLA_QPAD] * inv * g[:, MLA_NOPE_DIM:MLA_QPAD], cq_ref[...], sq_ref[...])
    q_all = (jnp.concatenate([q_nope, q_rot], axis=1) * scale).astype(BF16)
    out = _flash(q_all, kn_ref, lambda start: kv_ref[pl.ds(start, tk), MLA_NOPE_DIM:MLA_QPAD], n_keys, tk)
    o_ref[...] = out.astype(o_ref.dtype)


def _mla_attention(q, kv, down, cos, sin, q_gain, kn_gain, kr_gain):
    b, t, _ = q.shape
    tq = min(512, t)
    tk = min(512, t)
    kprep = min(1024, t)
    kr_col = (MLA_Q_RANK + MLA_KV_RANK) // LANES
    tab_q = pl.BlockSpec((tq, LANES), lambda bi, h, qi: (qi, 0))
    tab_k = pl.BlockSpec((t, LANES), lambda bi, h, qi: (0, 0))
    return pl.pallas_call(
        functools.partial(_mla_attn_kernel, tk=tk, kprep=kprep),
        grid=(b, MLA_HEADS, t // tq),
        in_specs=[
            pl.BlockSpec((None, tq, MLA_QPAD), lambda bi, h, qi: (bi, qi, h)),
            pl.BlockSpec((None, t, MLA_QPAD), lambda bi, h, qi: (bi, 0, h)),
            pl.BlockSpec((None, t, LANES), lambda bi, h, qi: (bi, 0, kr_col)),
            tab_q, tab_q, tab_k, tab_k,
            pl.BlockSpec((1, MLA_QPAD), lambda bi, h, qi: (0, 0)),
            pl.BlockSpec((1, LANES), lambda bi, h, qi: (0, 0)),
            pl.BlockSpec((1, LANES), lambda bi, h, qi: (0, 0)),
        ],
        out_specs=pl.BlockSpec((None, tq, MLA_V_DIM), lambda bi, h, qi: (bi, qi, h)),
        out_shape=jax.ShapeDtypeStruct((b, t, MLA_HEADS * MLA_V_DIM), BF16),
        scratch_shapes=[pltpu.VMEM((t, MLA_QPAD), BF16)],
        compiler_params=_cparams("parallel", "parallel", "arbitrary"),
    )(q, kv, down, cos, sin, cos, sin, q_gain.reshape(1, MLA_QPAD), kn_gain.reshape(1, LANES),
      kr_gain.reshape(1, LANES))


def _split3(x):
    hi = x.astype(BF16)
    r1 = x - hi.astype(F32)
    mid = r1.astype(BF16)
    lo = (r1 - mid.astype(F32)).astype(BF16)
    return hi, mid, lo


def _gla_kernel(*refs, reverse, n_chunks):
    if reverse:
        (q_ref, k_ref, v_ref, low_ref, wg_ref, bg_ref, of_ref, r_ref, on_ref, o_ref, s_ref) = refs
    else:
        (q_ref, k_ref, v_ref, low_ref, wg_ref, bg_ref, o_ref, s_ref) = refs
    L = GLA_CHUNK

    @pl.when(pl.program_id(2) == 0)
    def _():
        s_ref[...] = jnp.zeros_like(s_ref)

    row = lax.broadcasted_iota(jnp.int32, (L, L), 0)
    col = lax.broadcasted_iota(jnp.int32, (L, L), 1)
    keep = (col >= row) if reverse else (col <= row)
    tri = jnp.where(keep, 1.0, 0.0).astype(BF16)
    ones = jnp.ones((L, LANES), BF16)
    tn_dims = (((0,), (0,)), ((), ()))
    nt_dims = (((1,), (1,)), ((), ()))

    order = range(n_chunks - 1, -1, -1) if reverse else range(n_chunks)
    for c in order:
        rows = slice(c * L, (c + 1) * L)
        z = jnp.dot(low_ref[rows, :].astype(BF16), wg_ref[...], preferred_element_type=F32) + bg_ref[...]
        la = (jnp.minimum(z, 0.0) - jnp.log1p(jnp.exp(-jnp.abs(z)))) * (1.0 / GLA_GATE_TAU)
        parts = _split3(la)
        cum = sum(jnp.dot(tri, p, preferred_element_type=F32) for p in parts)
        tot_col = sum(lax.dot_general(p, ones, tn_dims, preferred_element_type=F32) for p in parts)
        edge = (0 if reverse else L - 1)
        tot_row = cum[edge:edge + 1, :]

        q = q_ref[rows, :].astype(F32) * (GLA_DK_HEAD ** -0.5)
        k = k_ref[rows, :].astype(F32)
        v = v_ref[rows, :]
        qd = (q * jnp.exp(cum)).astype(BF16)
        kd = (k * jnp.exp(-cum)).astype(BF16)
        ke = (k * jnp.exp(tot_row - cum)).astype(BF16)
        a = lax.dot_general(qd, kd, nt_dims, preferred_element_type=F32)
        a = jnp.where(keep, a, 0.0).astype(BF16)
        s_old = s_ref[...]
        o = (jnp.dot(a, v, preferred_element_type=F32)
             + jnp.dot(qd, s_old.astype(BF16), preferred_element_type=F32))
        de = jnp.exp(tot_col)
        de = jnp.concatenate([de] * (GLA_DV_HEAD // LANES), axis=1)
        s_ref[...] = de * s_old + lax.dot_general(ke, v, tn_dims, preferred_element_type=F32)

        if reverse:
            o = o + of_ref[rows, :]
            o = _rms(o, on_ref[...], GLA_DV_HEAD)
            r = r_ref[rows, :].astype(F32)
            o = o * (r * (1.0 / (1.0 + jnp.exp(-r))))
        o_ref[rows, :] = o.astype(o_ref.dtype)


def _gla_direction(proj, low, wg, bg, o_fwd, o_norm, reverse):
    b, t, _ = proj.shape
    tb = min(512, t)
    nblk = t // tb
    blk = (lambda i: nblk - 1 - i) if reverse else (lambda i: i)
    k_col0 = GLA_DK // GLA_DK_HEAD
    v_col0 = 2 * GLA_DK // GLA_DV_HEAD
    r_col0 = (2 * GLA_DK + GLA_DV) // GLA_DV_HEAD
    in_specs = [
        pl.BlockSpec((None, tb, GLA_DK_HEAD), lambda bi, h, i: (bi, blk(i), h)),
        pl.BlockSpec((None, tb, GLA_DK_HEAD), lambda bi, h, i: (bi, blk(i), k_col0 + h)),
        pl.BlockSpec((None, tb, GLA_DV_HEAD), lambda bi, h, i: (bi, blk(i), v_col0 + h)),
        pl.BlockSpec((None, tb, LANES), lambda bi, h, i: (bi, blk(i), 0)),
        pl.BlockSpec((LANES, GLA_DK_HEAD), lambda bi, h, i: (0, h)),
        pl.BlockSpec((1, GLA_DK_HEAD), lambda bi, h, i: (0, h)),
    ]
    args = [proj, proj, proj, low, wg, bg]
    if reverse:
        in_specs += [
            pl.BlockSpec((None, tb, GLA_DV_HEAD), lambda bi, h, i: (bi, blk(i), h)),
            pl.BlockSpec((None, tb, GLA_DV_HEAD), lambda bi, h, i: (bi, blk(i), r_col0 + h)),
            pl.BlockSpec((1, GLA_DV_HEAD), lambda bi, h, i: (0, 0)),
        ]
        args += [o_fwd, proj, o_norm.reshape(1, GLA_DV_HEAD)]
    return pl.pallas_call(
        functools.partial(_gla_kernel, reverse=reverse, n_chunks=tb // GLA_CHUNK),
        grid=(b, GLA_HEADS, nblk),
        in_specs=in_specs,
        out_specs=pl.BlockSpec((None, tb, GLA_DV_HEAD), lambda bi, h, i: (bi, blk(i), h)),
        out_shape=jax.ShapeDtypeStruct((b, t, GLA_DV), BF16 if reverse else F32),
        scratch_shapes=[pltpu.VMEM((GLA_DK_HEAD, GLA_DV_HEAD), F32)],
        compiler_params=_cparams("parallel", "parallel", "arbitrary"),
    )(*args)


def _prep_gqa(w_qkv, q_norm, k_norm):
    n_rot = GQA_Q_HEADS + GQA_KV_HEADS
    cols = (np.arange(n_rot)[:, None] * GQA_HEAD_DIM + _HALF_SPLIT_128[None, :]).reshape(-1)
    cols = np.concatenate([cols, np.arange(n_rot * GQA_HEAD_DIM, w_qkv.shape[1])])
    return w_qkv[:, cols].astype(BF16), q_norm[_HALF_SPLIT_128], k_norm[_HALF_SPLIT_128]


def _prep_mla(w_down, w_uq, q_norm, k_rope_norm):
    qidx, qmsk = _mla_q_layout()
    cols = (np.arange(MLA_HEADS)[:, None] * MLA_QK_DIM + qidx[None, :]).reshape(-1)
    w_uq_p = (w_uq[:, cols] * jnp.asarray(np.tile(qmsk, MLA_HEADS))).astype(BF16)
    q_gain = q_norm[qidx] * jnp.asarray(qmsk)
    ridx, rmsk = _mla_rot_layout(0)
    kr0 = MLA_Q_RANK + MLA_KV_RANK
    w_kr = w_down[:, kr0 + ridx] * jnp.asarray(rmsk)
    w_down_p = jnp.concatenate([w_down[:, :kr0], w_kr], axis=1).astype(BF16)
    kr_gain = k_rope_norm[ridx] * jnp.asarray(rmsk)
    return w_down_p, w_uq_p, q_gain, kr_gain


def _prep_gla_gates(w_g1, w_g2):
    r = GLA_GATE_RANK
    d = w_g1.shape[1]
    w1 = jnp.zeros((d, LANES), F32).at[:, 0:r].set(w_g1[0]).at[:, r:2 * r].set(w_g1[1])
    w2 = jnp.zeros((2, LANES, GLA_DK), F32).at[0, 0:r].set(w_g2[0]).at[1, r:2 * r].set(w_g2[1])
    return w1.astype(BF16), w2.astype(BF16)


def _trunk(x3, mod, p):
    b, t, d = x3.shape
    n = b * t
    x = x3.reshape(n, d)
    for i in range(DEPTH):
        m = mod[i].reshape(b, 6, 1, d)
        sh1, sc1, g1, sh2, sc2, g2 = (m[:, s] for s in range(6))
        kind, j = i % N_MIXERS, i // N_MIXERS
        if kind == 0:
            qkv = _norm_mod_matmul(x, p['norm1_g'][i], sc1, sh1, p['gqa_w_qkv'][j], t, BF16)
            cos, sin = _gqa_rope_tables(t)
            a = _gqa_attention(qkv.reshape(b, t, -1), cos, sin, p['gqa_q_norm'][j], p['gqa_k_norm'][j])
            x = _oproj_residual(a.reshape(n, -1), p['gqa_w_o'][j], x, g1, t)
        elif kind == 1:
            proj = _norm_mod_matmul(x, p['norm1_g'][i], sc1, sh1, p['gla_w_in'][j], t, BF16, tn=2048)
            low = _norm_mod_matmul(x, p['norm1_g'][i], sc1, sh1, p['gla_w_g1'][j], t, F32)
            proj3, low3 = proj.reshape(b, t, -1), low.reshape(b, t, -1)
            bg = p['gla_b_g'][j]
            o_f = _gla_direction(proj3, low3, p['gla_w_g2'][j][0], bg[0:1], None, None, reverse=False)
            a = _gla_direction(proj3, low3, p['gla_w_g2'][j][1], bg[1:2], o_f, p['gla_o_norm'][j], reverse=True)
            x = _oproj_residual(a.reshape(n, -1), p['gla_w_o'][j], x, g1, t)
        else:
            down = _norm_mod_matmul(x, p['norm1_g'][i], sc1, sh1, p['mla_w_down'][j], t, F32)
            q = _norm_matmul(down, 0, MLA_Q_RANK, p['mla_q_a_norm'][j], p['mla_w_uq'][j], t, BF16)
            kv = _norm_matmul(down, MLA_Q_RANK // MLA_KV_RANK, MLA_KV_RANK, p['mla_kv_a_norm'][j],
                              p['mla_w_ukv'][j], t, BF16)
            cos, sin = _mla_rope_tables(t)
            a = _mla_attention(q.reshape(b, t, -1), kv.reshape(b, t, -1), down.reshape(b, t, -1), cos, sin,
                               p['mla_q_gain'][j], p['mla_k_nope_norm'][j], p['mla_kr_gain'][j])
            x = _oproj_residual(a.reshape(n, -1), p['mla_w_o'][j], x, g1, t)
        x = _mlp_residual(x, p['norm2_g'][i], sc2, sh2, g2, p['mlp_w1'][i], p['mlp_w2'][i], t)
    return x.reshape(b, t, d)


def kernel(x_prompt, x_sample, c_prompt, c_sample, norm1_g, norm2_g, ada_w, ada_b, gqa_w_qkv, gqa_q_norm, gqa_k_norm, gqa_w_o, gla_w_in, gla_w_g1, gla_w_g2, gla_b_g, gla_o_norm, gla_w_o, mla_w_down, mla_q_a_norm, mla_kv_a_norm, mla_w_uq, mla_w_ukv, mla_q_norm, mla_k_nope_norm, mla_k_rope_norm, mla_w_o, mlp_w1, mlp_w2):
    nA, nB, nC = gqa_w_qkv.shape[0], gla_w_in.shape[0], mla_w_down.shape[0]
    gqa = [_prep_gqa(gqa_w_qkv[j], gqa_q_norm[j], gqa_k_norm[j]) for j in range(nA)]
    mla = [_prep_mla(mla_w_down[j], mla_w_uq[j], mla_q_norm[j], mla_k_rope_norm[j]) for j in range(nC)]
    gates = [_prep_gla_gates(gla_w_g1[j], gla_w_g2[j]) for j in range(nB)]
    p = dict(
        norm1_g=norm1_g, norm2_g=norm2_g,
        gqa_w_qkv=[g[0] for g in gqa], gqa_q_norm=[g[1] for g in gqa], gqa_k_norm=[g[2] for g in gqa],
        gqa_w_o=gqa_w_o.astype(BF16),
        gla_w_in=gla_w_in.astype(BF16), gla_w_g1=[g[0] for g in gates], gla_w_g2=[g[1] for g in gates],
        gla_b_g=gla_b_g, gla_o_norm=gla_o_norm, gla_w_o=gla_w_o.astype(BF16),
        mla_w_down=[m[0] for m in mla], mla_w_uq=[m[1] for m in mla], mla_q_gain=[m[2] for m in mla],
        mla_kr_gain=[m[3] for m in mla], mla_q_a_norm=mla_q_a_norm, mla_kv_a_norm=mla_kv_a_norm,
        mla_w_ukv=mla_w_ukv.astype(BF16), mla_k_nope_norm=mla_k_nope_norm, mla_w_o=mla_w_o.astype(BF16),
        mlp_w1=mlp_w1.astype(BF16), mlp_w2=mlp_w2.astype(BF16),
    )
    bp, bs = c_prompt.shape[0], c_sample.shape[0]
    nb_pad = -(-(bp + bs) // 16) * 16
    c_all = jnp.concatenate([c_prompt, c_sample, jnp.zeros((nb_pad - bp - bs, c_prompt.shape[1]), F32)], axis=0)
    mod = _ada_modulation(c_all, ada_w, ada_b)
    y_prompt = _trunk(x_prompt, mod[:, :bp], p)
    y_sample = _trunk(x_sample, mod[:, bp:bp + bs], p)
    return (y_prompt, y_sample)
```

```python
import functools
import math

import numpy as np
import jax
import jax.numpy as jnp
from jax import lax
from jax.experimental import pallas as pl
from jax.experimental.pallas import tpu as pltpu

F32 = jnp.float32
BF16 = jnp.bfloat16

D_MODEL = 2048
DEPTH = 4
GRID_W = 64
ROPE_THETA = 10000.0
NORM_EPS = 1e-6
N_MIXERS = 3

GQA_HEAD_DIM = 128
GQA_Q_HEADS = 16
GQA_KV_HEADS = 4
GQA_GROUP = 4

GLA_HEADS = 4
GLA_DK = 1024
GLA_DV = 2048
GLA_DK_HEAD = 256
GLA_DV_HEAD = 512
GLA_GATE_RANK = 16
GLA_GATE_TAU = 16.0
GLA_CHUNK = 64

MLA_HEADS = 16
MLA_Q_RANK = 512
MLA_KV_RANK = 256
MLA_NOPE_DIM = 128
MLA_ROPE_DIM = 64
MLA_QK_DIM = 192
MLA_V_DIM = 128
MLA_QPAD = 256
MLA_DOWN_PAD = 896

D_FF = 4 * D_MODEL

LANES = 128
VMEM_LIMIT_BYTES = 52 * 1024 * 1024


def _tiles(n_rows, seq):
    tm = min(512, seq)
    assert seq % tm == 0 and n_rows % tm == 0
    return tm


def _cparams(*sem):
    return pltpu.CompilerParams(dimension_semantics=sem, vmem_limit_bytes=VMEM_LIMIT_BYTES)


def _ada_kernel(c_ref, w_ref, b_ref, o_ref):
    c = c_ref[...]
    cond = (c * (1.0 / (1.0 + jnp.exp(-c)))).astype(BF16)
    o_ref[...] = jnp.dot(cond, w_ref[...].astype(BF16), preferred_element_type=F32) + b_ref[...]


def _ada_modulation(c_all, ada_w, ada_b):
    nb, d = c_all.shape
    depth, _, n6 = ada_w.shape
    tn = 1024
    return pl.pallas_call(
        _ada_kernel,
        grid=(depth, n6 // tn),
        in_specs=[
            pl.BlockSpec((nb, d), lambda i, j: (0, 0)),
            pl.BlockSpec((None, d, tn), lambda i, j: (i, 0, j)),
            pl.BlockSpec((None, 1, tn), lambda i, j: (i, 0, j)),
        ],
        out_specs=pl.BlockSpec((None, nb, tn), lambda i, j: (i, 0, j)),
        out_shape=jax.ShapeDtypeStruct((depth, nb, n6), F32),
        compiler_params=_cparams("parallel", "parallel"),
    )(c_all, ada_w, ada_b.reshape(depth, 1, n6))


def _rms(x, g, n):
    ms = jnp.sum(x * x, axis=-1, keepdims=True) * (1.0 / n)
    return x * lax.rsqrt(ms + NORM_EPS) * g


def _norm_mod_matmul_kernel(x_ref, g_ref, sc_ref, sh_ref, w_ref, o_ref, h_ref):
    @pl.when(pl.program_id(1) == 0)
    def _():
        x = x_ref[...].astype(F32)
        y = _rms(x, g_ref[...], x.shape[-1])
        h_ref[...] = (y * (1.0 + sc_ref[...]) + sh_ref[...]).astype(BF16)

    o_ref[...] = jnp.dot(h_ref[...], w_ref[...], preferred_element_type=F32).astype(o_ref.dtype)


def _norm_matmul_kernel(x_ref, g_ref, w_ref, o_ref, h_ref):
    @pl.when(pl.program_id(1) == 0)
    def _():
        x = x_ref[...].astype(F32)
        h_ref[...] = _rms(x, g_ref[...], x.shape[-1]).astype(BF16)

    o_ref[...] = jnp.dot(h_ref[...], w_ref[...], preferred_element_type=F32).astype(o_ref.dtype)


def _norm_mod_matmul(x, g, sc, sh, w, seq, out_dtype, tn=None):
    n, d = x.shape
    nout = w.shape[1]
    tm = _tiles(n, seq)
    tn = nout if tn is None else tn
    tpb = seq // tm
    return pl.pallas_call(
        _norm_mod_matmul_kernel,
        grid=(n // tm, nout // tn),
        in_specs=[
            pl.BlockSpec((tm, d), lambda i, j: (i, 0)),
            pl.BlockSpec((1, d), lambda i, j: (0, 0)),
            pl.BlockSpec((None, 1, d), lambda i, j: (i // tpb, 0, 0)),
            pl.BlockSpec((None, 1, d), lambda i, j: (i // tpb, 0, 0)),
            pl.BlockSpec((d, tn), lambda i, j: (0, j)),
        ],
        out_specs=pl.BlockSpec((tm, tn), lambda i, j: (i, j)),
        out_shape=jax.ShapeDtypeStruct((n, nout), out_dtype),
        scratch_shapes=[pltpu.VMEM((tm, d), BF16)],
        compiler_params=_cparams("parallel", "arbitrary"),
    )(x, g.reshape(1, d), sc, sh, w)


def _norm_matmul(x, col_block, width, g, w, seq, out_dtype):
    n = x.shape[0]
    nout = w.shape[1]
    tm = _tiles(n, seq)
    return pl.pallas_call(
        _norm_matmul_kernel,
        grid=(n // tm, 1),
        in_specs=[
            pl.BlockSpec((tm, width), lambda i, j: (i, col_block)),
            pl.BlockSpec((1, width), lambda i, j: (0, 0)),
            pl.BlockSpec((width, nout), lambda i, j: (0, 0)),
        ],
        out_specs=pl.BlockSpec((tm, nout), lambda i, j: (i, 0)),
        out_shape=jax.ShapeDtypeStruct((n, nout), out_dtype),
        scratch_shapes=[pltpu.VMEM((tm, width), BF16)],
        compiler_params=_cparams("parallel", "arbitrary"),
    )(x, g.reshape(1, width), w)


def _oproj_kernel(a_ref, w_ref, x_ref, gate_ref, o_ref):
    y = jnp.dot(a_ref[...], w_ref[...], preferred_element_type=F32)
    o_ref[...] = x_ref[...] + gate_ref[...] * y


def _oproj_residual(a, w, x, gate, seq):
    n, k = a.shape
    d = w.shape[1]
    tm = _tiles(n, seq)
    tpb = seq // tm
    return pl.pallas_call(
        _oproj_kernel,
        grid=(n // tm,),
        in_specs=[
            pl.BlockSpec((tm, k), lambda i: (i, 0)),
            pl.BlockSpec((k, d), lambda i: (0, 0)),
            pl.BlockSpec((tm, d), lambda i: (i, 0)),
            pl.BlockSpec((None, 1, d), lambda i: (i // tpb, 0, 0)),
        ],
        out_specs=pl.BlockSpec((tm, d), lambda i: (i, 0)),
        out_shape=jax.ShapeDtypeStruct((n, d), F32),
        compiler_params=_cparams("parallel"),
    )(a, w, x, gate)


def _mlp_kernel(x_ref, g_ref, sc_ref, sh_ref, gate_ref, w1_ref, w2_ref, o_ref, h_ref, acc_ref):
    j = pl.program_id(1)

    @pl.when(j == 0)
    def _():
        x = x_ref[...]
        y = _rms(x, g_ref[...], x.shape[-1])
        h_ref[...] = (y * (1.0 + sc_ref[...]) + sh_ref[...]).astype(BF16)
        acc_ref[...] = jnp.zeros_like(acc_ref)

    a = jnp.maximum(jnp.dot(h_ref[...], w1_ref[...], preferred_element_type=F32), 0.0)
    acc_ref[...] += jnp.dot((a * a).astype(BF16), w2_ref[...], preferred_element_type=F32)

    @pl.when(j == pl.num_programs(1) - 1)
    def _():
        o_ref[...] = x_ref[...] + gate_ref[...] * acc_ref[...]


def _mlp_residual(x, g, sc, sh, gate, w1, w2, seq):
    n, d = x.shape
    dff = w1.shape[1]
    tm = _tiles(n, seq)
    tf = min(1024, dff)
    tpb = seq // tm
    mod_spec = pl.BlockSpec((None, 1, d), lambda i, j: (i // tpb, 0, 0))
    return pl.pallas_call(
        _mlp_kernel,
        grid=(n // tm, dff // tf),
        in_specs=[
            pl.BlockSpec((tm, d), lambda i, j: (i, 0)),
            pl.BlockSpec((1, d), lambda i, j: (0, 0)),
            mod_spec, mod_spec, mod_spec,
            pl.BlockSpec((d, tf), lambda i, j: (0, j)),
            pl.BlockSpec((tf, d), lambda i, j: (j, 0)),
        ],
        out_specs=pl.BlockSpec((tm, d), lambda i, j: (i, 0)),
        out_shape=jax.ShapeDtypeStruct((n, d), F32),
        scratch_shapes=[pltpu.VMEM((tm, d), BF16), pltpu.VMEM((tm, d), F32)],
        compiler_params=_cparams("parallel", "arbitrary"),
    )(x, g.reshape(1, d), sc, sh, gate, w1, w2)


def _axial_angles(n_tok, rot_dim):
    rows = n_tok // GRID_W
    row = jnp.repeat(jnp.arange(rows, dtype=F32), GRID_W)
    col = jnp.tile(jnp.arange(GRID_W, dtype=F32), rows)
    n_freq = rot_dim // 4
    inv = ROPE_THETA ** (-jnp.arange(n_freq, dtype=F32) / n_freq)
    ang = jnp.concatenate([row[:, None] * inv, col[:, None] * inv], axis=-1)
    return jnp.cos(ang), jnp.sin(ang)


def _gqa_rope_tables(n_tok):
    cos, sin = _axial_angles(n_tok, GQA_HEAD_DIM)
    return jnp.concatenate([cos, cos], -1), jnp.concatenate([-sin, sin], -1)


def _mla_rope_tables(n_tok):
    cos, sin = _axial_angles(n_tok, MLA_ROPE_DIM)
    z = jnp.zeros_like(cos)
    return jnp.concatenate([cos, z, cos, z], -1), jnp.concatenate([-sin, z, sin, z], -1)


_HALF_SPLIT_128 = np.concatenate([np.arange(0, 128, 2), np.arange(1, 128, 2)])


def _mla_rot_layout(base):
    idx = np.zeros(128, np.int32)
    msk = np.zeros(128, np.float32)
    idx[0:32] = base + np.arange(0, 64, 2)
    idx[64:96] = base + np.arange(1, 64, 2)
    msk[0:32] = 1.0
    msk[64:96] = 1.0
    return idx, msk


def _mla_q_layout():
    ridx, rmsk = _mla_rot_layout(MLA_NOPE_DIM)
    idx = np.concatenate([np.arange(MLA_NOPE_DIM), ridx])
    msk = np.concatenate([np.ones(MLA_NOPE_DIM, np.float32), rmsk])
    return idx, msk


def _rope(x, cos, sin):
    return x * cos + pltpu.roll(x, 64, 1) * sin


LOG2E = math.log2(math.e)


def _flash_t(q_t, kn_ref, vt_ref, n_keys, tk):
    m_cols = q_t.shape[1]
    m = jnp.full((1, m_cols), -jnp.inf, F32)
    l8 = jnp.zeros((8, m_cols), F32)
    acc = jnp.zeros((vt_ref.shape[0], m_cols), F32)
    for c in range(n_keys // tk):
        s = jnp.dot(kn_ref[c * tk:(c + 1) * tk, :], q_t, preferred_element_type=F32)
        m_new = jnp.maximum(m, jnp.max(s, axis=0, keepdims=True))
        alpha = jnp.exp2(m - m_new)
        p = jnp.exp2(s - m_new)
        l8 = alpha * l8 + jnp.sum(p.reshape(tk // 8, 8, m_cols), axis=0)
        acc = alpha * acc + jnp.dot(vt_ref[:, c * tk:(c + 1) * tk], p.astype(BF16),
                                    preferred_element_type=F32)
        m = m_new
    return acc * (1.0 / jnp.sum(l8, axis=0, keepdims=True))


def _store_vt(vt_ref, v_load, n_keys, blk):
    for c in range(n_keys // blk):
        vt_ref[:, c * blk:(c + 1) * blk] = v_load(c * blk, blk).astype(F32).T.astype(BF16)


def _gqa_attn_kernel(q_ref, k_ref, v_ref, cq_ref, sq_ref, ck_ref, sk_ref, qg_ref, kg_ref, o_ref, kn_ref, vt_ref,
                     *, tk, kprep):
    n_keys = k_ref.shape[0]
    tq = q_ref.shape[0]

    @pl.when(pl.program_id(2) == 0)
    def _():
        def prep(c, _):
            rows = pl.ds(pl.multiple_of(c * kprep, kprep), kprep)
            k = _rms(k_ref[rows, :].astype(F32), kg_ref[...], GQA_HEAD_DIM)
            kn_ref[rows, :] = _rope(k, ck_ref[rows, :], sk_ref[rows, :]).astype(BF16)
            return 0
        lax.fori_loop(0, n_keys // kprep, prep, 0)
        _store_vt(vt_ref, lambda r0, nr: v_ref[r0:r0 + nr, :], n_keys, kprep)

    scale = GQA_HEAD_DIM ** -0.5 * LOG2E
    qs = []
    for h in range(GQA_GROUP):
        q = _rms(q_ref[:, h * GQA_HEAD_DIM:(h + 1) * GQA_HEAD_DIM].astype(F32), qg_ref[...], GQA_HEAD_DIM)
        qs.append((_rope(q, cq_ref[...], sq_ref[...]) * scale).T.astype(BF16))
    q_t = jnp.concatenate(qs, axis=1)
    out_t = _flash_t(q_t, kn_ref, vt_ref, n_keys, tk)
    for h in range(GQA_GROUP):
        o_ref[:, h * GQA_HEAD_DIM:(h + 1) * GQA_HEAD_DIM] = out_t[:, h * tq:(h + 1) * tq].T.astype(o_ref.dtype)


def _gqa_attention(qkv, cos, sin, q_gain, k_gain):
    b, t, _ = qkv.shape
    tq = min(256, t)
    tk = min(512, t)
    kprep = min(512, t)
    gw = GQA_GROUP * GQA_HEAD_DIM
    k_col0 = GQA_Q_HEADS
    v_col0 = GQA_Q_HEADS + GQA_KV_HEADS
    tab_q = pl.BlockSpec((tq, LANES), lambda bi, g, qi: (qi, 0))
    tab_k = pl.BlockSpec((t, LANES), lambda bi, g, qi: (0, 0))
    gain = pl.BlockSpec((1, LANES), lambda bi, g, qi: (0, 0))
    return pl.pallas_call(
        functools.partial(_gqa_attn_kernel, tk=tk, kprep=kprep),
        grid=(b, GQA_KV_HEADS, t // tq),
        in_specs=[
            pl.BlockSpec((None, tq, gw), lambda bi, g, qi: (bi, qi, g)),
            pl.BlockSpec((None, t, GQA_HEAD_DIM), lambda bi, g, qi: (bi, 0, k_col0 + g)),
            pl.BlockSpec((None, t, GQA_HEAD_DIM), lambda bi, g, qi: (bi, 0, v_col0 + g)),
            tab_q, tab_q, tab_k, tab_k, gain, gain,
        ],
        out_specs=pl.BlockSpec((None, tq, gw), lambda bi, g, qi: (bi, qi, g)),
        out_shape=jax.ShapeDtypeStruct((b, t, GQA_Q_HEADS * GQA_HEAD_DIM), BF16),
        scratch_shapes=[pltpu.VMEM((t, GQA_HEAD_DIM), BF16), pltpu.VMEM((GQA_HEAD_DIM, t), BF16)],
        compiler_params=_cparams("parallel", "parallel", "arbitrary"),
    )(qkv, qkv, qkv, cos, sin, cos, sin, q_gain.reshape(1, LANES), k_gain.reshape(1, LANES))


def _mla_attn_kernel(q_ref, kv_ref, kr_ref, cq_ref, sq_ref, ck_ref, sk_ref, qg_ref, kng_ref, krg_ref,
                     o_ref, kn_ref, vt_ref, *, tk, kprep):
    n_keys = kv_ref.shape[0]

    @pl.when(pl.program_id(2) == 0)
    def _():
        def prep(c, _):
            rows = pl.ds(pl.multiple_of(c * kprep, kprep), kprep)
            kn = _rms(kv_ref[rows, 0:MLA_NOPE_DIM].astype(F32), kng_ref[...], MLA_NOPE_DIM)
            kn_ref[rows, 0:MLA_NOPE_DIM] = kn.astype(BF16)
            kr = _rms(kr_ref[rows, :].astype(F32), krg_ref[...], MLA_ROPE_DIM)
            kn_ref[rows, MLA_NOPE_DIM:MLA_QPAD] = _rope(kr, ck_ref[rows, :], sk_ref[rows, :]).astype(BF16)
            return 0
        lax.fori_loop(0, n_keys // kprep, prep, 0)
        _store_vt(vt_ref, lambda r0, nr: kv_ref[r0:r0 + nr, MLA_NOPE_DIM:MLA_QPAD], n_keys, kprep)

    scale = MLA_QK_DIM ** -0.5 * LOG2E
    q = q_ref[...].astype(F32)
    g = qg_ref[...]
    inv = lax.rsqrt(jnp.sum(q * q, axis=-1, keepdims=True) * (1.0 / MLA_QK_DIM) + NORM_EPS)
    q_nope = q[:, 0:MLA_NOPE_DIM] * inv * g[:, 0:MLA_NOPE_DIM]
    q_rot = _rope(q[:, MLA_NOPE_DIM:MLA_QPAD] * inv * g[:, MLA_NOPE_DIM:MLA_QPAD], cq_ref[...], sq_ref[...])
    q_t = jnp.concatenate([(q_nope * scale).T, (q_rot * scale).T], axis=0).astype(BF16)
    out_t = _flash_t(q_t, kn_ref, vt_ref, n_keys, tk)
    o_ref[...] = out_t.T.astype(o_ref.dtype)


def _mla_attention(q, kv, down, cos, sin, q_gain, kn_gain, kr_gain):
    b, t, _ = q.shape
    tq = min(1024, t)
    tk = min(512, t)
    kprep = min(512, t)
    kr_col = (MLA_Q_RANK + MLA_KV_RANK) // LANES
    tab_q = pl.BlockSpec((tq, LANES), lambda bi, h, qi: (qi, 0))
    tab_k = pl.BlockSpec((t, LANES), lambda bi, h, qi: (0, 0))
    return pl.pallas_call(
        functools.partial(_mla_attn_kernel, tk=tk, kprep=kprep),
        grid=(b, MLA_HEADS, t // tq),
        in_specs=[
            pl.BlockSpec((None, tq, MLA_QPAD), lambda bi, h, qi: (bi, qi, h)),
            pl.BlockSpec((None, t, MLA_QPAD), lambda bi, h, qi: (bi, 0, h)),
            pl.BlockSpec((None, t, LANES), lambda bi, h, qi: (bi, 0, kr_col)),
            tab_q, tab_q, tab_k, tab_k,
            pl.BlockSpec((1, MLA_QPAD), lambda bi, h, qi: (0, 0)),
            pl.BlockSpec((1, LANES), lambda bi, h, qi: (0, 0)),
            pl.BlockSpec((1, LANES), lambda bi, h, qi: (0, 0)),
        ],
        out_specs=pl.BlockSpec((None, tq, MLA_V_DIM), lambda bi, h, qi: (bi, qi, h)),
        out_shape=jax.ShapeDtypeStruct((b, t, MLA_HEADS * MLA_V_DIM), BF16),
        scratch_shapes=[pltpu.VMEM((t, MLA_QPAD), BF16), pltpu.VMEM((MLA_V_DIM, t), BF16)],
        compiler_params=_cparams("parallel", "parallel", "arbitrary"),
    )(q, kv, down, cos, sin, cos, sin, q_gain.reshape(1, MLA_QPAD), kn_gain.reshape(1, LANES),
      kr_gain.reshape(1, LANES))


def _split3(x):
    hi = x.astype(BF16)
    r1 = x - hi.astype(F32)
    mid = r1.astype(BF16)
    lo = (r1 - mid.astype(F32)).astype(BF16)
    return hi, mid, lo


def _gla_kernel(*refs, reverse, n_chunks):
    if reverse:
        (q_ref, k_ref, v_ref, low_ref, wg_ref, bg_ref, of_ref, r_ref, on_ref, o_ref, s_ref) = refs
    else:
        (q_ref, k_ref, v_ref, low_ref, wg_ref, bg_ref, o_ref, s_ref) = refs
    L = GLA_CHUNK

    @pl.when(pl.program_id(2) == 0)
    def _():
        s_ref[...] = jnp.zeros_like(s_ref)

    row = lax.broadcasted_iota(jnp.int32, (L, L), 0)
    col = lax.broadcasted_iota(jnp.int32, (L, L), 1)
    keep = (col >= row) if reverse else (col <= row)
    tri = jnp.where(keep, 1.0, 0.0).astype(BF16)
    ones = jnp.ones((L, LANES), BF16)
    tn_dims = (((0,), (0,)), ((), ()))
    nt_dims = (((1,), (1,)), ((), ()))

    order = range(n_chunks - 1, -1, -1) if reverse else range(n_chunks)
    for c in order:
        rows = slice(c * L, (c + 1) * L)
        z = jnp.dot(low_ref[rows, :].astype(BF16), wg_ref[...], preferred_element_type=F32) + bg_ref[...]
        la = (jnp.minimum(z, 0.0) - jnp.log1p(jnp.exp(-jnp.abs(z)))) * (1.0 / GLA_GATE_TAU)
        parts = _split3(la)
        cum = sum(jnp.dot(tri, p, preferred_element_type=F32) for p in parts)
        tot_col = sum(lax.dot_general(p, ones, tn_dims, preferred_element_type=F32) for p in parts)
        edge = (0 if reverse else L - 1)
        tot_row = cum[edge:edge + 1, :]

        q = q_ref[rows, :].astype(F32) * (GLA_DK_HEAD ** -0.5)
        k = k_ref[rows, :].astype(F32)
        v = v_ref[rows, :]
        qd = (q * jnp.exp(cum)).astype(BF16)
        kd = (k * jnp.exp(-cum)).astype(BF16)
        ke = (k * jnp.exp(tot_row - cum)).astype(BF16)
        a = lax.dot_general(qd, kd, nt_dims, preferred_element_type=F32)
        a = jnp.where(keep, a, 0.0).astype(BF16)
        s_old = s_ref[...]
        o = (jnp.dot(a, v, preferred_element_type=F32)
             + jnp.dot(qd, s_old.astype(BF16), preferred_element_type=F32))
        de = jnp.exp(tot_col)
        de = jnp.concatenate([de] * (GLA_DV_HEAD // LANES), axis=1)
        s_ref[...] = de * s_old + lax.dot_general(ke, v, tn_dims, preferred_element_type=F32)

        if reverse:
            o = o + of_ref[rows, :]
            o = _rms(o, on_ref[...], GLA_DV_HEAD)
            r = r_ref[rows, :].astype(F32)
            o = o * (r * (1.0 / (1.0 + jnp.exp(-r))))
        o_ref[rows, :] = o.astype(o_ref.dtype)


def _gla_direction(proj, low, wg, bg, o_fwd, o_norm, reverse):
    b, t, _ = proj.shape
    tb = min(512, t)
    nblk = t // tb
    blk = (lambda i: nblk - 1 - i) if reverse else (lambda i: i)
    k_col0 = GLA_DK // GLA_DK_HEAD
    v_col0 = 2 * GLA_DK // GLA_DV_HEAD
    r_col0 = (2 * GLA_DK + GLA_DV) // GLA_DV_HEAD
    in_specs = [
        pl.BlockSpec((None, tb, GLA_DK_HEAD), lambda bi, h, i: (bi, blk(i), h)),
        pl.BlockSpec((None, tb, GLA_DK_HEAD), lambda bi, h, i: (bi, blk(i), k_col0 + h)),
        pl.BlockSpec((None, tb, GLA_DV_HEAD), lambda bi, h, i: (bi, blk(i), v_col0 + h)),
        pl.BlockSpec((None, tb, LANES), lambda bi, h, i: (bi, blk(i), 0)),
        pl.BlockSpec((LANES, GLA_DK_HEAD), lambda bi, h, i: (0, h)),
        pl.BlockSpec((1, GLA_DK_HEAD), lambda bi, h, i: (0, h)),
    ]
    args = [proj, proj, proj, low, wg, bg]
    if reverse:
        in_specs += [
            pl.BlockSpec((None, tb, GLA_DV_HEAD), lambda bi, h, i: (bi, blk(i), h)),
            pl.BlockSpec((None, tb, GLA_DV_HEAD), lambda bi, h, i: (bi, blk(i), r_col0 + h)),
            pl.BlockSpec((1, GLA_DV_HEAD), lambda bi, h, i: (0, 0)),
        ]
        args += [o_fwd, proj, o_norm.reshape(1, GLA_DV_HEAD)]
    return pl.pallas_call(
        functools.partial(_gla_kernel, reverse=reverse, n_chunks=tb // GLA_CHUNK),
        grid=(b, GLA_HEADS, nblk),
        in_specs=in_specs,
        out_specs=pl.BlockSpec((None, tb, GLA_DV_HEAD), lambda bi, h, i: (bi, blk(i), h)),
        out_shape=jax.ShapeDtypeStruct((b, t, GLA_DV), BF16 if reverse else F32),
        scratch_shapes=[pltpu.VMEM((GLA_DK_HEAD, GLA_DV_HEAD), F32)],
        compiler_params=_cparams("parallel", "parallel", "arbitrary"),
    )(*args)


def _prep_gqa(w_qkv, q_norm, k_norm):
    n_rot = GQA_Q_HEADS + GQA_KV_HEADS
    cols = (np.arange(n_rot)[:, None] * GQA_HEAD_DIM + _HALF_SPLIT_128[None, :]).reshape(-1)
    cols = np.concatenate([cols, np.arange(n_rot * GQA_HEAD_DIM, w_qkv.shape[1])])
    return w_qkv[:, cols].astype(BF16), q_norm[_HALF_SPLIT_128], k_norm[_HALF_SPLIT_128]


def _prep_mla(w_down, w_uq, q_norm, k_rope_norm):
    qidx, qmsk = _mla_q_layout()
    cols = (np.arange(MLA_HEADS)[:, None] * MLA_QK_DIM + qidx[None, :]).reshape(-1)
    w_uq_p = (w_uq[:, cols] * jnp.asarray(np.tile(qmsk, MLA_HEADS))).astype(BF16)
    q_gain = q_norm[qidx] * jnp.asarray(qmsk)
    ridx, rmsk = _mla_rot_layout(0)
    kr0 = MLA_Q_RANK + MLA_KV_RANK
    w_kr = w_down[:, kr0 + ridx] * jnp.asarray(rmsk)
    w_down_p = jnp.concatenate([w_down[:, :kr0], w_kr], axis=1).astype(BF16)
    kr_gain = k_rope_norm[ridx] * jnp.asarray(rmsk)
    return w_down_p, w_uq_p, q_gain, kr_gain


def _prep_gla_gates(w_g1, w_g2):
    r = GLA_GATE_RANK
    d = w_g1.shape[1]
    w1 = jnp.zeros((d, LANES), F32).at[:, 0:r].set(w_g1[0]).at[:, r:2 * r].set(w_g1[1])
    w2 = jnp.zeros((2, LANES, GLA_DK), F32).at[0, 0:r].set(w_g2[0]).at[1, r:2 * r].set(w_g2[1])
    return w1.astype(BF16), w2.astype(BF16)


def _trunk(x3, mod, p):
    b, t, d = x3.shape
    n = b * t
    x = x3.reshape(n, d)
    for i in range(DEPTH):
        m = mod[i].reshape(b, 6, 1, d)
        sh1, sc1, g1, sh2, sc2, g2 = (m[:, s] for s in range(6))
        kind, j = i % N_MIXERS, i // N_MIXERS
        if kind == 0:
            qkv = _norm_mod_matmul(x, p['norm1_g'][i], sc1, sh1, p['gqa_w_qkv'][j], t, BF16)
            cos, sin = _gqa_rope_tables(t)
            a = _gqa_attention(qkv.reshape(b, t, -1), cos, sin, p['gqa_q_norm'][j], p['gqa_k_norm'][j])
            x = _oproj_residual(a.reshape(n, -1), p['gqa_w_o'][j], x, g1, t)
        elif kind == 1:
            proj = _norm_mod_matmul(x, p['norm1_g'][i], sc1, sh1, p['gla_w_in'][j], t, BF16, tn=2048)
            low = _norm_mod_matmul(x, p['norm1_g'][i], sc1, sh1, p['gla_w_g1'][j], t, F32)
            proj3, low3 = proj.reshape(b, t, -1), low.reshape(b, t, -1)
            bg = p['gla_b_g'][j]
            o_f = _gla_direction(proj3, low3, p['gla_w_g2'][j][0], bg[0:1], None, None, reverse=False)
            a = _gla_direction(proj3, low3, p['gla_w_g2'][j][1], bg[1:2], o_f, p['gla_o_norm'][j], reverse=True)
            x = _oproj_residual(a.reshape(n, -1), p['gla_w_o'][j], x, g1, t)
        else:
            down = _norm_mod_matmul(x, p['norm1_g'][i], sc1, sh1, p['mla_w_down'][j], t, F32)
            q = _norm_matmul(down, 0, MLA_Q_RANK, p['mla_q_a_norm'][j], p['mla_w_uq'][j], t, BF16)
            kv = _norm_matmul(down, MLA_Q_RANK // MLA_KV_RANK, MLA_KV_RANK, p['mla_kv_a_norm'][j],
                              p['mla_w_ukv'][j], t, BF16)
            cos, sin = _mla_rope_tables(t)
            a = _mla_attention(q.reshape(b, t, -1), kv.reshape(b, t, -1), down.reshape(b, t, -1), cos, sin,
                               p['mla_q_gain'][j], p['mla_k_nope_norm'][j], p['mla_kr_gain'][j])
            x = _oproj_residual(a.reshape(n, -1), p['mla_w_o'][j], x, g1, t)
        x = _mlp_residual(x, p['norm2_g'][i], sc2, sh2, g2, p['mlp_w1'][i], p['mlp_w2'][i], t)
    return x.reshape(b, t, d)


def kernel(x_prompt, x_sample, c_prompt, c_sample, norm1_g, norm2_g, ada_w, ada_b, gqa_w_qkv, gqa_q_norm, gqa_k_norm, gqa_w_o, gla_w_in, gla_w_g1, gla_w_g2, gla_b_g, gla_o_norm, gla_w_o, mla_w_down, mla_q_a_norm, mla_kv_a_norm, mla_w_uq, mla_w_ukv, mla_q_norm, mla_k_nope_norm, mla_k_rope_norm, mla_w_o, mlp_w1, mlp_w2):
    nA, nB, nC = gqa_w_qkv.shape[0], gla_w_in.shape[0], mla_w_down.shape[0]
    gqa = [_prep_gqa(gqa_w_qkv[j], gqa_q_norm[j], gqa_k_norm[j]) for j in range(nA)]
    mla = [_prep_mla(mla_w_down[j], mla_w_uq[j], mla_q_norm[j], mla_k_rope_norm[j]) for j in range(nC)]
    gates = [_prep_gla_gates(gla_w_g1[j], gla_w_g2[j]) for j in range(nB)]
    p = dict(
        norm1_g=norm1_g, norm2_g=norm2_g,
        gqa_w_qkv=[g[0] for g in gqa], gqa_q_norm=[g[1] for g in gqa], gqa_k_norm=[g[2] for g in gqa],
        gqa_w_o=gqa_w_o.astype(BF16),
        gla_w_in=gla_w_in.astype(BF16), gla_w_g1=[g[0] for g in gates], gla_w_g2=[g[1] for g in gates],
        gla_b_g=gla_b_g, gla_o_norm=gla_o_norm, gla_w_o=gla_w_o.astype(BF16),
        mla_w_down=[m[0] for m in mla], mla_w_uq=[m[1] for m in mla], mla_q_gain=[m[2] for m in mla],
        mla_kr_gain=[m[3] for m in mla], mla_q_a_norm=mla_q_a_norm, mla_kv_a_norm=mla_kv_a_norm,
        mla_w_ukv=mla_w_ukv.astype(BF16), mla_k_nope_norm=mla_k_nope_norm, mla_w_o=mla_w_o.astype(BF16),
        mlp_w1=mlp_w1.astype(BF16), mlp_w2=mlp_w2.astype(BF16),
    )
    bp, bs = c_prompt.shape[0], c_sample.shape[0]
    nb_pad = -(-(bp + bs) // 16) * 16
    c_all = jnp.concatenate([c_prompt, c_sample, jnp.zeros((nb_pad - bp - bs, c_prompt.shape[1]), F32)], axis=0)
    mod = _ada_modulation(c_all, ada_w, ada_b)
    y_prompt = _trunk(x_prompt, mod[:, :bp], p)
    y_sample = _trunk(x_sample, mod[:, bp:bp + bs], p)
    return (y_prompt, y_sample)
```

```python
import functools
import math

import numpy as np
import jax
import jax.numpy as jnp
from jax import lax
from jax.experimental import pallas as pl
from jax.experimental.pallas import tpu as pltpu

F32 = jnp.float32
BF16 = jnp.bfloat16

D_MODEL = 2048
DEPTH = 4
GRID_W = 64
ROPE_THETA = 10000.0
NORM_EPS = 1e-6
N_MIXERS = 3

GQA_HEAD_DIM = 128
GQA_Q_HEADS = 16
GQA_KV_HEADS = 4
GQA_GROUP = 4

GLA_HEADS = 4
GLA_DK = 1024
GLA_DV = 2048
GLA_DK_HEAD = 256
GLA_DV_HEAD = 512
GLA_GATE_RANK = 16
GLA_GATE_TAU = 16.0
GLA_CHUNK = 64

MLA_HEADS = 16
MLA_Q_RANK = 512
MLA_KV_RANK = 256
MLA_NOPE_DIM = 128
MLA_ROPE_DIM = 64
MLA_QK_DIM = 192
MLA_V_DIM = 128
MLA_QPAD = 256
MLA_DOWN_PAD = 896

D_FF = 4 * D_MODEL

LANES = 128
VMEM_LIMIT_BYTES = 56 * 1024 * 1024


def _tiles(n_rows, seq):
    tm = min(512, seq)
    assert seq % tm == 0 and n_rows % tm == 0
    return tm


def _cparams(*sem):
    return pltpu.CompilerParams(dimension_semantics=sem, vmem_limit_bytes=VMEM_LIMIT_BYTES)


def _ada_kernel(c_ref, w_ref, b_ref, o_ref):
    c = c_ref[...]
    cond = (c * (1.0 / (1.0 + jnp.exp(-c)))).astype(BF16)
    o_ref[...] = jnp.dot(cond, w_ref[...].astype(BF16), preferred_element_type=F32) + b_ref[...]


def _ada_modulation(c_all, ada_w, ada_b):
    nb, d = c_all.shape
    depth, _, n6 = ada_w.shape
    tn = 1024
    return pl.pallas_call(
        _ada_kernel,
        grid=(depth, n6 // tn),
        in_specs=[
            pl.BlockSpec((nb, d), lambda i, j: (0, 0)),
            pl.BlockSpec((None, d, tn), lambda i, j: (i, 0, j)),
            pl.BlockSpec((None, 1, tn), lambda i, j: (i, 0, j)),
        ],
        out_specs=pl.BlockSpec((None, nb, tn), lambda i, j: (i, 0, j)),
        out_shape=jax.ShapeDtypeStruct((depth, nb, n6), F32),
        compiler_params=_cparams("parallel", "parallel"),
    )(c_all, ada_w, ada_b.reshape(depth, 1, n6))


def _rms(x, g, n):
    ms = jnp.sum(x * x, axis=-1, keepdims=True) * (1.0 / n)
    return x * lax.rsqrt(ms + NORM_EPS) * g


def _norm_mod_matmul_kernel(x_ref, g_ref, sc_ref, sh_ref, w_ref, o_ref, h_ref):
    @pl.when(pl.program_id(1) == 0)
    def _():
        x = x_ref[...].astype(F32)
        y = _rms(x, g_ref[...], x.shape[-1])
        h_ref[...] = (y * (1.0 + sc_ref[...]) + sh_ref[...]).astype(BF16)

    o_ref[...] = jnp.dot(h_ref[...], w_ref[...], preferred_element_type=F32).astype(o_ref.dtype)


def _norm_matmul_kernel(x_ref, g_ref, w_ref, o_ref, h_ref):
    @pl.when(pl.program_id(1) == 0)
    def _():
        x = x_ref[...].astype(F32)
        h_ref[...] = _rms(x, g_ref[...], x.shape[-1]).astype(BF16)

    o_ref[...] = jnp.dot(h_ref[...], w_ref[...], preferred_element_type=F32).astype(o_ref.dtype)


def _norm_mod_matmul(x, g, sc, sh, w, seq, out_dtype, tn=None):
    n, d = x.shape
    nout = w.shape[1]
    tm = _tiles(n, seq)
    tn = nout if tn is None else tn
    tpb = seq // tm
    return pl.pallas_call(
        _norm_mod_matmul_kernel,
        grid=(n // tm, nout // tn),
        in_specs=[
            pl.BlockSpec((tm, d), lambda i, j: (i, 0)),
            pl.BlockSpec((1, d), lambda i, j: (0, 0)),
            pl.BlockSpec((None, 1, d), lambda i, j: (i // tpb, 0, 0)),
            pl.BlockSpec((None, 1, d), lambda i, j: (i // tpb, 0, 0)),
            pl.BlockSpec((d, tn), lambda i, j: (0, j)),
        ],
        out_specs=pl.BlockSpec((tm, tn), lambda i, j: (i, j)),
        out_shape=jax.ShapeDtypeStruct((n, nout), out_dtype),
        scratch_shapes=[pltpu.VMEM((tm, d), BF16)],
        compiler_params=_cparams("parallel", "arbitrary"),
    )(x, g.reshape(1, d), sc, sh, w)


def _norm_matmul(x, col_block, width, g, w, seq, out_dtype):
    n = x.shape[0]
    nout = w.shape[1]
    tm = _tiles(n, seq)
    return pl.pallas_call(
        _norm_matmul_kernel,
        grid=(n // tm, 1),
        in_specs=[
            pl.BlockSpec((tm, width), lambda i, j: (i, col_block)),
            pl.BlockSpec((1, width), lambda i, j: (0, 0)),
            pl.BlockSpec((width, nout), lambda i, j: (0, 0)),
        ],
        out_specs=pl.BlockSpec((tm, nout), lambda i, j: (i, 0)),
        out_shape=jax.ShapeDtypeStruct((n, nout), out_dtype),
        scratch_shapes=[pltpu.VMEM((tm, width), BF16)],
        compiler_params=_cparams("parallel", "arbitrary"),
    )(x, g.reshape(1, width), w)


def _oproj_kernel(a_ref, w_ref, x_ref, gate_ref, o_ref):
    y = jnp.dot(a_ref[...], w_ref[...], preferred_element_type=F32)
    o_ref[...] = x_ref[...] + gate_ref[...] * y


def _oproj_residual(a, w, x, gate, seq):
    n, k = a.shape
    d = w.shape[1]
    tm = _tiles(n, seq)
    tpb = seq // tm
    return pl.pallas_call(
        _oproj_kernel,
        grid=(n // tm,),
        in_specs=[
            pl.BlockSpec((tm, k), lambda i: (i, 0)),
            pl.BlockSpec((k, d), lambda i: (0, 0)),
            pl.BlockSpec((tm, d), lambda i: (i, 0)),
            pl.BlockSpec((None, 1, d), lambda i: (i // tpb, 0, 0)),
        ],
        out_specs=pl.BlockSpec((tm, d), lambda i: (i, 0)),
        out_shape=jax.ShapeDtypeStruct((n, d), F32),
        compiler_params=_cparams("parallel"),
    )(a, w, x, gate)


def _mlp_kernel(x_ref, g_ref, sc_ref, sh_ref, gate_ref, w1_ref, w2_ref, o_ref, h_ref):
    j = pl.program_id(1)

    @pl.when(j == 0)
    def _():
        x = x_ref[...]
        y = _rms(x, g_ref[...], x.shape[-1])
        h_ref[...] = (y * (1.0 + sc_ref[...]) + sh_ref[...]).astype(BF16)
        o_ref[...] = jnp.zeros_like(o_ref)

    a = jnp.maximum(jnp.dot(h_ref[...], w1_ref[...], preferred_element_type=F32), 0.0)
    o_ref[...] += jnp.dot((a * a).astype(BF16), w2_ref[...], preferred_element_type=F32)

    @pl.when(j == pl.num_programs(1) - 1)
    def _():
        o_ref[...] = x_ref[...] + gate_ref[...] * o_ref[...]


def _mlp_residual(x, g, sc, sh, gate, w1, w2, seq):
    n, d = x.shape
    dff = w1.shape[1]
    tm = min(1024, seq)
    tf = min(512, dff)
    assert seq % tm == 0 and dff % tf == 0
    tpb = seq // tm
    mod_spec = pl.BlockSpec((None, 1, d), lambda i, j: (i // tpb, 0, 0))
    return pl.pallas_call(
        _mlp_kernel,
        grid=(n // tm, dff // tf),
        in_specs=[
            pl.BlockSpec((tm, d), lambda i, j: (i, 0)),
            pl.BlockSpec((1, d), lambda i, j: (0, 0)),
            mod_spec, mod_spec, mod_spec,
            pl.BlockSpec((d, tf), lambda i, j: (0, j)),
            pl.BlockSpec((tf, d), lambda i, j: (j, 0)),
        ],
        out_specs=pl.BlockSpec((tm, d), lambda i, j: (i, 0)),
        out_shape=jax.ShapeDtypeStruct((n, d), F32),
        scratch_shapes=[pltpu.VMEM((tm, d), BF16)],
        compiler_params=_cparams("parallel", "arbitrary"),
    )(x, g.reshape(1, d), sc, sh, gate, w1, w2)


def _axial_angles(n_tok, rot_dim):
    rows = n_tok // GRID_W
    row = jnp.repeat(jnp.arange(rows, dtype=F32), GRID_W)
    col = jnp.tile(jnp.arange(GRID_W, dtype=F32), rows)
    n_freq = rot_dim // 4
    inv = ROPE_THETA ** (-jnp.arange(n_freq, dtype=F32) / n_freq)
    ang = jnp.concatenate([row[:, None] * inv, col[:, None] * inv], axis=-1)
    return jnp.cos(ang), jnp.sin(ang)


def _gqa_rope_tables(n_tok):
    cos, sin = _axial_angles(n_tok, GQA_HEAD_DIM)
    return jnp.concatenate([cos, cos], -1), jnp.concatenate([-sin, sin], -1)


def _mla_rope_tables(n_tok):
    cos, sin = _axial_angles(n_tok, MLA_ROPE_DIM)
    z = jnp.zeros_like(cos)
    return jnp.concatenate([cos, z, cos, z], -1), jnp.concatenate([-sin, z, sin, z], -1)


_HALF_SPLIT_128 = np.concatenate([np.arange(0, 128, 2), np.arange(1, 128, 2)])


def _mla_rot_layout(base):
    idx = np.zeros(128, np.int32)
    msk = np.zeros(128, np.float32)
    idx[0:32] = base + np.arange(0, 64, 2)
    idx[64:96] = base + np.arange(1, 64, 2)
    msk[0:32] = 1.0
    msk[64:96] = 1.0
    return idx, msk


def _mla_q_layout():
    ridx, rmsk = _mla_rot_layout(MLA_NOPE_DIM)
    idx = np.concatenate([np.arange(MLA_NOPE_DIM), ridx])
    msk = np.concatenate([np.ones(MLA_NOPE_DIM, np.float32), rmsk])
    return idx, msk


def _rope(x, cos, sin):
    return x * cos + pltpu.roll(x, 64, 1) * sin


LOG2E = math.log2(math.e)


MAX_UNSHIFTED_SCORE = 40.0


def _flash_t(q_t, kn_ref, vt_ref, n_keys, tk, bounded):
    m_cols = q_t.shape[1]
    m = jnp.full((1, m_cols), -jnp.inf, F32)
    l8 = jnp.zeros((8, m_cols), F32)
    acc = jnp.zeros((vt_ref.shape[0], m_cols), F32)
    for c in range(n_keys // tk):
        s = jnp.dot(kn_ref[c * tk:(c + 1) * tk, :], q_t, preferred_element_type=F32)
        if bounded:
            p = jnp.exp2(s)
            l8 = l8 + jnp.sum(p.reshape(tk // 8, 8, m_cols), axis=0)
            acc = acc + jnp.dot(vt_ref[:, c * tk:(c + 1) * tk], p.astype(BF16), preferred_element_type=F32)
        else:
            m_new = jnp.maximum(m, jnp.max(s, axis=0, keepdims=True))
            alpha = jnp.exp2(m - m_new)
            p = jnp.exp2(s - m_new)
            l8 = alpha * l8 + jnp.sum(p.reshape(tk // 8, 8, m_cols), axis=0)
            acc = alpha * acc + jnp.dot(vt_ref[:, c * tk:(c + 1) * tk], p.astype(BF16),
                                        preferred_element_type=F32)
            m = m_new
    return acc * (1.0 / jnp.sum(l8, axis=0, keepdims=True))


def _store_vt(vt_ref, v_load, n_keys, blk):
    for c in range(n_keys // blk):
        vt_ref[:, c * blk:(c + 1) * blk] = v_load(c * blk, blk).astype(F32).T.astype(BF16)


def _gqa_attn_kernel(q_ref, k_ref, v_ref, cq_ref, sq_ref, ck_ref, sk_ref, qg_ref, kg_ref, o_ref, kn_ref, vt_ref,
                     *, tk, kprep, bounded):
    n_keys = k_ref.shape[0]
    tq = q_ref.shape[0]

    @pl.when(pl.program_id(2) == 0)
    def _():
        def prep(c, _):
            rows = pl.ds(pl.multiple_of(c * kprep, kprep), kprep)
            k = _rms(k_ref[rows, :].astype(F32), kg_ref[...], GQA_HEAD_DIM)
            kn_ref[rows, :] = _rope(k, ck_ref[rows, :], sk_ref[rows, :]).astype(BF16)
            return 0
        lax.fori_loop(0, n_keys // kprep, prep, 0)
        _store_vt(vt_ref, lambda r0, nr: v_ref[r0:r0 + nr, :], n_keys, kprep)

    scale = GQA_HEAD_DIM ** -0.5 * LOG2E
    qs = []
    for h in range(GQA_GROUP):
        q = _rms(q_ref[:, h * GQA_HEAD_DIM:(h + 1) * GQA_HEAD_DIM].astype(F32), qg_ref[...], GQA_HEAD_DIM)
        qs.append((_rope(q, cq_ref[...], sq_ref[...]) * scale).T.astype(BF16))
    q_t = jnp.concatenate(qs, axis=1)
    out_t = _flash_t(q_t, kn_ref, vt_ref, n_keys, tk, bounded)
    for h in range(GQA_GROUP):
        o_ref[:, h * GQA_HEAD_DIM:(h + 1) * GQA_HEAD_DIM] = out_t[:, h * tq:(h + 1) * tq].T.astype(o_ref.dtype)


def _dispatch_on_score_bound(bound, attention, *args):
    return lax.cond(bound <= MAX_UNSHIFTED_SCORE,
                    functools.partial(attention, bounded=True),
                    functools.partial(attention, bounded=False), *args)


def _gqa_score_bound(q_gain, k_gain):
    return (GQA_HEAD_DIM * jnp.max(jnp.abs(q_gain)) * jnp.max(jnp.abs(k_gain))) * (GQA_HEAD_DIM ** -0.5 * LOG2E)


def _mla_score_bound(q_gain, kn_gain, kr_gain):
    q_norm = math.sqrt(MLA_QK_DIM) * jnp.max(jnp.abs(q_gain))
    k_norm = jnp.sqrt(MLA_NOPE_DIM * jnp.max(kn_gain * kn_gain) + MLA_ROPE_DIM * jnp.max(kr_gain * kr_gain))
    return q_norm * k_norm * (MLA_QK_DIM ** -0.5 * LOG2E)


def _gqa_attention(qkv, cos, sin, q_gain, k_gain, *, bounded):
    b, t, _ = qkv.shape
    tq = min(256, t)
    tk = min(512, t)
    kprep = min(512, t)
    gw = GQA_GROUP * GQA_HEAD_DIM
    k_col0 = GQA_Q_HEADS
    v_col0 = GQA_Q_HEADS + GQA_KV_HEADS
    tab_q = pl.BlockSpec((tq, LANES), lambda bi, g, qi: (qi, 0))
    tab_k = pl.BlockSpec((t, LANES), lambda bi, g, qi: (0, 0))
    gain = pl.BlockSpec((1, LANES), lambda bi, g, qi: (0, 0))
    return pl.pallas_call(
        functools.partial(_gqa_attn_kernel, tk=tk, kprep=kprep, bounded=bounded),
        grid=(b, GQA_KV_HEADS, t // tq),
        in_specs=[
            pl.BlockSpec((None, tq, gw), lambda bi, g, qi: (bi, qi, g)),
            pl.BlockSpec((None, t, GQA_HEAD_DIM), lambda bi, g, qi: (bi, 0, k_col0 + g)),
            pl.BlockSpec((None, t, GQA_HEAD_DIM), lambda bi, g, qi: (bi, 0, v_col0 + g)),
            tab_q, tab_q, tab_k, tab_k, gain, gain,
        ],
        out_specs=pl.BlockSpec((None, tq, gw), lambda bi, g, qi: (bi, qi, g)),
        out_shape=jax.ShapeDtypeStruct((b, t, GQA_Q_HEADS * GQA_HEAD_DIM), BF16),
        scratch_shapes=[pltpu.VMEM((t, GQA_HEAD_DIM), BF16), pltpu.VMEM((GQA_HEAD_DIM, t), BF16)],
        compiler_params=_cparams("parallel", "parallel", "arbitrary"),
    )(qkv, qkv, qkv, cos, sin, cos, sin, q_gain.reshape(1, LANES), k_gain.reshape(1, LANES))


def _mla_attn_kernel(q_ref, kv_ref, kr_ref, cq_ref, sq_ref, ck_ref, sk_ref, qg_ref, kng_ref, krg_ref,
                     o_ref, kn_ref, vt_ref, *, tk, kprep, bounded):
    n_keys = kv_ref.shape[0]

    @pl.when(pl.program_id(2) == 0)
    def _():
        def prep(c, _):
            rows = pl.ds(pl.multiple_of(c * kprep, kprep), kprep)
            kn = _rms(kv_ref[rows, 0:MLA_NOPE_DIM].astype(F32), kng_ref[...], MLA_NOPE_DIM)
            kn_ref[rows, 0:MLA_NOPE_DIM] = kn.astype(BF16)
            kr = _rms(kr_ref[rows, :].astype(F32), krg_ref[...], MLA_ROPE_DIM)
            kn_ref[rows, MLA_NOPE_DIM:MLA_QPAD] = _rope(kr, ck_ref[rows, :], sk_ref[rows, :]).astype(BF16)
            return 0
        lax.fori_loop(0, n_keys // kprep, prep, 0)
        _store_vt(vt_ref, lambda r0, nr: kv_ref[r0:r0 + nr, MLA_NOPE_DIM:MLA_QPAD], n_keys, kprep)

    scale = MLA_QK_DIM ** -0.5 * LOG2E
    q = q_ref[...].astype(F32)
    g = qg_ref[...]
    inv = lax.rsqrt(jnp.sum(q * q, axis=-1, keepdims=True) * (1.0 / MLA_QK_DIM) + NORM_EPS)
    q_nope = q[:, 0:MLA_NOPE_DIM] * inv * g[:, 0:MLA_NOPE_DIM]
    q_rot = _rope(q[:, MLA_NOPE_DIM:MLA_QPAD] * inv * g[:, MLA_NOPE_DIM:MLA_QPAD], cq_ref[...], sq_ref[...])
    q_t = jnp.concatenate([(q_nope * scale).T, (q_rot * scale).T], axis=0).astype(BF16)
    out_t = _flash_t(q_t, kn_ref, vt_ref, n_keys, tk, bounded)
    o_ref[...] = out_t.T.astype(o_ref.dtype)


def _mla_attention(q, kv, down, cos, sin, q_gain, kn_gain, kr_gain, *, bounded):
    b, t, _ = q.shape
    tq = min(1024, t)
    tk = min(512, t)
    kprep = min(512, t)
    kr_col = (MLA_Q_RANK + MLA_KV_RANK) // LANES
    tab_q = pl.BlockSpec((tq, LANES), lambda bi, h, qi: (qi, 0))
    tab_k = pl.BlockSpec((t, LANES), lambda bi, h, qi: (0, 0))
    return pl.pallas_call(
        functools.partial(_mla_attn_kernel, tk=tk, kprep=kprep, bounded=bounded),
        grid=(b, MLA_HEADS, t // tq),
        in_specs=[
            pl.BlockSpec((None, tq, MLA_QPAD), lambda bi, h, qi: (bi, qi, h)),
            pl.BlockSpec((None, t, MLA_QPAD), lambda bi, h, qi: (bi, 0, h)),
            pl.BlockSpec((None, t, LANES), lambda bi, h, qi: (bi, 0, kr_col)),
            tab_q, tab_q, tab_k, tab_k,
            pl.BlockSpec((1, MLA_QPAD), lambda bi, h, qi: (0, 0)),
            pl.BlockSpec((1, LANES), lambda bi, h, qi: (0, 0)),
            pl.BlockSpec((1, LANES), lambda bi, h, qi: (0, 0)),
        ],
        out_specs=pl.BlockSpec((None, tq, MLA_V_DIM), lambda bi, h, qi: (bi, qi, h)),
        out_shape=jax.ShapeDtypeStruct((b, t, MLA_HEADS * MLA_V_DIM), BF16),
        scratch_shapes=[pltpu.VMEM((t, MLA_QPAD), BF16), pltpu.VMEM((MLA_V_DIM, t), BF16)],
        compiler_params=_cparams("parallel", "parallel", "arbitrary"),
    )(q, kv, down, cos, sin, cos, sin, q_gain.reshape(1, MLA_QPAD), kn_gain.reshape(1, LANES),
      kr_gain.reshape(1, LANES))


def _split3(x):
    hi = x.astype(BF16)
    r1 = x - hi.astype(F32)
    mid = r1.astype(BF16)
    lo = (r1 - mid.astype(F32)).astype(BF16)
    return hi, mid, lo


def _gla_kernel(*refs, reverse, n_chunks):
    if reverse:
        (q_ref, k_ref, v_ref, low_ref, wg_ref, bg_ref, of_ref, r_ref, on_ref, o_ref, s_ref) = refs
    else:
        (q_ref, k_ref, v_ref, low_ref, wg_ref, bg_ref, o_ref, s_ref) = refs
    L = GLA_CHUNK

    @pl.when(pl.program_id(2) == 0)
    def _():
        s_ref[...] = jnp.zeros_like(s_ref)

    row = lax.broadcasted_iota(jnp.int32, (L, L), 0)
    col = lax.broadcasted_iota(jnp.int32, (L, L), 1)
    keep = (col >= row) if reverse else (col <= row)
    tri = jnp.where(keep, 1.0, 0.0).astype(BF16)
    ones = jnp.ones((L, LANES), BF16)
    tn_dims = (((0,), (0,)), ((), ()))
    nt_dims = (((1,), (1,)), ((), ()))

    order = range(n_chunks - 1, -1, -1) if reverse else range(n_chunks)
    for c in order:
        rows = slice(c * L, (c + 1) * L)
        z = jnp.dot(low_ref[rows, :].astype(BF16), wg_ref[...], preferred_element_type=F32) + bg_ref[...]
        la = (jnp.minimum(z, 0.0) - jnp.log1p(jnp.exp(-jnp.abs(z)))) * (1.0 / GLA_GATE_TAU)
        parts = _split3(la)
        cum = sum(jnp.dot(tri, p, preferred_element_type=F32) for p in parts)
        tot_col = sum(lax.dot_general(p, ones, tn_dims, preferred_element_type=F32) for p in parts)
        edge = (0 if reverse else L - 1)
        tot_row = cum[edge:edge + 1, :]

        q = q_ref[rows, :].astype(F32) * (GLA_DK_HEAD ** -0.5)
        k = k_ref[rows, :].astype(F32)
        v = v_ref[rows, :]
        qd = (q * jnp.exp(cum)).astype(BF16)
        kd = (k * jnp.exp(-cum)).astype(BF16)
        ke = (k * jnp.exp(tot_row - cum)).astype(BF16)
        a = lax.dot_general(qd, kd, nt_dims, preferred_element_type=F32)
        a = jnp.where(keep, a, 0.0).astype(BF16)
        s_old = s_ref[...]
        o = (jnp.dot(a, v, preferred_element_type=F32)
             + jnp.dot(qd, s_old.astype(BF16), preferred_element_type=F32))
        de = jnp.exp(tot_col)
        de = jnp.concatenate([de] * (GLA_DV_HEAD // LANES), axis=1)
        s_ref[...] = de * s_old + lax.dot_general(ke, v, tn_dims, preferred_element_type=F32)

        if reverse:
            o = o + of_ref[rows, :]
            o = _rms(o, on_ref[...], GLA_DV_HEAD)
            r = r_ref[rows, :].astype(F32)
            o = o * (r * (1.0 / (1.0 + jnp.exp(-r))))
        o_ref[rows, :] = o.astype(o_ref.dtype)


def _gla_direction(proj, low, wg, bg, o_fwd, o_norm, reverse):
    b, t, _ = proj.shape
    tb = min(512, t)
    nblk = t // tb
    blk = (lambda i: nblk - 1 - i) if reverse else (lambda i: i)
    k_col0 = GLA_DK // GLA_DK_HEAD
    v_col0 = 2 * GLA_DK // GLA_DV_HEAD
    r_col0 = (2 * GLA_DK + GLA_DV) // GLA_DV_HEAD
    in_specs = [
        pl.BlockSpec((None, tb, GLA_DK_HEAD), lambda bi, h, i: (bi, blk(i), h)),
        pl.BlockSpec((None, tb, GLA_DK_HEAD), lambda bi, h, i: (bi, blk(i), k_col0 + h)),
        pl.BlockSpec((None, tb, GLA_DV_HEAD), lambda bi, h, i: (bi, blk(i), v_col0 + h)),
        pl.BlockSpec((None, tb, LANES), lambda bi, h, i: (bi, blk(i), 0)),
        pl.BlockSpec((LANES, GLA_DK_HEAD), lambda bi, h, i: (0, h)),
        pl.BlockSpec((1, GLA_DK_HEAD), lambda bi, h, i: (0, h)),
    ]
    args = [proj, proj, proj, low, wg, bg]
    if reverse:
        in_specs += [
            pl.BlockSpec((None, tb, GLA_DV_HEAD), lambda bi, h, i: (bi, blk(i), h)),
            pl.BlockSpec((None, tb, GLA_DV_HEAD), lambda bi, h, i: (bi, blk(i), r_col0 + h)),
            pl.BlockSpec((1, GLA_DV_HEAD), lambda bi, h, i: (0, 0)),
        ]
        args += [o_fwd, proj, o_norm.reshape(1, GLA_DV_HEAD)]
    return pl.pallas_call(
        functools.partial(_gla_kernel, reverse=reverse, n_chunks=tb // GLA_CHUNK),
        grid=(b, GLA_HEADS, nblk),
        in_specs=in_specs,
        out_specs=pl.BlockSpec((None, tb, GLA_DV_HEAD), lambda bi, h, i: (bi, blk(i), h)),
        out_shape=jax.ShapeDtypeStruct((b, t, GLA_DV), BF16 if reverse else F32),
        scratch_shapes=[pltpu.VMEM((GLA_DK_HEAD, GLA_DV_HEAD), F32)],
        compiler_params=_cparams("parallel", "parallel", "arbitrary"),
    )(*args)


def _prep_gqa(w_qkv, q_norm, k_norm):
    n_rot = GQA_Q_HEADS + GQA_KV_HEADS
    cols = (np.arange(n_rot)[:, None] * GQA_HEAD_DIM + _HALF_SPLIT_128[None, :]).reshape(-1)
    cols = np.concatenate([cols, np.arange(n_rot * GQA_HEAD_DIM, w_qkv.shape[1])])
    return w_qkv[:, cols].astype(BF16), q_norm[_HALF_SPLIT_128], k_norm[_HALF_SPLIT_128]


def _prep_mla(w_down, w_uq, q_norm, k_rope_norm):
    qidx, qmsk = _mla_q_layout()
    cols = (np.arange(MLA_HEADS)[:, None] * MLA_QK_DIM + qidx[None, :]).reshape(-1)
    w_uq_p = (w_uq[:, cols] * jnp.asarray(np.tile(qmsk, MLA_HEADS))).astype(BF16)
    q_gain = q_norm[qidx] * jnp.asarray(qmsk)
    ridx, rmsk = _mla_rot_layout(0)
    kr0 = MLA_Q_RANK + MLA_KV_RANK
    w_kr = w_down[:, kr0 + ridx] * jnp.asarray(rmsk)
    w_down_p = jnp.concatenate([w_down[:, :kr0], w_kr], axis=1).astype(BF16)
    kr_gain = k_rope_norm[ridx] * jnp.asarray(rmsk)
    return w_down_p, w_uq_p, q_gain, kr_gain


def _prep_gla_gates(w_g1, w_g2):
    r = GLA_GATE_RANK
    d = w_g1.shape[1]
    w1 = jnp.zeros((d, LANES), F32).at[:, 0:r].set(w_g1[0]).at[:, r:2 * r].set(w_g1[1])
    w2 = jnp.zeros((2, LANES, GLA_DK), F32).at[0, 0:r].set(w_g2[0]).at[1, r:2 * r].set(w_g2[1])
    return w1.astype(BF16), w2.astype(BF16)


def _trunk(x3, mod, p):
    b, t, d = x3.shape
    n = b * t
    x = x3.reshape(n, d)
    for i in range(DEPTH):
        m = mod[i].reshape(b, 6, 1, d)
        sh1, sc1, g1, sh2, sc2, g2 = (m[:, s] for s in range(6))
        kind, j = i % N_MIXERS, i // N_MIXERS
        if kind == 0:
            qkv = _norm_mod_matmul(x, p['norm1_g'][i], sc1, sh1, p['gqa_w_qkv'][j], t, BF16)
            cos, sin = _gqa_rope_tables(t)
            a = _dispatch_on_score_bound(_gqa_score_bound(p['gqa_q_norm'][j], p['gqa_k_norm'][j]), _gqa_attention,
                                         qkv.reshape(b, t, -1), cos, sin, p['gqa_q_norm'][j], p['gqa_k_norm'][j])
            x = _oproj_residual(a.reshape(n, -1), p['gqa_w_o'][j], x, g1, t)
        elif kind == 1:
            proj = _norm_mod_matmul(x, p['norm1_g'][i], sc1, sh1, p['gla_w_in'][j], t, BF16, tn=2048)
            low = _norm_mod_matmul(x, p['norm1_g'][i], sc1, sh1, p['gla_w_g1'][j], t, F32)
            proj3, low3 = proj.reshape(b, t, -1), low.reshape(b, t, -1)
            bg = p['gla_b_g'][j]
            o_f = _gla_direction(proj3, low3, p['gla_w_g2'][j][0], bg[0:1], None, None, reverse=False)
            a = _gla_direction(proj3, low3, p['gla_w_g2'][j][1], bg[1:2], o_f, p['gla_o_norm'][j], reverse=True)
            x = _oproj_residual(a.reshape(n, -1), p['gla_w_o'][j], x, g1, t)
        else:
            down = _norm_mod_matmul(x, p['norm1_g'][i], sc1, sh1, p['mla_w_down'][j], t, F32)
            q = _norm_matmul(down, 0, MLA_Q_RANK, p['mla_q_a_norm'][j], p['mla_w_uq'][j], t, BF16)
            kv = _norm_matmul(down, MLA_Q_RANK // MLA_KV_RANK, MLA_KV_RANK, p['mla_kv_a_norm'][j],
                              p['mla_w_ukv'][j], t, BF16)
            cos, sin = _mla_rope_tables(t)
            gains = (p['mla_q_gain'][j], p['mla_k_nope_norm'][j], p['mla_kr_gain'][j])
            a = _dispatch_on_score_bound(_mla_score_bound(*gains), _mla_attention, q.reshape(b, t, -1),
                                         kv.reshape(b, t, -1), down.reshape(b, t, -1), cos, sin, *gains)
            x = _oproj_residual(a.reshape(n, -1), p['mla_w_o'][j], x, g1, t)
        x = _mlp_residual(x, p['norm2_g'][i], sc2, sh2, g2, p['mlp_w1'][i], p['mlp_w2'][i], t)
    return x.reshape(b, t, d)


def kernel(x_prompt, x_sample, c_prompt, c_sample, norm1_g, norm2_g, ada_w, ada_b, gqa_w_qkv, gqa_q_norm, gqa_k_norm, gqa_w_o, gla_w_in, gla_w_g1, gla_w_g2, gla_b_g, gla_o_norm, gla_w_o, mla_w_down, mla_q_a_norm, mla_kv_a_norm, mla_w_uq, mla_w_ukv, mla_q_norm, mla_k_nope_norm, mla_k_rope_norm, mla_w_o, mlp_w1, mlp_w2):
    nA, nB, nC = gqa_w_qkv.shape[0], gla_w_in.shape[0], mla_w_down.shape[0]
    gqa = [_prep_gqa(gqa_w_qkv[j], gqa_q_norm[j], gqa_k_norm[j]) for j in range(nA)]
    mla = [_prep_mla(mla_w_down[j], mla_w_uq[j], mla_q_norm[j], mla_k_rope_norm[j]) for j in range(nC)]
    gates = [_prep_gla_gates(gla_w_g1[j], gla_w_g2[j]) for j in range(nB)]
    p = dict(
        norm1_g=norm1_g, norm2_g=norm2_g,
        gqa_w_qkv=[g[0] for g in gqa], gqa_q_norm=[g[1] for g in gqa], gqa_k_norm=[g[2] for g in gqa],
        gqa_w_o=gqa_w_o.astype(BF16),
        gla_w_in=gla_w_in.astype(BF16), gla_w_g1=[g[0] for g in gates], gla_w_g2=[g[1] for g in gates],
        gla_b_g=gla_b_g, gla_o_norm=gla_o_norm, gla_w_o=gla_w_o.astype(BF16),
        mla_w_down=[m[0] for m in mla], mla_w_uq=[m[1] for m in mla], mla_q_gain=[m[2] for m in mla],
        mla_kr_gain=[m[3] for m in mla], mla_q_a_norm=mla_q_a_norm, mla_kv_a_norm=mla_kv_a_norm,
        mla_w_ukv=mla_w_ukv.astype(BF16), mla_k_nope_norm=mla_k_nope_norm, mla_w_o=mla_w_o.astype(BF16),
        mlp_w1=mlp_w1.astype(BF16), mlp_w2=mlp_w2.astype(BF16),
    )
    bp, bs = c_prompt.shape[0], c_sample.shape[0]
    nb_pad = -(-(bp + bs) // 16) * 16
    c_all = jnp.concatenate([c_prompt, c_sample, jnp.zeros((nb_pad - bp - bs, c_prompt.shape[1]), F32)], axis=0)
    mod = _ada_modulation(c_all, ada_w, ada_b)
    y_prompt = _trunk(x_prompt, mod[:, :bp], p)
    y_sample = _trunk(x_sample, mod[:, bp:bp + bs], p)
    return (y_prompt, y_sample)
```

```python
import functools
import math

import numpy as np
import jax
import jax.numpy as jnp
from jax import lax
from jax.experimental import pallas as pl
from jax.experimental.pallas import tpu as pltpu

F32 = jnp.float32
BF16 = jnp.bfloat16

D_MODEL = 2048
DEPTH = 4
GRID_W = 64
ROPE_THETA = 10000.0
NORM_EPS = 1e-6
N_MIXERS = 3

GQA_HEAD_DIM = 128
GQA_Q_HEADS = 16
GQA_KV_HEADS = 4
GQA_GROUP = 4

GLA_HEADS = 4
GLA_DK = 1024
GLA_DV = 2048
GLA_DK_HEAD = 256
GLA_DV_HEAD = 512
GLA_GATE_RANK = 16
GLA_GATE_TAU = 16.0
GLA_BLOCK = 128

MLA_HEADS = 16
MLA_Q_RANK = 512
MLA_KV_RANK = 256
MLA_NOPE_DIM = 128
MLA_ROPE_DIM = 64
MLA_QK_DIM = 192
MLA_V_DIM = 128
MLA_QPAD = 256
MLA_DOWN_PAD = 896

D_FF = 4 * D_MODEL

LANES = 128
VMEM_LIMIT_BYTES = 56 * 1024 * 1024


def _tiles(n_rows, seq):
    tm = min(512, seq)
    assert seq % tm == 0 and n_rows % tm == 0
    return tm


def _cparams(*sem):
    return pltpu.CompilerParams(dimension_semantics=sem, vmem_limit_bytes=VMEM_LIMIT_BYTES)


def _ada_kernel(c_ref, w_ref, b_ref, o_ref):
    c = c_ref[...]
    cond = (c * (1.0 / (1.0 + jnp.exp(-c)))).astype(BF16)
    o_ref[...] = jnp.dot(cond, w_ref[...].astype(BF16), preferred_element_type=F32) + b_ref[...]


def _ada_modulation(c_all, ada_w, ada_b):
    nb, d = c_all.shape
    depth, _, n6 = ada_w.shape
    tn = 1024
    return pl.pallas_call(
        _ada_kernel,
        grid=(depth, n6 // tn),
        in_specs=[
            pl.BlockSpec((nb, d), lambda i, j: (0, 0)),
            pl.BlockSpec((None, d, tn), lambda i, j: (i, 0, j)),
            pl.BlockSpec((None, 1, tn), lambda i, j: (i, 0, j)),
        ],
        out_specs=pl.BlockSpec((None, nb, tn), lambda i, j: (i, 0, j)),
        out_shape=jax.ShapeDtypeStruct((depth, nb, n6), F32),
        compiler_params=_cparams("parallel", "parallel"),
    )(c_all, ada_w, ada_b.reshape(depth, 1, n6))


def _rms(x, g, n):
    ms = jnp.sum(x * x, axis=-1, keepdims=True) * (1.0 / n)
    return x * lax.rsqrt(ms + NORM_EPS) * g


NORM_ROWS = 16


def _store_norm(h_ref, x_ref, gain, shift):
    d = x_ref.shape[-1]
    blocks = [slice(r, r + NORM_ROWS) for r in range(0, x_ref.shape[0], NORM_ROWS)]
    inv = []
    for rows in blocks:
        x = x_ref[rows, :].astype(F32)
        inv.append(lax.rsqrt(jnp.sum(x * x, axis=-1, keepdims=True) * (1.0 / d) + NORM_EPS))
    for rows, inv_r in zip(blocks, inv):
        y = x_ref[rows, :].astype(F32) * inv_r * gain
        h_ref[rows, :] = (y if shift is None else y + shift).astype(BF16)


def _norm_mod_matmul_kernel(x_ref, g_ref, sc_ref, sh_ref, w_ref, o_ref, h_ref):
    @pl.when(pl.program_id(1) == 0)
    def _():
        _store_norm(h_ref, x_ref, g_ref[...] * (1.0 + sc_ref[...]), sh_ref[...])

    o_ref[...] = jnp.dot(h_ref[...], w_ref[...], preferred_element_type=F32).astype(o_ref.dtype)


def _norm_matmul_kernel(x_ref, g_ref, w_ref, o_ref, h_ref):
    @pl.when(pl.program_id(1) == 0)
    def _():
        _store_norm(h_ref, x_ref, g_ref[...], None)

    o_ref[...] = jnp.dot(h_ref[...], w_ref[...], preferred_element_type=F32).astype(o_ref.dtype)


def _norm_mod_matmul(x, g, sc, sh, w, seq, out_dtype, tn=None):
    n, d = x.shape
    nout = w.shape[1]
    tm = _tiles(n, seq)
    tn = nout if tn is None else tn
    tpb = seq // tm
    return pl.pallas_call(
        _norm_mod_matmul_kernel,
        grid=(n // tm, nout // tn),
        in_specs=[
            pl.BlockSpec((tm, d), lambda i, j: (i, 0)),
            pl.BlockSpec((1, d), lambda i, j: (0, 0)),
            pl.BlockSpec((None, 1, d), lambda i, j: (i // tpb, 0, 0)),
            pl.BlockSpec((None, 1, d), lambda i, j: (i // tpb, 0, 0)),
            pl.BlockSpec((d, tn), lambda i, j: (0, j)),
        ],
        out_specs=pl.BlockSpec((tm, tn), lambda i, j: (i, j)),
        out_shape=jax.ShapeDtypeStruct((n, nout), out_dtype),
        scratch_shapes=[pltpu.VMEM((tm, d), BF16)],
        compiler_params=_cparams("parallel", "arbitrary"),
    )(x, g.reshape(1, d), sc, sh, w)


def _norm_matmul(x, col_block, width, g, w, seq, out_dtype):
    n = x.shape[0]
    nout = w.shape[1]
    tm = _tiles(n, seq)
    return pl.pallas_call(
        _norm_matmul_kernel,
        grid=(n // tm, 1),
        in_specs=[
            pl.BlockSpec((tm, width), lambda i, j: (i, col_block)),
            pl.BlockSpec((1, width), lambda i, j: (0, 0)),
            pl.BlockSpec((width, nout), lambda i, j: (0, 0)),
        ],
        out_specs=pl.BlockSpec((tm, nout), lambda i, j: (i, 0)),
        out_shape=jax.ShapeDtypeStruct((n, nout), out_dtype),
        scratch_shapes=[pltpu.VMEM((tm, width), BF16)],
        compiler_params=_cparams("parallel", "arbitrary"),
    )(x, g.reshape(1, width), w)


def _oproj_kernel(a_ref, w_ref, x_ref, gate_ref, o_ref):
    y = jnp.dot(a_ref[...], w_ref[...], preferred_element_type=F32)
    o_ref[...] = x_ref[...] + gate_ref[...] * y


def _oproj_residual(a, w, x, gate, seq):
    n, k = a.shape
    d = w.shape[1]
    tm = _tiles(n, seq)
    tpb = seq // tm
    return pl.pallas_call(
        _oproj_kernel,
        grid=(n // tm,),
        in_specs=[
            pl.BlockSpec((tm, k), lambda i: (i, 0)),
            pl.BlockSpec((k, d), lambda i: (0, 0)),
            pl.BlockSpec((tm, d), lambda i: (i, 0)),
            pl.BlockSpec((None, 1, d), lambda i: (i // tpb, 0, 0)),
        ],
        out_specs=pl.BlockSpec((tm, d), lambda i: (i, 0)),
        out_shape=jax.ShapeDtypeStruct((n, d), F32),
        compiler_params=_cparams("parallel"),
    )(a, w, x, gate)


def _mlp_kernel(x_ref, g_ref, sc_ref, sh_ref, gate_ref, w1_ref, w2_ref, o_ref, h_ref):
    j = pl.program_id(1)

    @pl.when(j == 0)
    def _():
        _store_norm(h_ref, x_ref, g_ref[...] * (1.0 + sc_ref[...]), sh_ref[...])
        o_ref[...] = jnp.zeros_like(o_ref)

    a = jnp.maximum(jnp.dot(h_ref[...], w1_ref[...], preferred_element_type=F32), 0.0)
    o_ref[...] += jnp.dot((a * a).astype(BF16), w2_ref[...], preferred_element_type=F32)

    @pl.when(j == pl.num_programs(1) - 1)
    def _():
        o_ref[...] = x_ref[...] + gate_ref[...] * o_ref[...]


def _mlp_residual(x, g, sc, sh, gate, w1, w2, seq):
    n, d = x.shape
    dff = w1.shape[1]
    tm = _tiles(n, seq)
    tf = min(1024, dff)
    assert dff % tf == 0
    tpb = seq // tm
    mod_spec = pl.BlockSpec((None, 1, d), lambda i, j: (i // tpb, 0, 0))
    return pl.pallas_call(
        _mlp_kernel,
        grid=(n // tm, dff // tf),
        in_specs=[
            pl.BlockSpec((tm, d), lambda i, j: (i, 0)),
            pl.BlockSpec((1, d), lambda i, j: (0, 0)),
            mod_spec, mod_spec, mod_spec,
            pl.BlockSpec((d, tf), lambda i, j: (0, j)),
            pl.BlockSpec((tf, d), lambda i, j: (j, 0)),
        ],
        out_specs=pl.BlockSpec((tm, d), lambda i, j: (i, 0)),
        out_shape=jax.ShapeDtypeStruct((n, d), F32),
        scratch_shapes=[pltpu.VMEM((tm, d), BF16)],
        compiler_params=_cparams("parallel", "arbitrary"),
    )(x, g.reshape(1, d), sc, sh, gate, w1, w2)


def _axial_angles(n_tok, rot_dim):
    rows = n_tok // GRID_W
    row = jnp.repeat(jnp.arange(rows, dtype=F32), GRID_W)
    col = jnp.tile(jnp.arange(GRID_W, dtype=F32), rows)
    n_freq = rot_dim // 4
    inv = ROPE_THETA ** (-jnp.arange(n_freq, dtype=F32) / n_freq)
    ang = jnp.concatenate([row[:, None] * inv, col[:, None] * inv], axis=-1)
    return jnp.cos(ang), jnp.sin(ang)


def _gqa_rope_tables(n_tok):
    cos, sin = _axial_angles(n_tok, GQA_HEAD_DIM)
    return jnp.concatenate([cos, cos], -1), jnp.concatenate([-sin, sin], -1)


def _mla_rope_tables(n_tok):
    cos, sin = _axial_angles(n_tok, MLA_ROPE_DIM)
    z = jnp.zeros_like(cos)
    return jnp.concatenate([cos, z, cos, z], -1), jnp.concatenate([-sin, z, sin, z], -1)


_HALF_SPLIT_128 = np.concatenate([np.arange(0, 128, 2), np.arange(1, 128, 2)])


def _mla_rot_layout(base):
    idx = np.zeros(128, np.int32)
    msk = np.zeros(128, np.float32)
    idx[0:32] = base + np.arange(0, 64, 2)
    idx[64:96] = base + np.arange(1, 64, 2)
    msk[0:32] = 1.0
    msk[64:96] = 1.0
    return idx, msk


def _mla_q_layout():
    ridx, rmsk = _mla_rot_layout(MLA_NOPE_DIM)
    idx = np.concatenate([np.arange(MLA_NOPE_DIM), ridx])
    msk = np.concatenate([np.ones(MLA_NOPE_DIM, np.float32), rmsk])
    return idx, msk


def _rope(x, cos, sin):
    return x * cos + pltpu.roll(x, 64, 1) * sin


LOG2E = math.log2(math.e)


MAX_UNSHIFTED_SCORE = 40.0


def _flash_t(q_t, kn_ref, vt_ref, n_keys, tk, bounded):
    m_cols = q_t.shape[1]
    m = jnp.full((1, m_cols), -jnp.inf, F32)
    l8 = jnp.zeros((8, m_cols), F32)
    acc = jnp.zeros((vt_ref.shape[0], m_cols), F32)
    for c in range(n_keys // tk):
        s = jnp.dot(kn_ref[c * tk:(c + 1) * tk, :], q_t, preferred_element_type=F32)
        if bounded:
            p = jnp.exp2(s)
            l8 = l8 + jnp.sum(p.reshape(tk // 8, 8, m_cols), axis=0)
            acc = acc + jnp.dot(vt_ref[:, c * tk:(c + 1) * tk], p.astype(BF16), preferred_element_type=F32)
        else:
            m_new = jnp.maximum(m, jnp.max(s, axis=0, keepdims=True))
            alpha = jnp.exp2(m - m_new)
            p = jnp.exp2(s - m_new)
            l8 = alpha * l8 + jnp.sum(p.reshape(tk // 8, 8, m_cols), axis=0)
            acc = alpha * acc + jnp.dot(vt_ref[:, c * tk:(c + 1) * tk], p.astype(BF16),
                                        preferred_element_type=F32)
            m = m_new
    return acc * (1.0 / jnp.sum(l8, axis=0, keepdims=True))


def _store_vt(vt_ref, v_load, n_keys, blk):
    for c in range(n_keys // blk):
        vt_ref[:, c * blk:(c + 1) * blk] = v_load(c * blk, blk).astype(F32).T.astype(BF16)


def _gqa_attn_kernel(q_ref, k_ref, v_ref, cq_ref, sq_ref, ck_ref, sk_ref, qg_ref, kg_ref, o_ref, kn_ref, vt_ref,
                     *, tk, kprep, bounded):
    n_keys = k_ref.shape[0]
    tq = q_ref.shape[0]

    @pl.when(pl.program_id(2) == 0)
    def _():
        def prep(c, _):
            rows = pl.ds(pl.multiple_of(c * kprep, kprep), kprep)
            k = _rms(k_ref[rows, :].astype(F32), kg_ref[...], GQA_HEAD_DIM)
            kn_ref[rows, :] = _rope(k, ck_ref[rows, :], sk_ref[rows, :]).astype(BF16)
            return 0
        lax.fori_loop(0, n_keys // kprep, prep, 0)
        _store_vt(vt_ref, lambda r0, nr: v_ref[r0:r0 + nr, :], n_keys, kprep)

    scale = GQA_HEAD_DIM ** -0.5 * LOG2E
    qs = []
    for h in range(GQA_GROUP):
        q = _rms(q_ref[:, h * GQA_HEAD_DIM:(h + 1) * GQA_HEAD_DIM].astype(F32), qg_ref[...], GQA_HEAD_DIM)
        qs.append((_rope(q, cq_ref[...], sq_ref[...]) * scale).T.astype(BF16))
    q_t = jnp.concatenate(qs, axis=1)
    out_t = _flash_t(q_t, kn_ref, vt_ref, n_keys, tk, bounded)
    for h in range(GQA_GROUP):
        o_ref[:, h * GQA_HEAD_DIM:(h + 1) * GQA_HEAD_DIM] = out_t[:, h * tq:(h + 1) * tq].T.astype(o_ref.dtype)


def _dispatch_on_score_bound(bound, attention, *args):
    return lax.cond(bound <= MAX_UNSHIFTED_SCORE,
                    functools.partial(attention, bounded=True),
                    functools.partial(attention, bounded=False), *args)


def _gqa_score_bound(q_gain, k_gain):
    return (GQA_HEAD_DIM * jnp.max(jnp.abs(q_gain)) * jnp.max(jnp.abs(k_gain))) * (GQA_HEAD_DIM ** -0.5 * LOG2E)


def _mla_score_bound(q_gain, kn_gain, kr_gain):
    q_norm = math.sqrt(MLA_QK_DIM) * jnp.max(jnp.abs(q_gain))
    k_norm = jnp.sqrt(MLA_NOPE_DIM * jnp.max(kn_gain * kn_gain) + MLA_ROPE_DIM * jnp.max(kr_gain * kr_gain))
    return q_norm * k_norm * (MLA_QK_DIM ** -0.5 * LOG2E)


def _gqa_attention(qkv, cos, sin, q_gain, k_gain, *, bounded):
    b, t, _ = qkv.shape
    tq = min(256, t)
    tk = min(512, t)
    kprep = min(512, t)
    gw = GQA_GROUP * GQA_HEAD_DIM
    k_col0 = GQA_Q_HEADS
    v_col0 = GQA_Q_HEADS + GQA_KV_HEADS
    tab_q = pl.BlockSpec((tq, LANES), lambda bi, g, qi: (qi, 0))
    tab_k = pl.BlockSpec((t, LANES), lambda bi, g, qi: (0, 0))
    gain = pl.BlockSpec((1, LANES), lambda bi, g, qi: (0, 0))
    return pl.pallas_call(
        functools.partial(_gqa_attn_kernel, tk=tk, kprep=kprep, bounded=bounded),
        grid=(b, GQA_KV_HEADS, t // tq),
        in_specs=[
            pl.BlockSpec((None, tq, gw), lambda bi, g, qi: (bi, qi, g)),
            pl.BlockSpec((None, t, GQA_HEAD_DIM), lambda bi, g, qi: (bi, 0, k_col0 + g)),
            pl.BlockSpec((None, t, GQA_HEAD_DIM), lambda bi, g, qi: (bi, 0, v_col0 + g)),
            tab_q, tab_q, tab_k, tab_k, gain, gain,
        ],
        out_specs=pl.BlockSpec((None, tq, gw), lambda bi, g, qi: (bi, qi, g)),
        out_shape=jax.ShapeDtypeStruct((b, t, GQA_Q_HEADS * GQA_HEAD_DIM), BF16),
        scratch_shapes=[pltpu.VMEM((t, GQA_HEAD_DIM), BF16), pltpu.VMEM((GQA_HEAD_DIM, t), BF16)],
        compiler_params=_cparams("parallel", "parallel", "arbitrary"),
    )(qkv, qkv, qkv, cos, sin, cos, sin, q_gain.reshape(1, LANES), k_gain.reshape(1, LANES))


def _mla_attn_kernel(q_ref, kv_ref, kr_ref, cq_ref, sq_ref, ck_ref, sk_ref, qg_ref, kng_ref, krg_ref,
                     o_ref, kn_ref, vt_ref, *, tk, kprep, bounded):
    n_keys = kv_ref.shape[0]

    @pl.when(pl.program_id(2) == 0)
    def _():
        def prep(c, _):
            rows = pl.ds(pl.multiple_of(c * kprep, kprep), kprep)
            kn = _rms(kv_ref[rows, 0:MLA_NOPE_DIM].astype(F32), kng_ref[...], MLA_NOPE_DIM)
            kn_ref[rows, 0:MLA_NOPE_DIM] = kn.astype(BF16)
            kr = _rms(kr_ref[rows, :].astype(F32), krg_ref[...], MLA_ROPE_DIM)
            kn_ref[rows, MLA_NOPE_DIM:MLA_QPAD] = _rope(kr, ck_ref[rows, :], sk_ref[rows, :]).astype(BF16)
            return 0
        lax.fori_loop(0, n_keys // kprep, prep, 0)
        _store_vt(vt_ref, lambda r0, nr: kv_ref[r0:r0 + nr, MLA_NOPE_DIM:MLA_QPAD], n_keys, kprep)

    scale = MLA_QK_DIM ** -0.5 * LOG2E
    q = q_ref[...].astype(F32)
    g = qg_ref[...]
    inv = lax.rsqrt(jnp.sum(q * q, axis=-1, keepdims=True) * (1.0 / MLA_QK_DIM) + NORM_EPS)
    q_nope = q[:, 0:MLA_NOPE_DIM] * inv * g[:, 0:MLA_NOPE_DIM]
    q_rot = _rope(q[:, MLA_NOPE_DIM:MLA_QPAD] * inv * g[:, MLA_NOPE_DIM:MLA_QPAD], cq_ref[...], sq_ref[...])
    q_t = jnp.concatenate([(q_nope * scale).T, (q_rot * scale).T], axis=0).astype(BF16)
    out_t = _flash_t(q_t, kn_ref, vt_ref, n_keys, tk, bounded)
    o_ref[...] = out_t.T.astype(o_ref.dtype)


def _mla_attention(q, kv, down, cos, sin, q_gain, kn_gain, kr_gain, *, bounded):
    b, t, _ = q.shape
    tq = min(1024, t)
    tk = min(512, t)
    kprep = min(512, t)
    kr_col = (MLA_Q_RANK + MLA_KV_RANK) // LANES
    tab_q = pl.BlockSpec((tq, LANES), lambda bi, h, qi: (qi, 0))
    tab_k = pl.BlockSpec((t, LANES), lambda bi, h, qi: (0, 0))
    return pl.pallas_call(
        functools.partial(_mla_attn_kernel, tk=tk, kprep=kprep, bounded=bounded),
        grid=(b, MLA_HEADS, t // tq),
        in_specs=[
            pl.BlockSpec((None, tq, MLA_QPAD), lambda bi, h, qi: (bi, qi, h)),
            pl.BlockSpec((None, t, MLA_QPAD), lambda bi, h, qi: (bi, 0, h)),
            pl.BlockSpec((None, t, LANES), lambda bi, h, qi: (bi, 0, kr_col)),
            tab_q, tab_q, tab_k, tab_k,
            pl.BlockSpec((1, MLA_QPAD), lambda bi, h, qi: (0, 0)),
            pl.BlockSpec((1, LANES), lambda bi, h, qi: (0, 0)),
            pl.BlockSpec((1, LANES), lambda bi, h, qi: (0, 0)),
        ],
        out_specs=pl.BlockSpec((None, tq, MLA_V_DIM), lambda bi, h, qi: (bi, qi, h)),
        out_shape=jax.ShapeDtypeStruct((b, t, MLA_HEADS * MLA_V_DIM), BF16),
        scratch_shapes=[pltpu.VMEM((t, MLA_QPAD), BF16), pltpu.VMEM((MLA_V_DIM, t), BF16)],
        compiler_params=_cparams("parallel", "parallel", "arbitrary"),
    )(q, kv, down, cos, sin, cos, sin, q_gain.reshape(1, MLA_QPAD), kn_gain.reshape(1, LANES),
      kr_gain.reshape(1, LANES))


def _split3(x):
    hi = x.astype(BF16)
    r1 = x - hi.astype(F32)
    mid = r1.astype(BF16)
    lo = (r1 - mid.astype(F32)).astype(BF16)
    return hi, mid, lo


def _gla_kernel(*refs, reverse, n_chunks):
    if reverse:
        (q_ref, k_ref, v_ref, low_ref, wg_ref, bg_ref, of_ref, r_ref, on_ref, o_ref, s_ref) = refs
    else:
        (q_ref, k_ref, v_ref, low_ref, wg_ref, bg_ref, o_ref, s_ref) = refs
    L = GLA_BLOCK

    @pl.when(pl.program_id(2) == 0)
    def _():
        s_ref[...] = jnp.zeros_like(s_ref)

    row = lax.broadcasted_iota(jnp.int32, (L, L), 0)
    col = lax.broadcasted_iota(jnp.int32, (L, L), 1)
    keep = (col >= row) if reverse else (col <= row)
    tri = jnp.where(keep, 1.0, 0.0).astype(BF16)
    tn_dims = (((0,), (0,)), ((), ()))
    nt_dims = (((1,), (1,)), ((), ()))
    mid = L // 2 if reverse else L // 2 - 1
    edge = 0 if reverse else L - 1

    order = list(range(n_chunks - 1, -1, -1) if reverse else range(n_chunks))
    rows = {c: slice(c * L, (c + 1) * L) for c in order}

    z = jnp.dot(low_ref[...].astype(BF16), wg_ref[...], preferred_element_type=F32) + bg_ref[...]
    cum = {}
    for c in order:
        zc = z[rows[c], :]
        la = (jnp.minimum(zc, 0.0) - jnp.log1p(jnp.exp(-jnp.abs(zc)))) * (1.0 / GLA_GATE_TAU)
        cum[c] = sum(jnp.dot(tri, p, preferred_element_type=F32) for p in _split3(la))
    qd, qs, kd, ke, de = {}, {}, {}, {}, {}
    for c in order:
        ref_row = cum[c][mid:mid + 1, :]
        tot_row = cum[c][edge:edge + 1, :]
        q = q_ref[rows[c], :].astype(F32) * (GLA_DK_HEAD ** -0.5)
        k = k_ref[rows[c], :].astype(F32)
        q_mid = q * jnp.exp(cum[c] - ref_row)
        qd[c] = q_mid.astype(BF16)
        qs[c] = (q_mid * jnp.exp(ref_row)).astype(BF16)
        kd[c] = (k * jnp.exp(ref_row - cum[c])).astype(BF16)
        ke[c] = (k * jnp.exp(tot_row - cum[c])).astype(BF16)
        col_de = jnp.broadcast_to(jnp.exp(tot_row), (LANES, GLA_DK_HEAD)).T
        de[c] = jnp.concatenate([col_de] * (GLA_DV_HEAD // LANES), axis=1)
    a = {c: jnp.where(keep, lax.dot_general(qd[c], kd[c], nt_dims, preferred_element_type=F32), 0.0).astype(BF16)
         for c in order}
    o_intra = {c: jnp.dot(a[c], v_ref[rows[c], :], preferred_element_type=F32) for c in order}
    kv = {c: lax.dot_general(ke[c], v_ref[rows[c], :], tn_dims, preferred_element_type=F32) for c in order}

    s = s_ref[...]
    for c in order:
        o = o_intra[c] + jnp.dot(qs[c], s.astype(BF16), preferred_element_type=F32)
        s = de[c] * s + kv[c]
        if reverse:
            o = o + of_ref[rows[c], :]
            o = _rms(o, on_ref[...], GLA_DV_HEAD)
            r = r_ref[rows[c], :].astype(F32)
            o = o * (r * (1.0 / (1.0 + jnp.exp(-r))))
        o_ref[rows[c], :] = o.astype(o_ref.dtype)
    s_ref[...] = s


def _gla_direction(proj, low, wg, bg, o_fwd, o_norm, reverse):
    b, t, _ = proj.shape
    tb = min(1024, t)
    assert t % tb == 0 and tb % GLA_BLOCK == 0
    nblk = t // tb
    blk = (lambda i: nblk - 1 - i) if reverse else (lambda i: i)
    k_col0 = GLA_DK // GLA_DK_HEAD
    v_col0 = 2 * GLA_DK // GLA_DV_HEAD
    r_col0 = (2 * GLA_DK + GLA_DV) // GLA_DV_HEAD
    in_specs = [
        pl.BlockSpec((None, tb, GLA_DK_HEAD), lambda bi, h, i: (bi, blk(i), h)),
        pl.BlockSpec((None, tb, GLA_DK_HEAD), lambda bi, h, i: (bi, blk(i), k_col0 + h)),
        pl.BlockSpec((None, tb, GLA_DV_HEAD), lambda bi, h, i: (bi, blk(i), v_col0 + h)),
        pl.BlockSpec((None, tb, LANES), lambda bi, h, i: (bi, blk(i), 0)),
        pl.BlockSpec((LANES, GLA_DK_HEAD), lambda bi, h, i: (0, h)),
        pl.BlockSpec((1, GLA_DK_HEAD), lambda bi, h, i: (0, h)),
    ]
    args = [proj, proj, proj, low, wg, bg]
    if reverse:
        in_specs += [
            pl.BlockSpec((None, tb, GLA_DV_HEAD), lambda bi, h, i: (bi, blk(i), h)),
            pl.BlockSpec((None, tb, GLA_DV_HEAD), lambda bi, h, i: (bi, blk(i), r_col0 + h)),
            pl.BlockSpec((1, GLA_DV_HEAD), lambda bi, h, i: (0, 0)),
        ]
        args += [o_fwd, proj, o_norm.reshape(1, GLA_DV_HEAD)]
    return pl.pallas_call(
        functools.partial(_gla_kernel, reverse=reverse, n_chunks=tb // GLA_BLOCK),
        grid=(b, GLA_HEADS, nblk),
        in_specs=in_specs,
        out_specs=pl.BlockSpec((None, tb, GLA_DV_HEAD), lambda bi, h, i: (bi, blk(i), h)),
        out_shape=jax.ShapeDtypeStruct((b, t, GLA_DV), BF16 if reverse else F32),
        scratch_shapes=[pltpu.VMEM((GLA_DK_HEAD, GLA_DV_HEAD), F32)],
        compiler_params=_cparams("parallel", "parallel", "arbitrary"),
    )(*args)


def _prep_gqa(w_qkv, q_norm, k_norm):
    n_rot = GQA_Q_HEADS + GQA_KV_HEADS
    cols = (np.arange(n_rot)[:, None] * GQA_HEAD_DIM + _HALF_SPLIT_128[None, :]).reshape(-1)
    cols = np.concatenate([cols, np.arange(n_rot * GQA_HEAD_DIM, w_qkv.shape[1])])
    return w_qkv[:, cols].astype(BF16), q_norm[_HALF_SPLIT_128], k_norm[_HALF_SPLIT_128]


def _prep_mla(w_down, w_uq, q_norm, k_rope_norm):
    qidx, qmsk = _mla_q_layout()
    cols = (np.arange(MLA_HEADS)[:, None] * MLA_QK_DIM + qidx[None, :]).reshape(-1)
    w_uq_p = (w_uq[:, cols] * jnp.asarray(np.tile(qmsk, MLA_HEADS))).astype(BF16)
    q_gain = q_norm[qidx] * jnp.asarray(qmsk)
    ridx, rmsk = _mla_rot_layout(0)
    kr0 = MLA_Q_RANK + MLA_KV_RANK
    w_kr = w_down[:, kr0 + ridx] * jnp.asarray(rmsk)
    w_down_p = jnp.concatenate([w_down[:, :kr0], w_kr], axis=1).astype(BF16)
    kr_gain = k_rope_norm[ridx] * jnp.asarray(rmsk)
    return w_down_p, w_uq_p, q_gain, kr_gain


def _prep_gla_gates(w_g1, w_g2):
    r = GLA_GATE_RANK
    d = w_g1.shape[1]
    w1 = jnp.zeros((d, LANES), F32).at[:, 0:r].set(w_g1[0]).at[:, r:2 * r].set(w_g1[1])
    w2 = jnp.zeros((2, LANES, GLA_DK), F32).at[0, 0:r].set(w_g2[0]).at[1, r:2 * r].set(w_g2[1])
    return w1.astype(BF16), w2.astype(BF16)


def _trunk(x3, mod, p):
    b, t, d = x3.shape
    n = b * t
    x = x3.reshape(n, d)
    for i in range(DEPTH):
        m = mod[i].reshape(b, 6, 1, d)
        sh1, sc1, g1, sh2, sc2, g2 = (m[:, s] for s in range(6))
        kind, j = i % N_MIXERS, i // N_MIXERS
        if kind == 0:
            qkv = _norm_mod_matmul(x, p['norm1_g'][i], sc1, sh1, p['gqa_w_qkv'][j], t, BF16)
            cos, sin = _gqa_rope_tables(t)
            a = _dispatch_on_score_bound(_gqa_score_bound(p['gqa_q_norm'][j], p['gqa_k_norm'][j]), _gqa_attention,
                                         qkv.reshape(b, t, -1), cos, sin, p['gqa_q_norm'][j], p['gqa_k_norm'][j])
            x = _oproj_residual(a.reshape(n, -1), p['gqa_w_o'][j], x, g1, t)
        elif kind == 1:
            proj = _norm_mod_matmul(x, p['norm1_g'][i], sc1, sh1, p['gla_w_in'][j], t, BF16, tn=2048)
            low = _norm_mod_matmul(x, p['norm1_g'][i], sc1, sh1, p['gla_w_g1'][j], t, F32)
            proj3, low3 = proj.reshape(b, t, -1), low.reshape(b, t, -1)
            bg = p['gla_b_g'][j]
            o_f = _gla_direction(proj3, low3, p['gla_w_g2'][j][0], bg[0:1], None, None, reverse=False)
            a = _gla_direction(proj3, low3, p['gla_w_g2'][j][1], bg[1:2], o_f, p['gla_o_norm'][j], reverse=True)
            x = _oproj_residual(a.reshape(n, -1), p['gla_w_o'][j], x, g1, t)
        else:
            down = _norm_mod_matmul(x, p['norm1_g'][i], sc1, sh1, p['mla_w_down'][j], t, F32)
            q = _norm_matmul(down, 0, MLA_Q_RANK, p['mla_q_a_norm'][j], p['mla_w_uq'][j], t, BF16)
            kv = _norm_matmul(down, MLA_Q_RANK // MLA_KV_RANK, MLA_KV_RANK, p['mla_kv_a_norm'][j],
                              p['mla_w_ukv'][j], t, BF16)
            cos, sin = _mla_rope_tables(t)
            gains = (p['mla_q_gain'][j], p['mla_k_nope_norm'][j], p['mla_kr_gain'][j])
            a = _dispatch_on_score_bound(_mla_score_bound(*gains), _mla_attention, q.reshape(b, t, -1),
                                         kv.reshape(b, t, -1), down.reshape(b, t, -1), cos, sin, *gains)
            x = _oproj_residual(a.reshape(n, -1), p['mla_w_o'][j], x, g1, t)
        x = _mlp_residual(x, p['norm2_g'][i], sc2, sh2, g2, p['mlp_w1'][i], p['mlp_w2'][i], t)
    return x.reshape(b, t, d)


def kernel(x_prompt, x_sample, c_prompt, c_sample, norm1_g, norm2_g, ada_w, ada_b, gqa_w_qkv, gqa_q_norm, gqa_k_norm, gqa_w_o, gla_w_in, gla_w_g1, gla_w_g2, gla_b_g, gla_o_norm, gla_w_o, mla_w_down, mla_q_a_norm, mla_kv_a_norm, mla_w_uq, mla_w_ukv, mla_q_norm, mla_k_nope_norm, mla_k_rope_norm, mla_w_o, mlp_w1, mlp_w2):
    nA, nB, nC = gqa_w_qkv.shape[0], gla_w_in.shape[0], mla_w_down.shape[0]
    gqa = [_prep_gqa(gqa_w_qkv[j], gqa_q_norm[j], gqa_k_norm[j]) for j in range(nA)]
    mla = [_prep_mla(mla_w_down[j], mla_w_uq[j], mla_q_norm[j], mla_k_rope_norm[j]) for j in range(nC)]
    gates = [_prep_gla_gates(gla_w_g1[j], gla_w_g2[j]) for j in range(nB)]
    p = dict(
        norm1_g=norm1_g, norm2_g=norm2_g,
        gqa_w_qkv=[g[0] for g in gqa], gqa_q_norm=[g[1] for g in gqa], gqa_k_norm=[g[2] for g in gqa],
        gqa_w_o=gqa_w_o.astype(BF16),
        gla_w_in=gla_w_in.astype(BF16), gla_w_g1=[g[0] for g in gates], gla_w_g2=[g[1] for g in gates],
        gla_b_g=gla_b_g, gla_o_norm=gla_o_norm, gla_w_o=gla_w_o.astype(BF16),
        mla_w_down=[m[0] for m in mla], mla_w_uq=[m[1] for m in mla], mla_q_gain=[m[2] for m in mla],
        mla_kr_gain=[m[3] for m in mla], mla_q_a_norm=mla_q_a_norm, mla_kv_a_norm=mla_kv_a_norm,
        mla_w_ukv=mla_w_ukv.astype(BF16), mla_k_nope_norm=mla_k_nope_norm, mla_w_o=mla_w_o.astype(BF16),
        mlp_w1=mlp_w1.astype(BF16), mlp_w2=mlp_w2.astype(BF16),
    )
    bp, bs = c_prompt.shape[0], c_sample.shape[0]
    nb_pad = -(-(bp + bs) // 16) * 16
    c_all = jnp.concatenate([c_prompt, c_sample, jnp.zeros((nb_pad - bp - bs, c_prompt.shape[1]), F32)], axis=0)
    mod = _ada_modulation(c_all, ada_w, ada_b)
    y_prompt = _trunk(x_prompt, mod[:, :bp], p)
    y_sample = _trunk(x_sample, mod[:, bp:bp + bs], p)
    return (y_prompt, y_sample)
```

```python
import functools
import math

import numpy as np
import jax
import jax.numpy as jnp
from jax import lax
from jax.experimental import pallas as pl
from jax.experimental.pallas import tpu as pltpu

F32 = jnp.float32
BF16 = jnp.bfloat16

D_MODEL = 2048
DEPTH = 4
GRID_W = 64
ROPE_THETA = 10000.0
NORM_EPS = 1e-6
N_MIXERS = 3

GQA_HEAD_DIM = 128
GQA_Q_HEADS = 16
GQA_KV_HEADS = 4
GQA_GROUP = 4

GLA_HEADS = 4
GLA_DK = 1024
GLA_DV = 2048
GLA_DK_HEAD = 256
GLA_DV_HEAD = 512
GLA_GATE_RANK = 16
GLA_GATE_TAU = 16.0
GLA_BLOCK = 128

MLA_HEADS = 16
MLA_Q_RANK = 512
MLA_KV_RANK = 256
MLA_NOPE_DIM = 128
MLA_ROPE_DIM = 64
MLA_QK_DIM = 192
MLA_V_DIM = 128
MLA_QPAD = 256
MLA_DOWN_PAD = 896

D_FF = 4 * D_MODEL

LANES = 128
VMEM_LIMIT_BYTES = 56 * 1024 * 1024


def _tiles(n_rows, seq):
    tm = min(512, seq)
    assert seq % tm == 0 and n_rows % tm == 0
    return tm


def _cparams(*sem):
    return pltpu.CompilerParams(dimension_semantics=sem, vmem_limit_bytes=VMEM_LIMIT_BYTES)


def _ada_kernel(c_ref, w_ref, b_ref, o_ref):
    c = c_ref[...]
    cond = (c * (1.0 / (1.0 + jnp.exp(-c)))).astype(BF16)
    o_ref[...] = jnp.dot(cond, w_ref[...].astype(BF16), preferred_element_type=F32) + b_ref[...]


def _ada_modulation(c_all, ada_w, ada_b):
    nb, d = c_all.shape
    depth, _, n6 = ada_w.shape
    tn = 1024
    return pl.pallas_call(
        _ada_kernel,
        grid=(depth, n6 // tn),
        in_specs=[
            pl.BlockSpec((nb, d), lambda i, j: (0, 0)),
            pl.BlockSpec((None, d, tn), lambda i, j: (i, 0, j)),
            pl.BlockSpec((None, 1, tn), lambda i, j: (i, 0, j)),
        ],
        out_specs=pl.BlockSpec((None, nb, tn), lambda i, j: (i, 0, j)),
        out_shape=jax.ShapeDtypeStruct((depth, nb, n6), F32),
        compiler_params=_cparams("parallel", "parallel"),
    )(c_all, ada_w, ada_b.reshape(depth, 1, n6))


def _rms(x, g, n):
    ms = jnp.sum(x * x, axis=-1, keepdims=True) * (1.0 / n)
    return x * lax.rsqrt(ms + NORM_EPS) * g


NORM_ROWS = 16


def _store_norm(h_ref, x_ref, gain, shift):
    d = x_ref.shape[-1]
    blocks = [slice(r, r + NORM_ROWS) for r in range(0, x_ref.shape[0], NORM_ROWS)]
    inv = []
    for rows in blocks:
        x = x_ref[rows, :].astype(F32)
        inv.append(lax.rsqrt(jnp.sum(x * x, axis=-1, keepdims=True) * (1.0 / d) + NORM_EPS))
    for rows, inv_r in zip(blocks, inv):
        y = x_ref[rows, :].astype(F32) * inv_r * gain
        h_ref[rows, :] = (y if shift is None else y + shift).astype(BF16)


def _norm_mod_matmul_kernel(x_ref, g_ref, sc_ref, sh_ref, w_ref, o_ref, h_ref):
    @pl.when(pl.program_id(1) == 0)
    def _():
        _store_norm(h_ref, x_ref, g_ref[...] * (1.0 + sc_ref[...]), sh_ref[...])

    o_ref[...] = jnp.dot(h_ref[...], w_ref[...], preferred_element_type=F32).astype(o_ref.dtype)


def _norm_matmul_kernel(x_ref, g_ref, w_ref, o_ref, h_ref):
    @pl.when(pl.program_id(1) == 0)
    def _():
        _store_norm(h_ref, x_ref, g_ref[...], None)

    o_ref[...] = jnp.dot(h_ref[...], w_ref[...], preferred_element_type=F32).astype(o_ref.dtype)


def _norm_mod_matmul(x, g, sc, sh, w, seq, out_dtype, tn=None):
    n, d = x.shape
    nout = w.shape[1]
    tm = _tiles(n, seq)
    tn = nout if tn is None else tn
    tpb = seq // tm
    return pl.pallas_call(
        _norm_mod_matmul_kernel,
        grid=(n // tm, nout // tn),
        in_specs=[
            pl.BlockSpec((tm, d), lambda i, j: (i, 0)),
            pl.BlockSpec((1, d), lambda i, j: (0, 0)),
            pl.BlockSpec((None, 1, d), lambda i, j: (i // tpb, 0, 0)),
            pl.BlockSpec((None, 1, d), lambda i, j: (i // tpb, 0, 0)),
            pl.BlockSpec((d, tn), lambda i, j: (0, j)),
        ],
        out_specs=pl.BlockSpec((tm, tn), lambda i, j: (i, j)),
        out_shape=jax.ShapeDtypeStruct((n, nout), out_dtype),
        scratch_shapes=[pltpu.VMEM((tm, d), BF16)],
        compiler_params=_cparams("parallel", "arbitrary"),
    )(x, g.reshape(1, d), sc, sh, w)


def _norm_matmul(x, col_block, width, g, w, seq, out_dtype):
    n = x.shape[0]
    nout = w.shape[1]
    tm = _tiles(n, seq)
    return pl.pallas_call(
        _norm_matmul_kernel,
        grid=(n // tm, 1),
        in_specs=[
            pl.BlockSpec((tm, width), lambda i, j: (i, col_block)),
            pl.BlockSpec((1, width), lambda i, j: (0, 0)),
            pl.BlockSpec((width, nout), lambda i, j: (0, 0)),
        ],
        out_specs=pl.BlockSpec((tm, nout), lambda i, j: (i, 0)),
        out_shape=jax.ShapeDtypeStruct((n, nout), out_dtype),
        scratch_shapes=[pltpu.VMEM((tm, width), BF16)],
        compiler_params=_cparams("parallel", "arbitrary"),
    )(x, g.reshape(1, width), w)


def _oproj_kernel(a_ref, w_ref, x_ref, gate_ref, o_ref):
    y = jnp.dot(a_ref[...], w_ref[...], preferred_element_type=F32)
    o_ref[...] = x_ref[...] + gate_ref[...] * y


def _oproj_residual(a, w, x, gate, seq):
    n, k = a.shape
    d = w.shape[1]
    tm = _tiles(n, seq)
    tpb = seq // tm
    return pl.pallas_call(
        _oproj_kernel,
        grid=(n // tm,),
        in_specs=[
            pl.BlockSpec((tm, k), lambda i: (i, 0)),
            pl.BlockSpec((k, d), lambda i: (0, 0)),
            pl.BlockSpec((tm, d), lambda i: (i, 0)),
            pl.BlockSpec((None, 1, d), lambda i: (i // tpb, 0, 0)),
        ],
        out_specs=pl.BlockSpec((tm, d), lambda i: (i, 0)),
        out_shape=jax.ShapeDtypeStruct((n, d), F32),
        compiler_params=_cparams("parallel"),
    )(a, w, x, gate)


def _mlp_kernel(x_ref, g_ref, sc_ref, sh_ref, gate_ref, w1_ref, w2_ref, o_ref, h_ref):
    j = pl.program_id(1)

    @pl.when(j == 0)
    def _():
        _store_norm(h_ref, x_ref, g_ref[...] * (1.0 + sc_ref[...]), sh_ref[...])
        o_ref[...] = jnp.zeros_like(o_ref)

    a = jnp.maximum(jnp.dot(h_ref[...], w1_ref[...], preferred_element_type=F32), 0.0)
    o_ref[...] += jnp.dot((a * a).astype(BF16), w2_ref[...], preferred_element_type=F32)

    @pl.when(j == pl.num_programs(1) - 1)
    def _():
        o_ref[...] = x_ref[...] + gate_ref[...] * o_ref[...]


def _mlp_residual(x, g, sc, sh, gate, w1, w2, seq):
    n, d = x.shape
    dff = w1.shape[1]
    tm = _tiles(n, seq)
    tf = min(1024, dff)
    assert dff % tf == 0
    tpb = seq // tm
    mod_spec = pl.BlockSpec((None, 1, d), lambda i, j: (i // tpb, 0, 0))
    return pl.pallas_call(
        _mlp_kernel,
        grid=(n // tm, dff // tf),
        in_specs=[
            pl.BlockSpec((tm, d), lambda i, j: (i, 0)),
            pl.BlockSpec((1, d), lambda i, j: (0, 0)),
            mod_spec, mod_spec, mod_spec,
            pl.BlockSpec((d, tf), lambda i, j: (0, j)),
            pl.BlockSpec((tf, d), lambda i, j: (j, 0)),
        ],
        out_specs=pl.BlockSpec((tm, d), lambda i, j: (i, 0)),
        out_shape=jax.ShapeDtypeStruct((n, d), F32),
        scratch_shapes=[pltpu.VMEM((tm, d), BF16)],
        compiler_params=_cparams("parallel", "arbitrary"),
    )(x, g.reshape(1, d), sc, sh, gate, w1, w2)


def _axial_angles(n_tok, rot_dim):
    rows = n_tok // GRID_W
    row = jnp.repeat(jnp.arange(rows, dtype=F32), GRID_W)
    col = jnp.tile(jnp.arange(GRID_W, dtype=F32), rows)
    n_freq = rot_dim // 4
    inv = ROPE_THETA ** (-jnp.arange(n_freq, dtype=F32) / n_freq)
    ang = jnp.concatenate([row[:, None] * inv, col[:, None] * inv], axis=-1)
    return jnp.cos(ang), jnp.sin(ang)


def _gqa_rope_tables(n_tok):
    cos, sin = _axial_angles(n_tok, GQA_HEAD_DIM)
    return jnp.concatenate([cos, cos], -1), jnp.concatenate([-sin, sin], -1)


def _mla_rope_tables(n_tok):
    cos, sin = _axial_angles(n_tok, MLA_ROPE_DIM)
    z = jnp.zeros_like(cos)
    return jnp.concatenate([cos, z, cos, z], -1), jnp.concatenate([-sin, z, sin, z], -1)


_HALF_SPLIT_128 = np.concatenate([np.arange(0, 128, 2), np.arange(1, 128, 2)])


def _mla_rot_layout(base):
    idx = np.zeros(128, np.int32)
    msk = np.zeros(128, np.float32)
    idx[0:32] = base + np.arange(0, 64, 2)
    idx[64:96] = base + np.arange(1, 64, 2)
    msk[0:32] = 1.0
    msk[64:96] = 1.0
    return idx, msk


def _mla_q_layout():
    ridx, rmsk = _mla_rot_layout(MLA_NOPE_DIM)
    idx = np.concatenate([np.arange(MLA_NOPE_DIM), ridx])
    msk = np.concatenate([np.ones(MLA_NOPE_DIM, np.float32), rmsk])
    return idx, msk


def _rope(x, cos, sin):
    return x * cos + pltpu.roll(x, 64, 1) * sin


LOG2E = math.log2(math.e)


MAX_UNSHIFTED_SCORE = 40.0


def _flash_t(q_t, kn_ref, vt_ref, n_keys, tk, bounded):
    m_cols = q_t.shape[1]
    m = jnp.full((1, m_cols), -jnp.inf, F32)
    l8 = jnp.zeros((8, m_cols), F32)
    acc = jnp.zeros((vt_ref.shape[0], m_cols), F32)
    for c in range(n_keys // tk):
        s = jnp.dot(kn_ref[c * tk:(c + 1) * tk, :], q_t, preferred_element_type=F32)
        if bounded:
            p = jnp.exp2(s)
            l8 = l8 + jnp.sum(p.reshape(tk // 8, 8, m_cols), axis=0)
            acc = acc + jnp.dot(vt_ref[:, c * tk:(c + 1) * tk], p.astype(BF16), preferred_element_type=F32)
        else:
            m_new = jnp.maximum(m, jnp.max(s, axis=0, keepdims=True))
            alpha = jnp.exp2(m - m_new)
            p = jnp.exp2(s - m_new)
            l8 = alpha * l8 + jnp.sum(p.reshape(tk // 8, 8, m_cols), axis=0)
            acc = alpha * acc + jnp.dot(vt_ref[:, c * tk:(c + 1) * tk], p.astype(BF16),
                                        preferred_element_type=F32)
            m = m_new
    return acc, l8


def _pipelined_attention(step, n_q, prep_kv, prep_q_first, prep_q_next, store_out, kn_ref, vt_ref, qt_ref,
                         acc_ref, l_ref, n_keys, tk, bounded):
    def finalize():
        store_out(acc_ref[...] * (1.0 / jnp.sum(l_ref[...], axis=0, keepdims=True)))

    @pl.when(step == 0)
    def _():
        prep_kv()
        qt_ref[0] = prep_q_first()
        acc_ref[...] = jnp.zeros_like(acc_ref)
        l_ref[...] = jnp.ones_like(l_ref)

    @pl.when(step < n_q)
    def _():
        finalize()
        slot = lax.rem(step, 2)
        q_t = qt_ref[slot]
        qt_ref[1 - slot] = prep_q_next()
        acc, l8 = _flash_t(q_t, kn_ref, vt_ref, n_keys, tk, bounded)
        acc_ref[...] = acc
        l_ref[...] = l8

    @pl.when(step == n_q)
    def _():
        finalize()


def _store_vt(vt_ref, v_load, n_keys, blk):
    for c in range(n_keys // blk):
        vt_ref[:, c * blk:(c + 1) * blk] = v_load(c * blk, blk).astype(F32).T.astype(BF16)


def _gqa_attn_kernel(q0_ref, qn_ref, k_ref, v_ref, c0_ref, s0_ref, cn_ref, sn_ref, ck_ref, sk_ref, qg_ref, kg_ref,
                     o_ref, kn_ref, vt_ref, qt_ref, acc_ref, l_ref, *, n_q, tk, kprep, bounded):
    n_keys = k_ref.shape[0]
    tq = o_ref.shape[0]
    scale = GQA_HEAD_DIM ** -0.5 * LOG2E

    def prep_kv():
        def prep(c, _):
            rows = pl.ds(pl.multiple_of(c * kprep, kprep), kprep)
            k = _rms(k_ref[rows, :].astype(F32), kg_ref[...], GQA_HEAD_DIM)
            kn_ref[rows, :] = _rope(k, ck_ref[rows, :], sk_ref[rows, :]).astype(BF16)
            return 0
        lax.fori_loop(0, n_keys // kprep, prep, 0)
        _store_vt(vt_ref, lambda r0, nr: v_ref[r0:r0 + nr, :], n_keys, kprep)

    def prep_q(q_ref, cos_ref, sin_ref):
        qs = []
        for h in range(GQA_GROUP):
            q = _rms(q_ref[:, h * GQA_HEAD_DIM:(h + 1) * GQA_HEAD_DIM].astype(F32), qg_ref[...], GQA_HEAD_DIM)
            qs.append((_rope(q, cos_ref[...], sin_ref[...]) * scale).T.astype(BF16))
        return jnp.concatenate(qs, axis=1)

    def store_out(out_t):
        for h in range(GQA_GROUP):
            o_ref[:, h * GQA_HEAD_DIM:(h + 1) * GQA_HEAD_DIM] = (
                out_t[:, h * tq:(h + 1) * tq].T.astype(o_ref.dtype))

    _pipelined_attention(pl.program_id(2), n_q, prep_kv, functools.partial(prep_q, q0_ref, c0_ref, s0_ref),
                         functools.partial(prep_q, qn_ref, cn_ref, sn_ref), store_out, kn_ref, vt_ref, qt_ref,
                         acc_ref, l_ref, n_keys, tk, bounded)


def _dispatch_on_score_bound(bound, attention, *args):
    return lax.cond(bound <= MAX_UNSHIFTED_SCORE,
                    functools.partial(attention, bounded=True),
                    functools.partial(attention, bounded=False), *args)


def _gqa_score_bound(q_gain, k_gain):
    return (GQA_HEAD_DIM * jnp.max(jnp.abs(q_gain)) * jnp.max(jnp.abs(k_gain))) * (GQA_HEAD_DIM ** -0.5 * LOG2E)


def _mla_score_bound(q_gain, kn_gain, kr_gain):
    q_norm = math.sqrt(MLA_QK_DIM) * jnp.max(jnp.abs(q_gain))
    k_norm = jnp.sqrt(MLA_NOPE_DIM * jnp.max(kn_gain * kn_gain) + MLA_ROPE_DIM * jnp.max(kr_gain * kr_gain))
    return q_norm * k_norm * (MLA_QK_DIM ** -0.5 * LOG2E)


def _gqa_attention(qkv, cos, sin, q_gain, k_gain, *, bounded):
    b, t, _ = qkv.shape
    tq = min(256, t)
    tk = min(512, t)
    kprep = min(512, t)
    gw = GQA_GROUP * GQA_HEAD_DIM
    k_col0 = GQA_Q_HEADS
    v_col0 = GQA_Q_HEADS + GQA_KV_HEADS
    n_q = t // tq
    nxt = lambda s: jnp.minimum(s + 1, n_q - 1)
    prv = lambda s: jnp.maximum(s - 1, 0)
    tab_0 = pl.BlockSpec((tq, LANES), lambda bi, g, s: (0, 0))
    tab_n = pl.BlockSpec((tq, LANES), lambda bi, g, s: (nxt(s), 0))
    tab_k = pl.BlockSpec((t, LANES), lambda bi, g, s: (0, 0))
    gain = pl.BlockSpec((1, LANES), lambda bi, g, s: (0, 0))
    m_cols = GQA_GROUP * tq
    return pl.pallas_call(
        functools.partial(_gqa_attn_kernel, n_q=n_q, tk=tk, kprep=kprep, bounded=bounded),
        grid=(b, GQA_KV_HEADS, n_q + 1),
        in_specs=[
            pl.BlockSpec((None, tq, gw), lambda bi, g, s: (bi, 0, g)),
            pl.BlockSpec((None, tq, gw), lambda bi, g, s: (bi, nxt(s), g)),
            pl.BlockSpec((None, t, GQA_HEAD_DIM), lambda bi, g, s: (bi, 0, k_col0 + g)),
            pl.BlockSpec((None, t, GQA_HEAD_DIM), lambda bi, g, s: (bi, 0, v_col0 + g)),
            tab_0, tab_0, tab_n, tab_n, tab_k, tab_k, gain, gain,
        ],
        out_specs=pl.BlockSpec((None, tq, gw), lambda bi, g, s: (bi, prv(s), g)),
        out_shape=jax.ShapeDtypeStruct((b, t, GQA_Q_HEADS * GQA_HEAD_DIM), BF16),
        scratch_shapes=[pltpu.VMEM((t, GQA_HEAD_DIM), BF16), pltpu.VMEM((GQA_HEAD_DIM, t), BF16),
                        pltpu.VMEM((2, GQA_HEAD_DIM, m_cols), BF16), pltpu.VMEM((GQA_HEAD_DIM, m_cols), F32),
                        pltpu.VMEM((8, m_cols), F32)],
        compiler_params=_cparams("parallel", "parallel", "arbitrary"),
    )(qkv, qkv, qkv, qkv, cos, sin, cos, sin, cos, sin, q_gain.reshape(1, LANES), k_gain.reshape(1, LANES))


def _mla_attn_kernel(q0_ref, qn_ref, kv_ref, kr_ref, c0_ref, s0_ref, cn_ref, sn_ref, ck_ref, sk_ref, qg_ref,
                     kng_ref, krg_ref, o_ref, kn_ref, vt_ref, qt_ref, acc_ref, l_ref,
                     *, n_q, tk, kprep, bounded):
    n_keys = kv_ref.shape[0]
    scale = MLA_QK_DIM ** -0.5 * LOG2E

    def prep_kv():
        def prep(c, _):
            rows = pl.ds(pl.multiple_of(c * kprep, kprep), kprep)
            kn = _rms(kv_ref[rows, 0:MLA_NOPE_DIM].astype(F32), kng_ref[...], MLA_NOPE_DIM)
            kn_ref[rows, 0:MLA_NOPE_DIM] = kn.astype(BF16)
            kr = _rms(kr_ref[rows, :].astype(F32), krg_ref[...], MLA_ROPE_DIM)
            kn_ref[rows, MLA_NOPE_DIM:MLA_QPAD] = _rope(kr, ck_ref[rows, :], sk_ref[rows, :]).astype(BF16)
            return 0
        lax.fori_loop(0, n_keys // kprep, prep, 0)
        _store_vt(vt_ref, lambda r0, nr: kv_ref[r0:r0 + nr, MLA_NOPE_DIM:MLA_QPAD], n_keys, kprep)

    def prep_q(q_ref, cos_ref, sin_ref):
        q = q_ref[...].astype(F32)
        g = qg_ref[...]
        inv = lax.rsqrt(jnp.sum(q * q, axis=-1, keepdims=True) * (1.0 / MLA_QK_DIM) + NORM_EPS)
        q_nope = q[:, 0:MLA_NOPE_DIM] * inv * g[:, 0:MLA_NOPE_DIM]
        q_rot = _rope(q[:, MLA_NOPE_DIM:MLA_QPAD] * inv * g[:, MLA_NOPE_DIM:MLA_QPAD], cos_ref[...], sin_ref[...])
        return jnp.concatenate([(q_nope * scale).T, (q_rot * scale).T], axis=0).astype(BF16)

    def store_out(out_t):
        o_ref[...] = out_t.T.astype(o_ref.dtype)

    _pipelined_attention(pl.program_id(2), n_q, prep_kv, functools.partial(prep_q, q0_ref, c0_ref, s0_ref),
                         functools.partial(prep_q, qn_ref, cn_ref, sn_ref), store_out, kn_ref, vt_ref, qt_ref,
                         acc_ref, l_ref, n_keys, tk, bounded)


def _mla_attention(q, kv, down, cos, sin, q_gain, kn_gain, kr_gain, *, bounded):
    b, t, _ = q.shape
    tq = min(1024, t)
    tk = min(512, t)
    kprep = min(512, t)
    kr_col = (MLA_Q_RANK + MLA_KV_RANK) // LANES
    n_q = t // tq
    nxt = lambda s: jnp.minimum(s + 1, n_q - 1)
    prv = lambda s: jnp.maximum(s - 1, 0)
    tab_0 = pl.BlockSpec((tq, LANES), lambda bi, h, s: (0, 0))
    tab_n = pl.BlockSpec((tq, LANES), lambda bi, h, s: (nxt(s), 0))
    tab_k = pl.BlockSpec((t, LANES), lambda bi, h, s: (0, 0))
    return pl.pallas_call(
        functools.partial(_mla_attn_kernel, n_q=n_q, tk=tk, kprep=kprep, bounded=bounded),
        grid=(b, MLA_HEADS, n_q + 1),
        in_specs=[
            pl.BlockSpec((None, tq, MLA_QPAD), lambda bi, h, s: (bi, 0, h)),
            pl.BlockSpec((None, tq, MLA_QPAD), lambda bi, h, s: (bi, nxt(s), h)),
            pl.BlockSpec((None, t, MLA_QPAD), lambda bi, h, s: (bi, 0, h)),
            pl.BlockSpec((None, t, LANES), lambda bi, h, s: (bi, 0, kr_col)),
            tab_0, tab_0, tab_n, tab_n, tab_k, tab_k,
            pl.BlockSpec((1, MLA_QPAD), lambda bi, h, s: (0, 0)),
            pl.BlockSpec((1, LANES), lambda bi, h, s: (0, 0)),
            pl.BlockSpec((1, LANES), lambda bi, h, s: (0, 0)),
        ],
        out_specs=pl.BlockSpec((None, tq, MLA_V_DIM), lambda bi, h, s: (bi, prv(s), h)),
        out_shape=jax.ShapeDtypeStruct((b, t, MLA_HEADS * MLA_V_DIM), BF16),
        scratch_shapes=[pltpu.VMEM((t, MLA_QPAD), BF16), pltpu.VMEM((MLA_V_DIM, t), BF16),
                        pltpu.VMEM((2, MLA_QPAD, tq), BF16), pltpu.VMEM((MLA_V_DIM, tq), F32),
                        pltpu.VMEM((8, tq), F32)],
        compiler_params=_cparams("parallel", "parallel", "arbitrary"),
    )(q, q, kv, down, cos, sin, cos, sin, cos, sin, q_gain.reshape(1, MLA_QPAD), kn_gain.reshape(1, LANES),
      kr_gain.reshape(1, LANES))


def _split3(x):
    hi = x.astype(BF16)
    r1 = x - hi.astype(F32)
    mid = r1.astype(BF16)
    lo = (r1 - mid.astype(F32)).astype(BF16)
    return hi, mid, lo


def _gla_kernel(*refs, reverse, n_chunks):
    if reverse:
        (q_ref, k_ref, v_ref, low_ref, wg_ref, bg_ref, of_ref, r_ref, on_ref, o_ref, s_ref) = refs
    else:
        (q_ref, k_ref, v_ref, low_ref, wg_ref, bg_ref, o_ref, s_ref) = refs
    L = GLA_BLOCK

    @pl.when(pl.program_id(2) == 0)
    def _():
        s_ref[...] = jnp.zeros_like(s_ref)

    row = lax.broadcasted_iota(jnp.int32, (L, L), 0)
    col = lax.broadcasted_iota(jnp.int32, (L, L), 1)
    keep = (col >= row) if reverse else (col <= row)
    tri = jnp.where(keep, 1.0, 0.0).astype(BF16)
    tn_dims = (((0,), (0,)), ((), ()))
    nt_dims = (((1,), (1,)), ((), ()))
    mid = L // 2 if reverse else L // 2 - 1
    edge = 0 if reverse else L - 1

    order = list(range(n_chunks - 1, -1, -1) if reverse else range(n_chunks))
    rows = {c: slice(c * L, (c + 1) * L) for c in order}

    z = jnp.dot(low_ref[...].astype(BF16), wg_ref[...], preferred_element_type=F32) + bg_ref[...]
    cum = {}
    for c in order:
        zc = z[rows[c], :]
        la = (jnp.minimum(zc, 0.0) - jnp.log1p(jnp.exp(-jnp.abs(zc)))) * (1.0 / GLA_GATE_TAU)
        cum[c] = sum(jnp.dot(tri, p, preferred_element_type=F32) for p in _split3(la))
    qd, qs, kd, ke, de = {}, {}, {}, {}, {}
    for c in order:
        ref_row = cum[c][mid:mid + 1, :]
        tot_row = cum[c][edge:edge + 1, :]
        q = q_ref[rows[c], :].astype(F32) * (GLA_DK_HEAD ** -0.5)
        k = k_ref[rows[c], :].astype(F32)
        q_mid = q * jnp.exp(cum[c] - ref_row)
        qd[c] = q_mid.astype(BF16)
        qs[c] = (q_mid * jnp.exp(ref_row)).astype(BF16)
        kd[c] = (k * jnp.exp(ref_row - cum[c])).astype(BF16)
        ke[c] = (k * jnp.exp(tot_row - cum[c])).astype(BF16)
        col_de = jnp.broadcast_to(jnp.exp(tot_row), (LANES, GLA_DK_HEAD)).T
        de[c] = jnp.concatenate([col_de] * (GLA_DV_HEAD // LANES), axis=1)
    a = {c: jnp.where(keep, lax.dot_general(qd[c], kd[c], nt_dims, preferred_element_type=F32), 0.0).astype(BF16)
         for c in order}
    o_intra = {c: jnp.dot(a[c], v_ref[rows[c], :], preferred_element_type=F32) for c in order}
    kv = {c: lax.dot_general(ke[c], v_ref[rows[c], :], tn_dims, preferred_element_type=F32) for c in order}

    s = s_ref[...]
    for c in order:
        o = o_intra[c] + jnp.dot(qs[c], s.astype(BF16), preferred_element_type=F32)
        s = de[c] * s + kv[c]
        if reverse:
            o = o + of_ref[rows[c], :]
            o = _rms(o, on_ref[...], GLA_DV_HEAD)
            r = r_ref[rows[c], :].astype(F32)
            o = o * (r * (1.0 / (1.0 + jnp.exp(-r))))
        o_ref[rows[c], :] = o.astype(o_ref.dtype)
    s_ref[...] = s


def _gla_direction(proj, low, wg, bg, o_fwd, o_norm, reverse):
    b, t, _ = proj.shape
    tb = min(1024, t)
    assert t % tb == 0 and tb % GLA_BLOCK == 0
    nblk = t // tb
    blk = (lambda i: nblk - 1 - i) if reverse else (lambda i: i)
    k_col0 = GLA_DK // GLA_DK_HEAD
    v_col0 = 2 * GLA_DK // GLA_DV_HEAD
    r_col0 = (2 * GLA_DK + GLA_DV) // GLA_DV_HEAD
    in_specs = [
        pl.BlockSpec((None, tb, GLA_DK_HEAD), lambda bi, h, i: (bi, blk(i), h)),
        pl.BlockSpec((None, tb, GLA_DK_HEAD), lambda bi, h, i: (bi, blk(i), k_col0 + h)),
        pl.BlockSpec((None, tb, GLA_DV_HEAD), lambda bi, h, i: (bi, blk(i), v_col0 + h)),
        pl.BlockSpec((None, tb, LANES), lambda bi, h, i: (bi, blk(i), 0)),
        pl.BlockSpec((LANES, GLA_DK_HEAD), lambda bi, h, i: (0, h)),
        pl.BlockSpec((1, GLA_DK_HEAD), lambda bi, h, i: (0, h)),
    ]
    args = [proj, proj, proj, low, wg, bg]
    if reverse:
        in_specs += [
            pl.BlockSpec((None, tb, GLA_DV_HEAD), lambda bi, h, i: (bi, blk(i), h)),
            pl.BlockSpec((None, tb, GLA_DV_HEAD), lambda bi, h, i: (bi, blk(i), r_col0 + h)),
            pl.BlockSpec((1, GLA_DV_HEAD), lambda bi, h, i: (0, 0)),
        ]
        args += [o_fwd, proj, o_norm.reshape(1, GLA_DV_HEAD)]
    return pl.pallas_call(
        functools.partial(_gla_kernel, reverse=reverse, n_chunks=tb // GLA_BLOCK),
        grid=(b, GLA_HEADS, nblk),
        in_specs=in_specs,
        out_specs=pl.BlockSpec((None, tb, GLA_DV_HEAD), lambda bi, h, i: (bi, blk(i), h)),
        out_shape=jax.ShapeDtypeStruct((b, t, GLA_DV), BF16 if reverse else F32),
        scratch_shapes=[pltpu.VMEM((GLA_DK_HEAD, GLA_DV_HEAD), F32)],
        compiler_params=_cparams("parallel", "parallel", "arbitrary"),
    )(*args)


def _prep_gqa(w_qkv, q_norm, k_norm):
    n_rot = GQA_Q_HEADS + GQA_KV_HEADS
    cols = (np.arange(n_rot)[:, None] * GQA_HEAD_DIM + _HALF_SPLIT_128[None, :]).reshape(-1)
    cols = np.concatenate([cols, np.arange(n_rot * GQA_HEAD_DIM, w_qkv.shape[1])])
    return w_qkv[:, cols].astype(BF16), q_norm[_HALF_SPLIT_128], k_norm[_HALF_SPLIT_128]


def _prep_mla(w_down, w_uq, q_norm, k_rope_norm):
    qidx, qmsk = _mla_q_layout()
    cols = (np.arange(MLA_HEADS)[:, None] * MLA_QK_DIM + qidx[None, :]).reshape(-1)
    w_uq_p = (w_uq[:, cols] * jnp.asarray(np.tile(qmsk, MLA_HEADS))).astype(BF16)
    q_gain = q_norm[qidx] * jnp.asarray(qmsk)
    ridx, rmsk = _mla_rot_layout(0)
    kr0 = MLA_Q_RANK + MLA_KV_RANK
    w_kr = w_down[:, kr0 + ridx] * jnp.asarray(rmsk)
    w_down_p = jnp.concatenate([w_down[:, :kr0], w_kr], axis=1).astype(BF16)
    kr_gain = k_rope_norm[ridx] * jnp.asarray(rmsk)
    return w_down_p, w_uq_p, q_gain, kr_gain


def _prep_gla_gates(w_g1, w_g2):
    r = GLA_GATE_RANK
    d = w_g1.shape[1]
    w1 = jnp.zeros((d, LANES), F32).at[:, 0:r].set(w_g1[0]).at[:, r:2 * r].set(w_g1[1])
    w2 = jnp.zeros((2, LANES, GLA_DK), F32).at[0, 0:r].set(w_g2[0]).at[1, r:2 * r].set(w_g2[1])
    return w1.astype(BF16), w2.astype(BF16)


def _trunk(x3, mod, p):
    b, t, d = x3.shape
    n = b * t
    x = x3.reshape(n, d)
    for i in range(DEPTH):
        m = mod[i].reshape(b, 6, 1, d)
        sh1, sc1, g1, sh2, sc2, g2 = (m[:, s] for s in range(6))
        kind, j = i % N_MIXERS, i // N_MIXERS
        if kind == 0:
            qkv = _norm_mod_matmul(x, p['norm1_g'][i], sc1, sh1, p['gqa_w_qkv'][j], t, BF16)
            cos, sin = _gqa_rope_tables(t)
            a = _dispatch_on_score_bound(_gqa_score_bound(p['gqa_q_norm'][j], p['gqa_k_norm'][j]), _gqa_attention,
                                         qkv.reshape(b, t, -1), cos, sin, p['gqa_q_norm'][j], p['gqa_k_norm'][j])
            x = _oproj_residual(a.reshape(n, -1), p['gqa_w_o'][j], x, g1, t)
        elif kind == 1:
            proj = _norm_mod_matmul(x, p['norm1_g'][i], sc1, sh1, p['gla_w_in'][j], t, BF16, tn=2048)
            low = _norm_mod_matmul(x, p['norm1_g'][i], sc1, sh1, p['gla_w_g1'][j], t, F32)
            proj3, low3 = proj.reshape(b, t, -1), low.reshape(b, t, -1)
            bg = p['gla_b_g'][j]
            o_f = _gla_direction(proj3, low3, p['gla_w_g2'][j][0], bg[0:1], None, None, reverse=False)
            a = _gla_direction(proj3, low3, p['gla_w_g2'][j][1], bg[1:2], o_f, p['gla_o_norm'][j], reverse=True)
            x = _oproj_residual(a.reshape(n, -1), p['gla_w_o'][j], x, g1, t)
        else:
            down = _norm_mod_matmul(x, p['norm1_g'][i], sc1, sh1, p['mla_w_down'][j], t, F32)
            q = _norm_matmul(down, 0, MLA_Q_RANK, p['mla_q_a_norm'][j], p['mla_w_uq'][j], t, BF16)
            kv = _norm_matmul(down, MLA_Q_RANK // MLA_KV_RANK, MLA_KV_RANK, p['mla_kv_a_norm'][j],
                              p['mla_w_ukv'][j], t, BF16)
            cos, sin = _mla_rope_tables(t)
            gains = (p['mla_q_gain'][j], p['mla_k_nope_norm'][j], p['mla_kr_gain'][j])
            a = _dispatch_on_score_bound(_mla_score_bound(*gains), _mla_attention, q.reshape(b, t, -1),
                                         kv.reshape(b, t, -1), down.reshape(b, t, -1), cos, sin, *gains)
            x = _oproj_residual(a.reshape(n, -1), p['mla_w_o'][j], x, g1, t)
        x = _mlp_residual(x, p['norm2_g'][i], sc2, sh2, g2, p['mlp_w1'][i], p['mlp_w2'][i], t)
    return x.reshape(b, t, d)


def kernel(x_prompt, x_sample, c_prompt, c_sample, norm1_g, norm2_g, ada_w, ada_b, gqa_w_qkv, gqa_q_norm, gqa_k_norm, gqa_w_o, gla_w_in, gla_w_g1, gla_w_g2, gla_b_g, gla_o_norm, gla_w_o, mla_w_down, mla_q_a_norm, mla_kv_a_norm, mla_w_uq, mla_w_ukv, mla_q_norm, mla_k_nope_norm, mla_k_rope_norm, mla_w_o, mlp_w1, mlp_w2):
    nA, nB, nC = gqa_w_qkv.shape[0], gla_w_in.shape[0], mla_w_down.shape[0]
    gqa = [_prep_gqa(gqa_w_qkv[j], gqa_q_norm[j], gqa_k_norm[j]) for j in range(nA)]
    mla = [_prep_mla(mla_w_down[j], mla_w_uq[j], mla_q_norm[j], mla_k_rope_norm[j]) for j in range(nC)]
    gates = [_prep_gla_gates(gla_w_g1[j], gla_w_g2[j]) for j in range(nB)]
    p = dict(
        norm1_g=norm1_g, norm2_g=norm2_g,
        gqa_w_qkv=[g[0] for g in gqa], gqa_q_norm=[g[1] for g in gqa], gqa_k_norm=[g[2] for g in gqa],
        gqa_w_o=gqa_w_o.astype(BF16),
        gla_w_in=gla_w_in.astype(BF16), gla_w_g1=[g[0] for g in gates], gla_w_g2=[g[1] for g in gates],
        gla_b_g=gla_b_g, gla_o_norm=gla_o_norm, gla_w_o=gla_w_o.astype(BF16),
        mla_w_down=[m[0] for m in mla], mla_w_uq=[m[1] for m in mla], mla_q_gain=[m[2] for m in mla],
        mla_kr_gain=[m[3] for m in mla], mla_q_a_norm=mla_q_a_norm, mla_kv_a_norm=mla_kv_a_norm,
        mla_w_ukv=mla_w_ukv.astype(BF16), mla_k_nope_norm=mla_k_nope_norm, mla_w_o=mla_w_o.astype(BF16),
        mlp_w1=mlp_w1.astype(BF16), mlp_w2=mlp_w2.astype(BF16),
    )
    bp, bs = c_prompt.shape[0], c_sample.shape[0]
    nb_pad = -(-(bp + bs) // 16) * 16
    c_all = jnp.concatenate([c_prompt, c_sample, jnp.zeros((nb_pad - bp - bs, c_prompt.shape[1]), F32)], axis=0)
    mod = _ada_modulation(c_all, ada_w, ada_b)
    y_prompt = _trunk(x_prompt, mod[:, :bp], p)
    y_sample = _trunk(x_sample, mod[:, bp:bp + bs], p)
    return (y_prompt, y_sample)
```

```python
import functools
import math

import numpy as np
import jax
import jax.numpy as jnp
from jax import lax
from jax.experimental import pallas as pl
from jax.experimental.pallas import tpu as pltpu

F32 = jnp.float32
BF16 = jnp.bfloat16

D_MODEL = 2048
DEPTH = 4
GRID_W = 64
ROPE_THETA = 10000.0
NORM_EPS = 1e-6
N_MIXERS = 3

GQA_HEAD_DIM = 128
GQA_Q_HEADS = 16
GQA_KV_HEADS = 4
GQA_GROUP = 4

GLA_HEADS = 4
GLA_DK = 1024
GLA_DV = 2048
GLA_DK_HEAD = 256
GLA_DV_HEAD = 512
GLA_GATE_RANK = 16
GLA_GATE_TAU = 16.0
GLA_BLOCK = 128

MLA_HEADS = 16
MLA_Q_RANK = 512
MLA_KV_RANK = 256
MLA_NOPE_DIM = 128
MLA_ROPE_DIM = 64
MLA_QK_DIM = 192
MLA_V_DIM = 128
MLA_QPAD = 256
MLA_DOWN_PAD = 896

D_FF = 4 * D_MODEL

LANES = 128
VMEM_LIMIT_BYTES = 56 * 1024 * 1024


def _tiles(n_rows, seq):
    tm = min(512, seq)
    assert seq % tm == 0 and n_rows % tm == 0
    return tm


def _cparams(*sem):
    return pltpu.CompilerParams(dimension_semantics=sem, vmem_limit_bytes=VMEM_LIMIT_BYTES)


def _ada_kernel(c_ref, w_ref, b_ref, o_ref):
    c = c_ref[...]
    cond = (c * (1.0 / (1.0 + jnp.exp(-c)))).astype(BF16)
    o_ref[...] = jnp.dot(cond, w_ref[...].astype(BF16), preferred_element_type=F32) + b_ref[...]


def _ada_modulation(c_all, ada_w, ada_b):
    nb, d = c_all.shape
    depth, _, n6 = ada_w.shape
    tn = 1024
    return pl.pallas_call(
        _ada_kernel,
        grid=(depth, n6 // tn),
        in_specs=[
            pl.BlockSpec((nb, d), lambda i, j: (0, 0)),
            pl.BlockSpec((None, d, tn), lambda i, j: (i, 0, j)),
            pl.BlockSpec((None, 1, tn), lambda i, j: (i, 0, j)),
        ],
        out_specs=pl.BlockSpec((None, nb, tn), lambda i, j: (i, 0, j)),
        out_shape=jax.ShapeDtypeStruct((depth, nb, n6), F32),
        compiler_params=_cparams("parallel", "parallel"),
    )(c_all, ada_w, ada_b.reshape(depth, 1, n6))


def _rms(x, g, n):
    ms = jnp.sum(x * x, axis=-1, keepdims=True) * (1.0 / n)
    return x * lax.rsqrt(ms + NORM_EPS) * g


NORM_ROWS = 16


def _store_norm(h_ref, x_ref, gain, shift):
    d = x_ref.shape[-1]
    blocks = [slice(r, r + NORM_ROWS) for r in range(0, x_ref.shape[0], NORM_ROWS)]
    inv = []
    for rows in blocks:
        x = x_ref[rows, :].astype(F32)
        inv.append(lax.rsqrt(jnp.sum(x * x, axis=-1, keepdims=True) * (1.0 / d) + NORM_EPS))
    for rows, inv_r in zip(blocks, inv):
        y = x_ref[rows, :].astype(F32) * inv_r * gain
        h_ref[rows, :] = (y if shift is None else y + shift).astype(BF16)


def _norm_mod_matmul_kernel(x_ref, g_ref, sc_ref, sh_ref, w_ref, o_ref, h_ref):
    @pl.when(pl.program_id(1) == 0)
    def _():
        _store_norm(h_ref, x_ref, g_ref[...] * (1.0 + sc_ref[...]), sh_ref[...])

    o_ref[...] = jnp.dot(h_ref[...], w_ref[...], preferred_element_type=F32).astype(o_ref.dtype)


def _norm_matmul_kernel(x_ref, g_ref, w_ref, o_ref, h_ref):
    @pl.when(pl.program_id(1) == 0)
    def _():
        _store_norm(h_ref, x_ref, g_ref[...], None)

    o_ref[...] = jnp.dot(h_ref[...], w_ref[...], preferred_element_type=F32).astype(o_ref.dtype)


def _norm_mod_matmul(x, g, sc, sh, w, seq, out_dtype, tn=None):
    n, d = x.shape
    nout = w.shape[1]
    tm = _tiles(n, seq)
    tn = nout if tn is None else tn
    tpb = seq // tm
    return pl.pallas_call(
        _norm_mod_matmul_kernel,
        grid=(n // tm, nout // tn),
        in_specs=[
            pl.BlockSpec((tm, d), lambda i, j: (i, 0)),
            pl.BlockSpec((1, d), lambda i, j: (0, 0)),
            pl.BlockSpec((None, 1, d), lambda i, j: (i // tpb, 0, 0)),
            pl.BlockSpec((None, 1, d), lambda i, j: (i // tpb, 0, 0)),
            pl.BlockSpec((d, tn), lambda i, j: (0, j)),
        ],
        out_specs=pl.BlockSpec((tm, tn), lambda i, j: (i, j)),
        out_shape=jax.ShapeDtypeStruct((n, nout), out_dtype),
        scratch_shapes=[pltpu.VMEM((tm, d), BF16)],
        compiler_params=_cparams("parallel", "arbitrary"),
    )(x, g.reshape(1, d), sc, sh, w)


def _norm_matmul(x, col_block, width, g, w, seq, out_dtype):
    n = x.shape[0]
    nout = w.shape[1]
    tm = _tiles(n, seq)
    return pl.pallas_call(
        _norm_matmul_kernel,
        grid=(n // tm, 1),
        in_specs=[
            pl.BlockSpec((tm, width), lambda i, j: (i, col_block)),
            pl.BlockSpec((1, width), lambda i, j: (0, 0)),
            pl.BlockSpec((width, nout), lambda i, j: (0, 0)),
        ],
        out_specs=pl.BlockSpec((tm, nout), lambda i, j: (i, 0)),
        out_shape=jax.ShapeDtypeStruct((n, nout), out_dtype),
        scratch_shapes=[pltpu.VMEM((tm, width), BF16)],
        compiler_params=_cparams("parallel", "arbitrary"),
    )(x, g.reshape(1, width), w)


def _oproj_kernel(a_ref, w_ref, x_ref, gate_ref, o_ref):
    y = jnp.dot(a_ref[...], w_ref[...], preferred_element_type=F32)
    o_ref[...] = x_ref[...] + gate_ref[...] * y


def _oproj_residual(a, w, x, gate, seq):
    n, k = a.shape
    d = w.shape[1]
    tm = _tiles(n, seq)
    tpb = seq // tm
    return pl.pallas_call(
        _oproj_kernel,
        grid=(n // tm,),
        in_specs=[
            pl.BlockSpec((tm, k), lambda i: (i, 0)),
            pl.BlockSpec((k, d), lambda i: (0, 0)),
            pl.BlockSpec((tm, d), lambda i: (i, 0)),
            pl.BlockSpec((None, 1, d), lambda i: (i // tpb, 0, 0)),
        ],
        out_specs=pl.BlockSpec((tm, d), lambda i: (i, 0)),
        out_shape=jax.ShapeDtypeStruct((n, d), F32),
        compiler_params=_cparams("parallel"),
    )(a, w, x, gate)


def _mlp_kernel(x_ref, xn_ref, g_ref, sc_ref, sh_ref, scn_ref, shn_ref, gate_ref, w1_ref, w2_ref, o_ref, h_ref):
    i = pl.program_id(0)
    j = pl.program_id(1)
    last = pl.num_programs(1) - 1
    slot = lax.rem(i, 2)

    def hidden(s):
        a = jnp.maximum(jnp.dot(h_ref[s], w1_ref[...], preferred_element_type=F32), 0.0)
        return jnp.dot((a * a).astype(BF16), w2_ref[...], preferred_element_type=F32)

    @pl.when((i == 0) & (j == 0))
    def _():
        _store_norm(h_ref.at[0], x_ref, g_ref[...] * (1.0 + sc_ref[...]), sh_ref[...])

    for parity in (0, 1):
        @pl.when((j == 0) & (slot == parity))
        def _():
            o_ref[...] = hidden(parity)
            _store_norm(h_ref.at[1 - parity], xn_ref, g_ref[...] * (1.0 + scn_ref[...]), shn_ref[...])

    @pl.when((j > 0) & (j < last))
    def _():
        o_ref[...] += hidden(slot)

    @pl.when(j == last)
    def _():
        o_ref[...] = x_ref[...] + gate_ref[...] * (o_ref[...] + hidden(slot))


def _mlp_residual(x, g, sc, sh, gate, w1, w2, seq):
    n, d = x.shape
    dff = w1.shape[1]
    tm = _tiles(n, seq)
    tf = min(1024, dff)
    assert dff % tf == 0 and dff // tf >= 2
    tpb = seq // tm
    n_i = n // tm
    nxt = lambda i: jnp.minimum(i + 1, n_i - 1)
    mod_spec = pl.BlockSpec((None, 1, d), lambda i, j: (i // tpb, 0, 0))
    mod_next = pl.BlockSpec((None, 1, d), lambda i, j: (nxt(i) // tpb, 0, 0))
    return pl.pallas_call(
        _mlp_kernel,
        grid=(n_i, dff // tf),
        in_specs=[
            pl.BlockSpec((tm, d), lambda i, j: (i, 0)),
            pl.BlockSpec((tm, d), lambda i, j: (nxt(i), 0)),
            pl.BlockSpec((1, d), lambda i, j: (0, 0)),
            mod_spec, mod_spec, mod_next, mod_next, mod_spec,
            pl.BlockSpec((d, tf), lambda i, j: (0, j)),
            pl.BlockSpec((tf, d), lambda i, j: (j, 0)),
        ],
        out_specs=pl.BlockSpec((tm, d), lambda i, j: (i, 0)),
        out_shape=jax.ShapeDtypeStruct((n, d), F32),
        scratch_shapes=[pltpu.VMEM((2, tm, d), BF16)],
        compiler_params=_cparams("arbitrary", "arbitrary"),
    )(x, x, g.reshape(1, d), sc, sh, sc, sh, gate, w1, w2)


def _axial_angles(n_tok, rot_dim):
    rows = n_tok // GRID_W
    row = jnp.repeat(jnp.arange(rows, dtype=F32), GRID_W)
    col = jnp.tile(jnp.arange(GRID_W, dtype=F32), rows)
    n_freq = rot_dim // 4
    inv = ROPE_THETA ** (-jnp.arange(n_freq, dtype=F32) / n_freq)
    ang = jnp.concatenate([row[:, None] * inv, col[:, None] * inv], axis=-1)
    return jnp.cos(ang), jnp.sin(ang)


def _gqa_rope_tables(n_tok):
    cos, sin = _axial_angles(n_tok, GQA_HEAD_DIM)
    return jnp.concatenate([cos, cos], -1), jnp.concatenate([-sin, sin], -1)


def _mla_rope_tables(n_tok):
    cos, sin = _axial_angles(n_tok, MLA_ROPE_DIM)
    z = jnp.zeros_like(cos)
    return jnp.concatenate([cos, z, cos, z], -1), jnp.concatenate([-sin, z, sin, z], -1)


_HALF_SPLIT_128 = np.concatenate([np.arange(0, 128, 2), np.arange(1, 128, 2)])


def _mla_rot_layout(base):
    idx = np.zeros(128, np.int32)
    msk = np.zeros(128, np.float32)
    idx[0:32] = base + np.arange(0, 64, 2)
    idx[64:96] = base + np.arange(1, 64, 2)
    msk[0:32] = 1.0
    msk[64:96] = 1.0
    return idx, msk


def _mla_q_layout():
    ridx, rmsk = _mla_rot_layout(MLA_NOPE_DIM)
    idx = np.concatenate([np.arange(MLA_NOPE_DIM), ridx])
    msk = np.concatenate([np.ones(MLA_NOPE_DIM, np.float32), rmsk])
    return idx, msk


def _rope(x, cos, sin):
    return x * cos + pltpu.roll(x, 64, 1) * sin


LOG2E = math.log2(math.e)


MAX_UNSHIFTED_SCORE = 40.0


def _flash_t(q_t, kn_ref, vt_ref, n_keys, tk, bounded):
    m_cols = q_t.shape[1]
    m = jnp.full((1, m_cols), -jnp.inf, F32)
    l8 = jnp.zeros((8, m_cols), F32)
    acc = jnp.zeros((vt_ref.shape[0], m_cols), F32)
    for c in range(n_keys // tk):
        s = jnp.dot(kn_ref[c * tk:(c + 1) * tk, :], q_t, preferred_element_type=F32)
        if bounded:
            p = jnp.exp2(s)
            l8 = l8 + jnp.sum(p.reshape(tk // 8, 8, m_cols), axis=0)
            acc = acc + jnp.dot(vt_ref[:, c * tk:(c + 1) * tk], p.astype(BF16), preferred_element_type=F32)
        else:
            m_new = jnp.maximum(m, jnp.max(s, axis=0, keepdims=True))
            alpha = jnp.exp2(m - m_new)
            p = jnp.exp2(s - m_new)
            l8 = alpha * l8 + jnp.sum(p.reshape(tk // 8, 8, m_cols), axis=0)
            acc = alpha * acc + jnp.dot(vt_ref[:, c * tk:(c + 1) * tk], p.astype(BF16),
                                        preferred_element_type=F32)
            m = m_new
    return acc * (1.0 / jnp.sum(l8, axis=0, keepdims=True))


def _store_vt(vt_ref, v_load, n_keys, blk):
    for c in range(n_keys // blk):
        vt_ref[:, c * blk:(c + 1) * blk] = v_load(c * blk, blk).astype(F32).T.astype(BF16)


def _gqa_attn_kernel(q_ref, k_ref, v_ref, cq_ref, sq_ref, ck_ref, sk_ref, qg_ref, kg_ref, o_ref, kn_ref, vt_ref,
                     *, tk, kprep, bounded):
    n_keys = k_ref.shape[0]
    tq = q_ref.shape[0]

    @pl.when(pl.program_id(2) == 0)
    def _():
        def prep(c, _):
            rows = pl.ds(pl.multiple_of(c * kprep, kprep), kprep)
            k = _rms(k_ref[rows, :].astype(F32), kg_ref[...], GQA_HEAD_DIM)
            kn_ref[rows, :] = _rope(k, ck_ref[rows, :], sk_ref[rows, :]).astype(BF16)
            return 0
        lax.fori_loop(0, n_keys // kprep, prep, 0)
        _store_vt(vt_ref, lambda r0, nr: v_ref[r0:r0 + nr, :], n_keys, kprep)

    scale = GQA_HEAD_DIM ** -0.5 * LOG2E
    qs = []
    for h in range(GQA_GROUP):
        q = _rms(q_ref[:, h * GQA_HEAD_DIM:(h + 1) * GQA_HEAD_DIM].astype(F32), qg_ref[...], GQA_HEAD_DIM)
        qs.append((_rope(q, cq_ref[...], sq_ref[...]) * scale).T.astype(BF16))
    q_t = jnp.concatenate(qs, axis=1)
    out_t = _flash_t(q_t, kn_ref, vt_ref, n_keys, tk, bounded)
    for h in range(GQA_GROUP):
        o_ref[:, h * GQA_HEAD_DIM:(h + 1) * GQA_HEAD_DIM] = out_t[:, h * tq:(h + 1) * tq].T.astype(o_ref.dtype)


def _dispatch_on_score_bound(bound, attention, *args):
    return lax.cond(bound <= MAX_UNSHIFTED_SCORE,
                    functools.partial(attention, bounded=True),
                    functools.partial(attention, bounded=False), *args)


def _gqa_score_bound(q_gain, k_gain):
    return (GQA_HEAD_DIM * jnp.max(jnp.abs(q_gain)) * jnp.max(jnp.abs(k_gain))) * (GQA_HEAD_DIM ** -0.5 * LOG2E)


def _mla_score_bound(q_gain, kn_gain, kr_gain):
    q_norm = math.sqrt(MLA_QK_DIM) * jnp.max(jnp.abs(q_gain))
    k_norm = jnp.sqrt(MLA_NOPE_DIM * jnp.max(kn_gain * kn_gain) + MLA_ROPE_DIM * jnp.max(kr_gain * kr_gain))
    return q_norm * k_norm * (MLA_QK_DIM ** -0.5 * LOG2E)


def _gqa_attention(qkv, cos, sin, q_gain, k_gain, *, bounded):
    b, t, _ = qkv.shape
    tq = min(512 if t <= 2048 else 256, t)
    tk = min(512, t)
    kprep = min(512, t)
    gw = GQA_GROUP * GQA_HEAD_DIM
    k_col0 = GQA_Q_HEADS
    v_col0 = GQA_Q_HEADS + GQA_KV_HEADS
    tab_q = pl.BlockSpec((tq, LANES), lambda bi, g, qi: (qi, 0))
    tab_k = pl.BlockSpec((t, LANES), lambda bi, g, qi: (0, 0), pipeline_mode=pl.Buffered(1))
    gain = pl.BlockSpec((1, LANES), lambda bi, g, qi: (0, 0))
    return pl.pallas_call(
        functools.partial(_gqa_attn_kernel, tk=tk, kprep=kprep, bounded=bounded),
        grid=(b, GQA_KV_HEADS, t // tq),
        in_specs=[
            pl.BlockSpec((None, tq, gw), lambda bi, g, qi: (bi, qi, g)),
            pl.BlockSpec((None, t, GQA_HEAD_DIM), lambda bi, g, qi: (bi, 0, k_col0 + g)),
            pl.BlockSpec((None, t, GQA_HEAD_DIM), lambda bi, g, qi: (bi, 0, v_col0 + g)),
            tab_q, tab_q, tab_k, tab_k, gain, gain,
        ],
        out_specs=pl.BlockSpec((None, tq, gw), lambda bi, g, qi: (bi, qi, g)),
        out_shape=jax.ShapeDtypeStruct((b, t, GQA_Q_HEADS * GQA_HEAD_DIM), BF16),
        scratch_shapes=[pltpu.VMEM((t, GQA_HEAD_DIM), BF16), pltpu.VMEM((GQA_HEAD_DIM, t), BF16)],
        compiler_params=_cparams("parallel", "parallel", "arbitrary"),
    )(qkv, qkv, qkv, cos, sin, cos, sin, q_gain.reshape(1, LANES), k_gain.reshape(1, LANES))


def _mla_attn_kernel(q_ref, kv_ref, kr_ref, cq_ref, sq_ref, ck_ref, sk_ref, qg_ref, kng_ref, krg_ref,
                     o_ref, kn_ref, vt_ref, *, tk, kprep, bounded):
    n_keys = kv_ref.shape[0]

    @pl.when(pl.program_id(2) == 0)
    def _():
        def prep(c, _):
            rows = pl.ds(pl.multiple_of(c * kprep, kprep), kprep)
            kn = _rms(kv_ref[rows, 0:MLA_NOPE_DIM].astype(F32), kng_ref[...], MLA_NOPE_DIM)
            kn_ref[rows, 0:MLA_NOPE_DIM] = kn.astype(BF16)
            kr = _rms(kr_ref[rows, :].astype(F32), krg_ref[...], MLA_ROPE_DIM)
            kn_ref[rows, MLA_NOPE_DIM:MLA_QPAD] = _rope(kr, ck_ref[rows, :], sk_ref[rows, :]).astype(BF16)
            return 0
        lax.fori_loop(0, n_keys // kprep, prep, 0)
        _store_vt(vt_ref, lambda r0, nr: kv_ref[r0:r0 + nr, MLA_NOPE_DIM:MLA_QPAD], n_keys, kprep)

    scale = MLA_QK_DIM ** -0.5 * LOG2E
    q = q_ref[...].astype(F32)
    g = qg_ref[...]
    inv = lax.rsqrt(jnp.sum(q * q, axis=-1, keepdims=True) * (1.0 / MLA_QK_DIM) + NORM_EPS)
    q_nope = q[:, 0:MLA_NOPE_DIM] * inv * g[:, 0:MLA_NOPE_DIM]
    q_rot = _rope(q[:, MLA_NOPE_DIM:MLA_QPAD] * inv * g[:, MLA_NOPE_DIM:MLA_QPAD], cq_ref[...], sq_ref[...])
    q_t = jnp.concatenate([(q_nope * scale).T, (q_rot * scale).T], axis=0).astype(BF16)
    out_t = _flash_t(q_t, kn_ref, vt_ref, n_keys, tk, bounded)
    o_ref[...] = out_t.T.astype(o_ref.dtype)


def _mla_attention(q, kv, down, cos, sin, q_gain, kn_gain, kr_gain, *, bounded):
    b, t, _ = q.shape
    tq = min(2048 if t <= 2048 else 1024, t)
    tk = min(512, t)
    kprep = min(512, t)
    kr_col = (MLA_Q_RANK + MLA_KV_RANK) // LANES
    tab_q = pl.BlockSpec((tq, LANES), lambda bi, h, qi: (qi, 0))
    tab_k = pl.BlockSpec((t, LANES), lambda bi, h, qi: (0, 0), pipeline_mode=pl.Buffered(1))
    return pl.pallas_call(
        functools.partial(_mla_attn_kernel, tk=tk, kprep=kprep, bounded=bounded),
        grid=(b, MLA_HEADS, t // tq),
        in_specs=[
            pl.BlockSpec((None, tq, MLA_QPAD), lambda bi, h, qi: (bi, qi, h)),
            pl.BlockSpec((None, t, MLA_QPAD), lambda bi, h, qi: (bi, 0, h)),
            pl.BlockSpec((None, t, LANES), lambda bi, h, qi: (bi, 0, kr_col)),
            tab_q, tab_q, tab_k, tab_k,
            pl.BlockSpec((1, MLA_QPAD), lambda bi, h, qi: (0, 0)),
            pl.BlockSpec((1, LANES), lambda bi, h, qi: (0, 0)),
            pl.BlockSpec((1, LANES), lambda bi, h, qi: (0, 0)),
        ],
        out_specs=pl.BlockSpec((None, tq, MLA_V_DIM), lambda bi, h, qi: (bi, qi, h)),
        out_shape=jax.ShapeDtypeStruct((b, t, MLA_HEADS * MLA_V_DIM), BF16),
        scratch_shapes=[pltpu.VMEM((t, MLA_QPAD), BF16), pltpu.VMEM((MLA_V_DIM, t), BF16)],
        compiler_params=_cparams("parallel", "parallel", "arbitrary"),
    )(q, kv, down, cos, sin, cos, sin, q_gain.reshape(1, MLA_QPAD), kn_gain.reshape(1, LANES),
      kr_gain.reshape(1, LANES))


def _split3(x):
    hi = x.astype(BF16)
    r1 = x - hi.astype(F32)
    mid = r1.astype(BF16)
    lo = (r1 - mid.astype(F32)).astype(BF16)
    return hi, mid, lo


def _gla_kernel(*refs, reverse, n_chunks):
    if reverse:
        (q_ref, k_ref, v_ref, low_ref, wg_ref, bg_ref, of_ref, r_ref, on_ref, o_ref, s_ref) = refs
    else:
        (q_ref, k_ref, v_ref, low_ref, wg_ref, bg_ref, o_ref, s_ref) = refs
    L = GLA_BLOCK

    @pl.when(pl.program_id(2) == 0)
    def _():
        s_ref[...] = jnp.zeros_like(s_ref)

    row = lax.broadcasted_iota(jnp.int32, (L, L), 0)
    col = lax.broadcasted_iota(jnp.int32, (L, L), 1)
    keep = (col >= row) if reverse else (col <= row)
    tri = jnp.where(keep, 1.0, 0.0).astype(BF16)
    tn_dims = (((0,), (0,)), ((), ()))
    nt_dims = (((1,), (1,)), ((), ()))
    mid = L // 2 if reverse else L // 2 - 1
    edge = 0 if reverse else L - 1

    order = list(range(n_chunks - 1, -1, -1) if reverse else range(n_chunks))
    rows = {c: slice(c * L, (c + 1) * L) for c in order}

    z = jnp.dot(low_ref[...].astype(BF16), wg_ref[...], preferred_element_type=F32) + bg_ref[...]
    cum = {}
    for c in order:
        zc = z[rows[c], :]
        la = (jnp.minimum(zc, 0.0) - jnp.log1p(jnp.exp(-jnp.abs(zc)))) * (1.0 / GLA_GATE_TAU)
        cum[c] = sum(jnp.dot(tri, p, preferred_element_type=F32) for p in _split3(la))
    qd, qs, kd, ke, de = {}, {}, {}, {}, {}
    for c in order:
        ref_row = cum[c][mid:mid + 1, :]
        tot_row = cum[c][edge:edge + 1, :]
        q = q_ref[rows[c], :].astype(F32) * (GLA_DK_HEAD ** -0.5)
        k = k_ref[rows[c], :].astype(F32)
        q_mid = q * jnp.exp(cum[c] - ref_row)
        qd[c] = q_mid.astype(BF16)
        qs[c] = (q_mid * jnp.exp(ref_row)).astype(BF16)
        kd[c] = (k * jnp.exp(ref_row - cum[c])).astype(BF16)
        ke[c] = (k * jnp.exp(tot_row - cum[c])).astype(BF16)
        col_de = jnp.broadcast_to(jnp.exp(tot_row), (LANES, GLA_DK_HEAD)).T
        de[c] = jnp.concatenate([col_de] * (GLA_DV_HEAD // LANES), axis=1)
    a = {c: jnp.where(keep, lax.dot_general(qd[c], kd[c], nt_dims, preferred_element_type=F32), 0.0).astype(BF16)
         for c in order}
    o_intra = {c: jnp.dot(a[c], v_ref[rows[c], :], preferred_element_type=F32) for c in order}
    kv = {c: lax.dot_general(ke[c], v_ref[rows[c], :], tn_dims, preferred_element_type=F32) for c in order}

    s = s_ref[...]
    for c in order:
        o = o_intra[c] + jnp.dot(qs[c], s.astype(BF16), preferred_element_type=F32)
        s = de[c] * s + kv[c]
        if reverse:
            o = o + of_ref[rows[c], :]
            o = _rms(o, on_ref[...], GLA_DV_HEAD)
            r = r_ref[rows[c], :].astype(F32)
            o = o * (r * (1.0 / (1.0 + jnp.exp(-r))))
        o_ref[rows[c], :] = o.astype(o_ref.dtype)
    s_ref[...] = s


def _gla_direction(proj, low, wg, bg, o_fwd, o_norm, reverse):
    b, t, _ = proj.shape
    tb = min(1024, t)
    assert t % tb == 0 and tb % GLA_BLOCK == 0
    nblk = t // tb
    blk = (lambda i: nblk - 1 - i) if reverse else (lambda i: i)
    k_col0 = GLA_DK // GLA_DK_HEAD
    v_col0 = 2 * GLA_DK // GLA_DV_HEAD
    r_col0 = (2 * GLA_DK + GLA_DV) // GLA_DV_HEAD
    in_specs = [
        pl.BlockSpec((None, tb, GLA_DK_HEAD), lambda bi, h, i: (bi, blk(i), h)),
        pl.BlockSpec((None, tb, GLA_DK_HEAD), lambda bi, h, i: (bi, blk(i), k_col0 + h)),
        pl.BlockSpec((None, tb, GLA_DV_HEAD), lambda bi, h, i: (bi, blk(i), v_col0 + h)),
        pl.BlockSpec((None, tb, LANES), lambda bi, h, i: (bi, blk(i), 0)),
        pl.BlockSpec((LANES, GLA_DK_HEAD), lambda bi, h, i: (0, h)),
        pl.BlockSpec((1, GLA_DK_HEAD), lambda bi, h, i: (0, h)),
    ]
    args = [proj, proj, proj, low, wg, bg]
    if reverse:
        in_specs += [
            pl.BlockSpec((None, tb, GLA_DV_HEAD), lambda bi, h, i: (bi, blk(i), h)),
            pl.BlockSpec((None, tb, GLA_DV_HEAD), lambda bi, h, i: (bi, blk(i), r_col0 + h)),
            pl.BlockSpec((1, GLA_DV_HEAD), lambda bi, h, i: (0, 0)),
        ]
        args += [o_fwd, proj, o_norm.reshape(1, GLA_DV_HEAD)]
    return pl.pallas_call(
        functools.partial(_gla_kernel, reverse=reverse, n_chunks=tb // GLA_BLOCK),
        grid=(b, GLA_HEADS, nblk),
        in_specs=in_specs,
        out_specs=pl.BlockSpec((None, tb, GLA_DV_HEAD), lambda bi, h, i: (bi, blk(i), h)),
        out_shape=jax.ShapeDtypeStruct((b, t, GLA_DV), BF16 if reverse else F32),
        scratch_shapes=[pltpu.VMEM((GLA_DK_HEAD, GLA_DV_HEAD), F32)],
        compiler_params=_cparams("parallel", "parallel", "arbitrary"),
    )(*args)


def _prep_gqa(w_qkv, q_norm, k_norm):
    n_rot = GQA_Q_HEADS + GQA_KV_HEADS
    cols = (np.arange(n_rot)[:, None] * GQA_HEAD_DIM + _HALF_SPLIT_128[None, :]).reshape(-1)
    cols = np.concatenate([cols, np.arange(n_rot * GQA_HEAD_DIM, w_qkv.shape[1])])
    return w_qkv[:, cols].astype(BF16), q_norm[_HALF_SPLIT_128], k_norm[_HALF_SPLIT_128]


def _prep_mla(w_down, w_uq, q_norm, k_rope_norm):
    qidx, qmsk = _mla_q_layout()
    cols = (np.arange(MLA_HEADS)[:, None] * MLA_QK_DIM + qidx[None, :]).reshape(-1)
    w_uq_p = (w_uq[:, cols] * jnp.asarray(np.tile(qmsk, MLA_HEADS))).astype(BF16)
    q_gain = q_norm[qidx] * jnp.asarray(qmsk)
    ridx, rmsk = _mla_rot_layout(0)
    kr0 = MLA_Q_RANK + MLA_KV_RANK
    w_kr = w_down[:, kr0 + ridx] * jnp.asarray(rmsk)
    w_down_p = jnp.concatenate([w_down[:, :kr0], w_kr], axis=1).astype(BF16)
    kr_gain = k_rope_norm[ridx] * jnp.asarray(rmsk)
    return w_down_p, w_uq_p, q_gain, kr_gain


def _prep_gla_gates(w_g1, w_g2):
    r = GLA_GATE_RANK
    d = w_g1.shape[1]
    w1 = jnp.zeros((d, LANES), F32).at[:, 0:r].set(w_g1[0]).at[:, r:2 * r].set(w_g1[1])
    w2 = jnp.zeros((2, LANES, GLA_DK), F32).at[0, 0:r].set(w_g2[0]).at[1, r:2 * r].set(w_g2[1])
    return w1.astype(BF16), w2.astype(BF16)


def _trunk(x3, mod, p):
    b, t, d = x3.shape
    n = b * t
    x = x3.reshape(n, d)
    for i in range(DEPTH):
        m = mod[i].reshape(b, 6, 1, d)
        sh1, sc1, g1, sh2, sc2, g2 = (m[:, s] for s in range(6))
        kind, j = i % N_MIXERS, i // N_MIXERS
        if kind == 0:
            qkv = _norm_mod_matmul(x, p['norm1_g'][i], sc1, sh1, p['gqa_w_qkv'][j], t, BF16)
            cos, sin = _gqa_rope_tables(t)
            a = _dispatch_on_score_bound(_gqa_score_bound(p['gqa_q_norm'][j], p['gqa_k_norm'][j]), _gqa_attention,
                                         qkv.reshape(b, t, -1), cos, sin, p['gqa_q_norm'][j], p['gqa_k_norm'][j])
            x = _oproj_residual(a.reshape(n, -1), p['gqa_w_o'][j], x, g1, t)
        elif kind == 1:
            proj = _norm_mod_matmul(x, p['norm1_g'][i], sc1, sh1, p['gla_w_in'][j], t, BF16, tn=2048)
            low = _norm_mod_matmul(x, p['norm1_g'][i], sc1, sh1, p['gla_w_g1'][j], t, F32)
            proj3, low3 = proj.reshape(b, t, -1), low.reshape(b, t, -1)
            bg = p['gla_b_g'][j]
            o_f = _gla_direction(proj3, low3, p['gla_w_g2'][j][0], bg[0:1], None, None, reverse=False)
            a = _gla_direction(proj3, low3, p['gla_w_g2'][j][1], bg[1:2], o_f, p['gla_o_norm'][j], reverse=True)
            x = _oproj_residual(a.reshape(n, -1), p['gla_w_o'][j], x, g1, t)
        else:
            down = _norm_mod_matmul(x, p['norm1_g'][i], sc1, sh1, p['mla_w_down'][j], t, F32)
            q = _norm_matmul(down, 0, MLA_Q_RANK, p['mla_q_a_norm'][j], p['mla_w_uq'][j], t, BF16)
            kv = _norm_matmul(down, MLA_Q_RANK // MLA_KV_RANK, MLA_KV_RANK, p['mla_kv_a_norm'][j],
                              p['mla_w_ukv'][j], t, BF16)
            cos, sin = _mla_rope_tables(t)
            gains = (p['mla_q_gain'][j], p['mla_k_nope_norm'][j], p['mla_kr_gain'][j])
            a = _dispatch_on_score_bound(_mla_score_bound(*gains), _mla_attention, q.reshape(b, t, -1),
                                         kv.reshape(b, t, -1), down.reshape(b, t, -1), cos, sin, *gains)
            x = _oproj_residual(a.reshape(n, -1), p['mla_w_o'][j], x, g1, t)
        x = _mlp_residual(x, p['norm2_g'][i], sc2, sh2, g2, p['mlp_w1'][i], p['mlp_w2'][i], t)
    return x.reshape(b, t, d)


def kernel(x_prompt, x_sample, c_prompt, c_sample, norm1_g, norm2_g, ada_w, ada_b, gqa_w_qkv, gqa_q_norm, gqa_k_norm, gqa_w_o, gla_w_in, gla_w_g1, gla_w_g2, gla_b_g, gla_o_norm, gla_w_o, mla_w_down, mla_q_a_norm, mla_kv_a_norm, mla_w_uq, mla_w_ukv, mla_q_norm, mla_k_nope_norm, mla_k_rope_norm, mla_w_o, mlp_w1, mlp_w2):
    nA, nB, nC = gqa_w_qkv.shape[0], gla_w_in.shape[0], mla_w_down.shape[0]
    gqa = [_prep_gqa(gqa_w_qkv[j], gqa_q_norm[j], gqa_k_norm[j]) for j in range(nA)]
    mla = [_prep_mla(mla_w_down[j], mla_w_uq[j], mla_q_norm[j], mla_k_rope_norm[j]) for j in range(nC)]
    gates = [_prep_gla_gates(gla_w_g1[j], gla_w_g2[j]) for j in range(nB)]
    p = dict(
        norm1_g=norm1_g, norm2_g=norm2_g,
        gqa_w_qkv=[g[0] for g in gqa], gqa_q_norm=[g[1] for g in gqa], gqa_k_norm=[g[2] for g in gqa],
        gqa_w_o=gqa_w_o.astype(BF16),
        gla_w_in=gla_w_in.astype(BF16), gla_w_g1=[g[0] for g in gates], gla_w_g2=[g[1] for g in gates],
        gla_b_g=gla_b_g, gla_o_norm=gla_o_norm, gla_w_o=gla_w_o.astype(BF16),
        mla_w_down=[m[0] for m in mla], mla_w_uq=[m[1] for m in mla], mla_q_gain=[m[2] for m in mla],
        mla_kr_gain=[m[3] for m in mla], mla_q_a_norm=mla_q_a_norm, mla_kv_a_norm=mla_kv_a_norm,
        mla_w_ukv=mla_w_ukv.astype(BF16), mla_k_nope_norm=mla_k_nope_norm, mla_w_o=mla_w_o.astype(BF16),
        mlp_w1=mlp_w1.astype(BF16), mlp_w2=mlp_w2.astype(BF16),
    )
    bp, bs = c_prompt.shape[0], c_sample.shape[0]
    nb_pad = -(-(bp + bs) // 16) * 16
    c_all = jnp.concatenate([c_prompt, c_sample, jnp.zeros((nb_pad - bp - bs, c_prompt.shape[1]), F32)], axis=0)
    mod = _ada_modulation(c_all, ada_w, ada_b)
    y_prompt = _trunk(x_prompt, mod[:, :bp], p)
    y_sample = _trunk(x_sample, mod[:, bp:bp + bs], p)
    return (y_prompt, y_sample)
```

```python
import functools
import math

import numpy as np
import jax
import jax.numpy as jnp
from jax import lax
from jax.experimental import pallas as pl
from jax.experimental.pallas import tpu as pltpu

F32 = jnp.float32
BF16 = jnp.bfloat16

D_MODEL = 2048
DEPTH = 4
GRID_W = 64
ROPE_THETA = 10000.0
NORM_EPS = 1e-6
N_MIXERS = 3

GQA_HEAD_DIM = 128
GQA_Q_HEADS = 16
GQA_KV_HEADS = 4
GQA_GROUP = 4

GLA_HEADS = 4
GLA_DK = 1024
GLA_DV = 2048
GLA_DK_HEAD = 256
GLA_DV_HEAD = 512
GLA_GATE_RANK = 16
GLA_GATE_TAU = 16.0
GLA_BLOCK = 128

MLA_HEADS = 16
MLA_Q_RANK = 512
MLA_KV_RANK = 256
MLA_NOPE_DIM = 128
MLA_ROPE_DIM = 64
MLA_QK_DIM = 192
MLA_V_DIM = 128
MLA_QPAD = 256
MLA_DOWN_PAD = 896

D_FF = 4 * D_MODEL

LANES = 128
VMEM_LIMIT_BYTES = 56 * 1024 * 1024


def _tiles(n_rows, seq):
    tm = min(512, seq)
    assert seq % tm == 0 and n_rows % tm == 0
    return tm


def _cparams(*sem):
    return pltpu.CompilerParams(dimension_semantics=sem, vmem_limit_bytes=VMEM_LIMIT_BYTES)


def _ada_kernel(c_ref, w_ref, b_ref, o_ref):
    c = c_ref[...]
    cond = (c * (1.0 / (1.0 + jnp.exp(-c)))).astype(BF16)
    o_ref[...] = jnp.dot(cond, w_ref[...].astype(BF16), preferred_element_type=F32) + b_ref[...]


def _ada_modulation(c_all, ada_w, ada_b):
    nb, d = c_all.shape
    depth, _, n6 = ada_w.shape
    tn = 1024
    return pl.pallas_call(
        _ada_kernel,
        grid=(depth, n6 // tn),
        in_specs=[
            pl.BlockSpec((nb, d), lambda i, j: (0, 0)),
            pl.BlockSpec((None, d, tn), lambda i, j: (i, 0, j)),
            pl.BlockSpec((None, 1, tn), lambda i, j: (i, 0, j)),
        ],
        out_specs=pl.BlockSpec((None, nb, tn), lambda i, j: (i, 0, j)),
        out_shape=jax.ShapeDtypeStruct((depth, nb, n6), F32),
        compiler_params=_cparams("parallel", "parallel"),
    )(c_all, ada_w, ada_b.reshape(depth, 1, n6))


def _rms(x, g, n):
    ms = jnp.sum(x * x, axis=-1, keepdims=True) * (1.0 / n)
    return x * lax.rsqrt(ms + NORM_EPS) * g


NORM_ROWS = 16


def _store_norm(h_ref, x_ref, gain, shift):
    d = x_ref.shape[-1]
    blocks = [slice(r, r + NORM_ROWS) for r in range(0, x_ref.shape[0], NORM_ROWS)]
    inv = []
    for rows in blocks:
        x = x_ref[rows, :].astype(F32)
        inv.append(lax.rsqrt(jnp.sum(x * x, axis=-1, keepdims=True) * (1.0 / d) + NORM_EPS))
    for rows, inv_r in zip(blocks, inv):
        y = x_ref[rows, :].astype(F32) * inv_r * gain
        h_ref[rows, :] = (y if shift is None else y + shift).astype(BF16)


def _norm_mod_matmul_kernel(x_ref, g_ref, sc_ref, sh_ref, w_ref, o_ref, h_ref):
    @pl.when(pl.program_id(1) == 0)
    def _():
        _store_norm(h_ref, x_ref, g_ref[...] * (1.0 + sc_ref[...]), sh_ref[...])

    o_ref[...] = jnp.dot(h_ref[...], w_ref[...], preferred_element_type=F32).astype(o_ref.dtype)


def _norm_matmul_kernel(x_ref, g_ref, w_ref, o_ref, h_ref):
    @pl.when(pl.program_id(1) == 0)
    def _():
        _store_norm(h_ref, x_ref, g_ref[...], None)

    o_ref[...] = jnp.dot(h_ref[...], w_ref[...], preferred_element_type=F32).astype(o_ref.dtype)


def _norm_mod_matmul(x, g, sc, sh, w, seq, out_dtype, tn=None):
    n, d = x.shape
    nout = w.shape[1]
    tm = _tiles(n, seq)
    tn = nout if tn is None else tn
    tpb = seq // tm
    n_j = nout // tn
    return pl.pallas_call(
        _norm_mod_matmul_kernel,
        grid=(n // tm, n_j),
        in_specs=[
            pl.BlockSpec((tm, d), lambda i, j: (i, 0)),
            pl.BlockSpec((1, d), lambda i, j: (0, 0)),
            pl.BlockSpec((None, 1, d), lambda i, j: (i // tpb, 0, 0)),
            pl.BlockSpec((None, 1, d), lambda i, j: (i // tpb, 0, 0)),
            pl.BlockSpec((d, tn), lambda i, j: (0, _serpentine(i, j, n_j))),
        ],
        out_specs=pl.BlockSpec((tm, tn), lambda i, j: (i, _serpentine(i, j, n_j))),
        out_shape=jax.ShapeDtypeStruct((n, nout), out_dtype),
        scratch_shapes=[pltpu.VMEM((tm, d), BF16)],
        compiler_params=_cparams("parallel", "arbitrary"),
    )(x, g.reshape(1, d), sc, sh, w)


def _norm_matmul(x, col_block, width, g, w, seq, out_dtype):
    n = x.shape[0]
    nout = w.shape[1]
    tm = _tiles(n, seq)
    return pl.pallas_call(
        _norm_matmul_kernel,
        grid=(n // tm, 1),
        in_specs=[
            pl.BlockSpec((tm, width), lambda i, j: (i, col_block)),
            pl.BlockSpec((1, width), lambda i, j: (0, 0)),
            pl.BlockSpec((width, nout), lambda i, j: (0, 0)),
        ],
        out_specs=pl.BlockSpec((tm, nout), lambda i, j: (i, 0)),
        out_shape=jax.ShapeDtypeStruct((n, nout), out_dtype),
        scratch_shapes=[pltpu.VMEM((tm, width), BF16)],
        compiler_params=_cparams("parallel", "arbitrary"),
    )(x, g.reshape(1, width), w)


def _oproj_kernel(a_ref, w_ref, x_ref, gate_ref, o_ref):
    y = jnp.dot(a_ref[...], w_ref[...], preferred_element_type=F32)
    o_ref[...] = x_ref[...] + gate_ref[...] * y


def _oproj_residual(a, w, x, gate, seq):
    n, k = a.shape
    d = w.shape[1]
    tm = _tiles(n, seq)
    tpb = seq // tm
    return pl.pallas_call(
        _oproj_kernel,
        grid=(n // tm,),
        in_specs=[
            pl.BlockSpec((tm, k), lambda i: (i, 0)),
            pl.BlockSpec((k, d), lambda i: (0, 0)),
            pl.BlockSpec((tm, d), lambda i: (i, 0)),
            pl.BlockSpec((None, 1, d), lambda i: (i // tpb, 0, 0)),
        ],
        out_specs=pl.BlockSpec((tm, d), lambda i: (i, 0)),
        out_shape=jax.ShapeDtypeStruct((n, d), F32),
        compiler_params=_cparams("parallel"),
    )(a, w, x, gate)


def _mlp_kernel(x_ref, g_ref, sc_ref, sh_ref, gate_ref, w1_ref, w2_ref, o_ref, h_ref):
    j = pl.program_id(1)

    @pl.when(j == 0)
    def _():
        _store_norm(h_ref, x_ref, g_ref[...] * (1.0 + sc_ref[...]), sh_ref[...])
        o_ref[...] = jnp.zeros_like(o_ref)

    a = jnp.maximum(jnp.dot(h_ref[...], w1_ref[...], preferred_element_type=F32), 0.0)
    o_ref[...] += jnp.dot((a * a).astype(BF16), w2_ref[...], preferred_element_type=F32)

    @pl.when(j == pl.num_programs(1) - 1)
    def _():
        o_ref[...] = x_ref[...] + gate_ref[...] * o_ref[...]


def _serpentine(i, j, n_j):
    return jnp.where(lax.rem(i, 2) == 0, j, n_j - 1 - j)


def _mlp_residual(x, g, sc, sh, gate, w1, w2, seq):
    n, d = x.shape
    dff = w1.shape[1]
    tm = _tiles(n, seq)
    tf = min(1024, dff)
    assert dff % tf == 0
    n_j = dff // tf
    tpb = seq // tm
    mod_spec = pl.BlockSpec((None, 1, d), lambda i, j: (i // tpb, 0, 0))
    return pl.pallas_call(
        _mlp_kernel,
        grid=(n // tm, n_j),
        in_specs=[
            pl.BlockSpec((tm, d), lambda i, j: (i, 0)),
            pl.BlockSpec((1, d), lambda i, j: (0, 0)),
            mod_spec, mod_spec, mod_spec,
            pl.BlockSpec((d, tf), lambda i, j: (0, _serpentine(i, j, n_j))),
            pl.BlockSpec((tf, d), lambda i, j: (_serpentine(i, j, n_j), 0)),
        ],
        out_specs=pl.BlockSpec((tm, d), lambda i, j: (i, 0)),
        out_shape=jax.ShapeDtypeStruct((n, d), F32),
        scratch_shapes=[pltpu.VMEM((tm, d), BF16)],
        compiler_params=_cparams("parallel", "arbitrary"),
    )(x, g.reshape(1, d), sc, sh, gate, w1, w2)


def _axial_angles(n_tok, rot_dim):
    rows = n_tok // GRID_W
    row = jnp.repeat(jnp.arange(rows, dtype=F32), GRID_W)
    col = jnp.tile(jnp.arange(GRID_W, dtype=F32), rows)
    n_freq = rot_dim // 4
    inv = ROPE_THETA ** (-jnp.arange(n_freq, dtype=F32) / n_freq)
    ang = jnp.concatenate([row[:, None] * inv, col[:, None] * inv], axis=-1)
    return jnp.cos(ang), jnp.sin(ang)


def _gqa_rope_tables(n_tok):
    cos, sin = _axial_angles(n_tok, GQA_HEAD_DIM)
    return jnp.concatenate([cos, cos], -1), jnp.concatenate([-sin, sin], -1)


def _mla_rope_tables(n_tok):
    cos, sin = _axial_angles(n_tok, MLA_ROPE_DIM)
    z = jnp.zeros_like(cos)
    return jnp.concatenate([cos, z, cos, z], -1), jnp.concatenate([-sin, z, sin, z], -1)


_HALF_SPLIT_128 = np.concatenate([np.arange(0, 128, 2), np.arange(1, 128, 2)])


def _mla_rot_layout(base):
    idx = np.zeros(128, np.int32)
    msk = np.zeros(128, np.float32)
    idx[0:32] = base + np.arange(0, 64, 2)
    idx[64:96] = base + np.arange(1, 64, 2)
    msk[0:32] = 1.0
    msk[64:96] = 1.0
    return idx, msk


def _mla_q_layout():
    ridx, rmsk = _mla_rot_layout(MLA_NOPE_DIM)
    idx = np.concatenate([np.arange(MLA_NOPE_DIM), ridx])
    msk = np.concatenate([np.ones(MLA_NOPE_DIM, np.float32), rmsk])
    return idx, msk


def _rope(x, cos, sin):
    return x * cos + pltpu.roll(x, 64, 1) * sin


LOG2E = math.log2(math.e)


MAX_UNSHIFTED_SCORE = 40.0


def _flash_t(q_t, kn_ref, vt_ref, n_keys, tk, bounded):
    m_cols = q_t.shape[1]
    m = jnp.full((1, m_cols), -jnp.inf, F32)
    l8 = jnp.zeros((8, m_cols), F32)
    acc = jnp.zeros((vt_ref.shape[0], m_cols), F32)
    for c in range(n_keys // tk):
        s = jnp.dot(kn_ref[c * tk:(c + 1) * tk, :], q_t, preferred_element_type=F32)
        if bounded:
            p = jnp.exp2(s)
            l8 = l8 + jnp.sum(p.reshape(tk // 8, 8, m_cols), axis=0)
            acc = acc + jnp.dot(vt_ref[:, c * tk:(c + 1) * tk], p.astype(BF16), preferred_element_type=F32)
        else:
            m_new = jnp.maximum(m, jnp.max(s, axis=0, keepdims=True))
            alpha = jnp.exp2(m - m_new)
            p = jnp.exp2(s - m_new)
            l8 = alpha * l8 + jnp.sum(p.reshape(tk // 8, 8, m_cols), axis=0)
            acc = alpha * acc + jnp.dot(vt_ref[:, c * tk:(c + 1) * tk], p.astype(BF16),
                                        preferred_element_type=F32)
            m = m_new
    return acc * (1.0 / jnp.sum(l8, axis=0, keepdims=True))


def _store_vt(vt_ref, v_load, n_keys, blk):
    for c in range(n_keys // blk):
        vt_ref[:, c * blk:(c + 1) * blk] = v_load(c * blk, blk).astype(F32).T.astype(BF16)


def _gqa_attn_kernel(q_ref, k_ref, v_ref, cq_ref, sq_ref, ck_ref, sk_ref, qg_ref, kg_ref, o_ref, kn_ref, vt_ref,
                     *, tk, kprep, bounded):
    n_keys = k_ref.shape[0]
    tq = q_ref.shape[0]

    @pl.when(pl.program_id(2) == 0)
    def _():
        def prep(c, _):
            rows = pl.ds(pl.multiple_of(c * kprep, kprep), kprep)
            k = _rms(k_ref[rows, :].astype(F32), kg_ref[...], GQA_HEAD_DIM)
            kn_ref[rows, :] = _rope(k, ck_ref[rows, :], sk_ref[rows, :]).astype(BF16)
            return 0
        lax.fori_loop(0, n_keys // kprep, prep, 0)
        _store_vt(vt_ref, lambda r0, nr: v_ref[r0:r0 + nr, :], n_keys, kprep)

    scale = GQA_HEAD_DIM ** -0.5 * LOG2E
    qs = []
    for h in range(GQA_GROUP):
        q = _rms(q_ref[:, h * GQA_HEAD_DIM:(h + 1) * GQA_HEAD_DIM].astype(F32), qg_ref[...], GQA_HEAD_DIM)
        qs.append((_rope(q, cq_ref[...], sq_ref[...]) * scale).T.astype(BF16))
    q_t = jnp.concatenate(qs, axis=1)
    out_t = _flash_t(q_t, kn_ref, vt_ref, n_keys, tk, bounded)
    for h in range(GQA_GROUP):
        o_ref[:, h * GQA_HEAD_DIM:(h + 1) * GQA_HEAD_DIM] = out_t[:, h * tq:(h + 1) * tq].T.astype(o_ref.dtype)


def _dispatch_on_score_bound(bound, attention, *args):
    return lax.cond(bound <= MAX_UNSHIFTED_SCORE,
                    functools.partial(attention, bounded=True),
                    functools.partial(attention, bounded=False), *args)


def _gqa_score_bound(q_gain, k_gain):
    return (GQA_HEAD_DIM * jnp.max(jnp.abs(q_gain)) * jnp.max(jnp.abs(k_gain))) * (GQA_HEAD_DIM ** -0.5 * LOG2E)


def _mla_score_bound(q_gain, kn_gain, kr_gain):
    q_norm = math.sqrt(MLA_QK_DIM) * jnp.max(jnp.abs(q_gain))
    k_norm = jnp.sqrt(MLA_NOPE_DIM * jnp.max(kn_gain * kn_gain) + MLA_ROPE_DIM * jnp.max(kr_gain * kr_gain))
    return q_norm * k_norm * (MLA_QK_DIM ** -0.5 * LOG2E)


def _gqa_attention(qkv, cos, sin, q_gain, k_gain, *, bounded):
    b, t, _ = qkv.shape
    tq = min(512 if t <= 2048 else 256, t)
    tk = min(512, t)
    kprep = min(512, t)
    gw = GQA_GROUP * GQA_HEAD_DIM
    k_col0 = GQA_Q_HEADS
    v_col0 = GQA_Q_HEADS + GQA_KV_HEADS
    tab_q = pl.BlockSpec((tq, LANES), lambda bi, g, qi: (qi, 0))
    tab_k = pl.BlockSpec((t, LANES), lambda bi, g, qi: (0, 0), pipeline_mode=pl.Buffered(1))
    gain = pl.BlockSpec((1, LANES), lambda bi, g, qi: (0, 0))
    return pl.pallas_call(
        functools.partial(_gqa_attn_kernel, tk=tk, kprep=kprep, bounded=bounded),
        grid=(b, GQA_KV_HEADS, t // tq),
        in_specs=[
            pl.BlockSpec((None, tq, gw), lambda bi, g, qi: (bi, qi, g)),
            pl.BlockSpec((None, t, GQA_HEAD_DIM), lambda bi, g, qi: (bi, 0, k_col0 + g)),
            pl.BlockSpec((None, t, GQA_HEAD_DIM), lambda bi, g, qi: (bi, 0, v_col0 + g)),
            tab_q, tab_q, tab_k, tab_k, gain, gain,
        ],
        out_specs=pl.BlockSpec((None, tq, gw), lambda bi, g, qi: (bi, qi, g)),
        out_shape=jax.ShapeDtypeStruct((b, t, GQA_Q_HEADS * GQA_HEAD_DIM), BF16),
        scratch_shapes=[pltpu.VMEM((t, GQA_HEAD_DIM), BF16), pltpu.VMEM((GQA_HEAD_DIM, t), BF16)],
        compiler_params=_cparams("parallel", "parallel", "arbitrary"),
    )(qkv, qkv, qkv, cos, sin, cos, sin, q_gain.reshape(1, LANES), k_gain.reshape(1, LANES))


def _mla_attn_kernel(q_ref, kv_ref, kr_ref, cq_ref, sq_ref, ck_ref, sk_ref, qg_ref, kng_ref, krg_ref,
                     o_ref, kn_ref, vt_ref, *, tk, kprep, bounded):
    n_keys = kv_ref.shape[0]

    @pl.when(pl.program_id(2) == 0)
    def _():
        def prep(c, _):
            rows = pl.ds(pl.multiple_of(c * kprep, kprep), kprep)
            kn = _rms(kv_ref[rows, 0:MLA_NOPE_DIM].astype(F32), kng_ref[...], MLA_NOPE_DIM)
            kn_ref[rows, 0:MLA_NOPE_DIM] = kn.astype(BF16)
            kr = _rms(kr_ref[rows, :].astype(F32), krg_ref[...], MLA_ROPE_DIM)
            kn_ref[rows, MLA_NOPE_DIM:MLA_QPAD] = _rope(kr, ck_ref[rows, :], sk_ref[rows, :]).astype(BF16)
            return 0
        lax.fori_loop(0, n_keys // kprep, prep, 0)
        _store_vt(vt_ref, lambda r0, nr: kv_ref[r0:r0 + nr, MLA_NOPE_DIM:MLA_QPAD], n_keys, kprep)

    scale = MLA_QK_DIM ** -0.5 * LOG2E
    q = q_ref[...].astype(F32)
    g = qg_ref[...]
    inv = lax.rsqrt(jnp.sum(q * q, axis=-1, keepdims=True) * (1.0 / MLA_QK_DIM) + NORM_EPS)
    q_nope = q[:, 0:MLA_NOPE_DIM] * inv * g[:, 0:MLA_NOPE_DIM]
    q_rot = _rope(q[:, MLA_NOPE_DIM:MLA_QPAD] * inv * g[:, MLA_NOPE_DIM:MLA_QPAD], cq_ref[...], sq_ref[...])
    q_t = jnp.concatenate([(q_nope * scale).T, (q_rot * scale).T], axis=0).astype(BF16)
    out_t = _flash_t(q_t, kn_ref, vt_ref, n_keys, tk, bounded)
    o_ref[...] = out_t.T.astype(o_ref.dtype)


def _mla_attention(q, kv, down, cos, sin, q_gain, kn_gain, kr_gain, *, bounded):
    b, t, _ = q.shape
    tq = min(2048 if t <= 2048 else 1024, t)
    tk = min(512, t)
    kprep = min(512, t)
    kr_col = (MLA_Q_RANK + MLA_KV_RANK) // LANES
    tab_q = pl.BlockSpec((tq, LANES), lambda bi, h, qi: (qi, 0))
    tab_k = pl.BlockSpec((t, LANES), lambda bi, h, qi: (0, 0), pipeline_mode=pl.Buffered(1))
    return pl.pallas_call(
        functools.partial(_mla_attn_kernel, tk=tk, kprep=kprep, bounded=bounded),
        grid=(b, MLA_HEADS, t // tq),
        in_specs=[
            pl.BlockSpec((None, tq, MLA_QPAD), lambda bi, h, qi: (bi, qi, h)),
            pl.BlockSpec((None, t, MLA_QPAD), lambda bi, h, qi: (bi, 0, h)),
            pl.BlockSpec((None, t, LANES), lambda bi, h, qi: (bi, 0, kr_col)),
            tab_q, tab_q, tab_k, tab_k,
            pl.BlockSpec((1, MLA_QPAD), lambda bi, h, qi: (0, 0)),
            pl.BlockSpec((1, LANES), lambda bi, h, qi: (0, 0)),
            pl.BlockSpec((1, LANES), lambda bi, h, qi: (0, 0)),
        ],
        out_specs=pl.BlockSpec((None, tq, MLA_V_DIM), lambda bi, h, qi: (bi, qi, h)),
        out_shape=jax.ShapeDtypeStruct((b, t, MLA_HEADS * MLA_V_DIM), BF16),
        scratch_shapes=[pltpu.VMEM((t, MLA_QPAD), BF16), pltpu.VMEM((MLA_V_DIM, t), BF16)],
        compiler_params=_cparams("parallel", "parallel", "arbitrary"),
    )(q, kv, down, cos, sin, cos, sin, q_gain.reshape(1, MLA_QPAD), kn_gain.reshape(1, LANES),
      kr_gain.reshape(1, LANES))


def _split3(x):
    hi = x.astype(BF16)
    r1 = x - hi.astype(F32)
    mid = r1.astype(BF16)
    lo = (r1 - mid.astype(F32)).astype(BF16)
    return hi, mid, lo


def _gla_kernel(*refs, reverse, n_chunks):
    if reverse:
        (q_ref, k_ref, v_ref, low_ref, wg_ref, bg_ref, of_ref, r_ref, on_ref, o_ref, s_ref) = refs
    else:
        (q_ref, k_ref, v_ref, low_ref, wg_ref, bg_ref, o_ref, s_ref) = refs
    L = GLA_BLOCK

    @pl.when(pl.program_id(2) == 0)
    def _():
        s_ref[...] = jnp.zeros_like(s_ref)

    row = lax.broadcasted_iota(jnp.int32, (L, L), 0)
    col = lax.broadcasted_iota(jnp.int32, (L, L), 1)
    keep = (col >= row) if reverse else (col <= row)
    tri = jnp.where(keep, 1.0, 0.0).astype(BF16)
    tn_dims = (((0,), (0,)), ((), ()))
    nt_dims = (((1,), (1,)), ((), ()))
    mid = L // 2 if reverse else L // 2 - 1
    edge = 0 if reverse else L - 1

    order = list(range(n_chunks - 1, -1, -1) if reverse else range(n_chunks))
    rows = {c: slice(c * L, (c + 1) * L) for c in order}

    z = jnp.dot(low_ref[...].astype(BF16), wg_ref[...], preferred_element_type=F32) + bg_ref[...]
    cum = {}
    for c in order:
        zc = z[rows[c], :]
        la = (jnp.minimum(zc, 0.0) - jnp.log1p(jnp.exp(-jnp.abs(zc)))) * (1.0 / GLA_GATE_TAU)
        cum[c] = sum(jnp.dot(tri, p, preferred_element_type=F32) for p in _split3(la))
    qd, qs, kd, ke, de = {}, {}, {}, {}, {}
    for c in order:
        ref_row = cum[c][mid:mid + 1, :]
        tot_row = cum[c][edge:edge + 1, :]
        q = q_ref[rows[c], :].astype(F32) * (GLA_DK_HEAD ** -0.5)
        k = k_ref[rows[c], :].astype(F32)
        q_mid = q * jnp.exp(cum[c] - ref_row)
        qd[c] = q_mid.astype(BF16)
        qs[c] = (q_mid * jnp.exp(ref_row)).astype(BF16)
        kd[c] = (k * jnp.exp(ref_row - cum[c])).astype(BF16)
        ke[c] = (k * jnp.exp(tot_row - cum[c])).astype(BF16)
        col_de = jnp.broadcast_to(jnp.exp(tot_row), (LANES, GLA_DK_HEAD)).T
        de[c] = jnp.concatenate([col_de] * (GLA_DV_HEAD // LANES), axis=1)
    a = {c: jnp.where(keep, lax.dot_general(qd[c], kd[c], nt_dims, preferred_element_type=F32), 0.0).astype(BF16)
         for c in order}
    o_intra = {c: jnp.dot(a[c], v_ref[rows[c], :], preferred_element_type=F32) for c in order}
    kv = {c: lax.dot_general(ke[c], v_ref[rows[c], :], tn_dims, preferred_element_type=F32) for c in order}

    s = s_ref[...]
    for c in order:
        o = o_intra[c] + jnp.dot(qs[c], s.astype(BF16), preferred_element_type=F32)
        s = de[c] * s + kv[c]
        if reverse:
            o = o + of_ref[rows[c], :]
            o = _rms(o, on_ref[...], GLA_DV_HEAD)
            r = r_ref[rows[c], :].astype(F32)
            o = o * (r * (1.0 / (1.0 + jnp.exp(-r))))
        o_ref[rows[c], :] = o.astype(o_ref.dtype)
    s_ref[...] = s


def _gla_direction(proj, low, wg, bg, o_fwd, o_norm, reverse):
    b, t, _ = proj.shape
    tb = min(1024, t)
    assert t % tb == 0 and tb % GLA_BLOCK == 0
    nblk = t // tb
    blk = (lambda i: nblk - 1 - i) if reverse else (lambda i: i)
    k_col0 = GLA_DK // GLA_DK_HEAD
    v_col0 = 2 * GLA_DK // GLA_DV_HEAD
    r_col0 = (2 * GLA_DK + GLA_DV) // GLA_DV_HEAD
    in_specs = [
        pl.BlockSpec((None, tb, GLA_DK_HEAD), lambda bi, h, i: (bi, blk(i), h)),
        pl.BlockSpec((None, tb, GLA_DK_HEAD), lambda bi, h, i: (bi, blk(i), k_col0 + h)),
        pl.BlockSpec((None, tb, GLA_DV_HEAD), lambda bi, h, i: (bi, blk(i), v_col0 + h)),
        pl.BlockSpec((None, tb, LANES), lambda bi, h, i: (bi, blk(i), 0)),
        pl.BlockSpec((LANES, GLA_DK_HEAD), lambda bi, h, i: (0, h)),
        pl.BlockSpec((1, GLA_DK_HEAD), lambda bi, h, i: (0, h)),
    ]
    args = [proj, proj, proj, low, wg, bg]
    if reverse:
        in_specs += [
            pl.BlockSpec((None, tb, GLA_DV_HEAD), lambda bi, h, i: (bi, blk(i), h)),
            pl.BlockSpec((None, tb, GLA_DV_HEAD), lambda bi, h, i: (bi, blk(i), r_col0 + h)),
            pl.BlockSpec((1, GLA_DV_HEAD), lambda bi, h, i: (0, 0)),
        ]
        args += [o_fwd, proj, o_norm.reshape(1, GLA_DV_HEAD)]
    return pl.pallas_call(
        functools.partial(_gla_kernel, reverse=reverse, n_chunks=tb // GLA_BLOCK),
        grid=(b, GLA_HEADS, nblk),
        in_specs=in_specs,
        out_specs=pl.BlockSpec((None, tb, GLA_DV_HEAD), lambda bi, h, i: (bi, blk(i), h)),
        out_shape=jax.ShapeDtypeStruct((b, t, GLA_DV), BF16 if reverse else F32),
        scratch_shapes=[pltpu.VMEM((GLA_DK_HEAD, GLA_DV_HEAD), F32)],
        compiler_params=_cparams("parallel", "parallel", "arbitrary"),
    )(*args)


def _prep_gqa(w_qkv, q_norm, k_norm):
    n_rot = GQA_Q_HEADS + GQA_KV_HEADS
    cols = (np.arange(n_rot)[:, None] * GQA_HEAD_DIM + _HALF_SPLIT_128[None, :]).reshape(-1)
    cols = np.concatenate([cols, np.arange(n_rot * GQA_HEAD_DIM, w_qkv.shape[1])])
    return w_qkv[:, cols].astype(BF16), q_norm[_HALF_SPLIT_128], k_norm[_HALF_SPLIT_128]


def _prep_mla(w_down, w_uq, q_norm, k_rope_norm):
    qidx, qmsk = _mla_q_layout()
    cols = (np.arange(MLA_HEADS)[:, None] * MLA_QK_DIM + qidx[None, :]).reshape(-1)
    w_uq_p = (w_uq[:, cols] * jnp.asarray(np.tile(qmsk, MLA_HEADS))).astype(BF16)
    q_gain = q_norm[qidx] * jnp.asarray(qmsk)
    ridx, rmsk = _mla_rot_layout(0)
    kr0 = MLA_Q_RANK + MLA_KV_RANK
    w_kr = w_down[:, kr0 + ridx] * jnp.asarray(rmsk)
    w_down_p = jnp.concatenate([w_down[:, :kr0], w_kr], axis=1).astype(BF16)
    kr_gain = k_rope_norm[ridx] * jnp.asarray(rmsk)
    return w_down_p, w_uq_p, q_gain, kr_gain


def _prep_gla_gates(w_g1, w_g2):
    r = GLA_GATE_RANK
    d = w_g1.shape[1]
    w1 = jnp.zeros((d, LANES), F32).at[:, 0:r].set(w_g1[0]).at[:, r:2 * r].set(w_g1[1])
    w2 = jnp.zeros((2, LANES, GLA_DK), F32).at[0, 0:r].set(w_g2[0]).at[1, r:2 * r].set(w_g2[1])
    return w1.astype(BF16), w2.astype(BF16)


def _trunk(x3, mod, p):
    b, t, d = x3.shape
    n = b * t
    x = x3.reshape(n, d)
    for i in range(DEPTH):
        m = mod[i].reshape(b, 6, 1, d)
        sh1, sc1, g1, sh2, sc2, g2 = (m[:, s] for s in range(6))
        kind, j = i % N_MIXERS, i // N_MIXERS
        if kind == 0:
            qkv = _norm_mod_matmul(x, p['norm1_g'][i], sc1, sh1, p['gqa_w_qkv'][j], t, BF16)
            cos, sin = _gqa_rope_tables(t)
            a = _dispatch_on_score_bound(_gqa_score_bound(p['gqa_q_norm'][j], p['gqa_k_norm'][j]), _gqa_attention,
                                         qkv.reshape(b, t, -1), cos, sin, p['gqa_q_norm'][j], p['gqa_k_norm'][j])
            x = _oproj_residual(a.reshape(n, -1), p['gqa_w_o'][j], x, g1, t)
        elif kind == 1:
            proj = _norm_mod_matmul(x, p['norm1_g'][i], sc1, sh1, p['gla_w_in'][j], t, BF16, tn=2048)
            low = _norm_mod_matmul(x, p['norm1_g'][i], sc1, sh1, p['gla_w_g1'][j], t, F32)
            proj3, low3 = proj.reshape(b, t, -1), low.reshape(b, t, -1)
            bg = p['gla_b_g'][j]
            o_f = _gla_direction(proj3, low3, p['gla_w_g2'][j][0], bg[0:1], None, None, reverse=False)
            a = _gla_direction(proj3, low3, p['gla_w_g2'][j][1], bg[1:2], o_f, p['gla_o_norm'][j], reverse=True)
            x = _oproj_residual(a.reshape(n, -1), p['gla_w_o'][j], x, g1, t)
        else:
            down = _norm_mod_matmul(x, p['norm1_g'][i], sc1, sh1, p['mla_w_down'][j], t, F32)
            q = _norm_matmul(down, 0, MLA_Q_RANK, p['mla_q_a_norm'][j], p['mla_w_uq'][j], t, BF16)
            kv = _norm_matmul(down, MLA_Q_RANK // MLA_KV_RANK, MLA_KV_RANK, p['mla_kv_a_norm'][j],
                              p['mla_w_ukv'][j], t, BF16)
            cos, sin = _mla_rope_tables(t)
            gains = (p['mla_q_gain'][j], p['mla_k_nope_norm'][j], p['mla_kr_gain'][j])
            a = _dispatch_on_score_bound(_mla_score_bound(*gains), _mla_attention, q.reshape(b, t, -1),
                                         kv.reshape(b, t, -1), down.reshape(b, t, -1), cos, sin, *gains)
            x = _oproj_residual(a.reshape(n, -1), p['mla_w_o'][j], x, g1, t)
        x = _mlp_residual(x, p['norm2_g'][i], sc2, sh2, g2, p['mlp_w1'][i], p['mlp_w2'][i], t)
    return x.reshape(b, t, d)


def kernel(x_prompt, x_sample, c_prompt, c_sample, norm1_g, norm2_g, ada_w, ada_b, gqa_w_qkv, gqa_q_norm, gqa_k_norm, gqa_w_o, gla_w_in, gla_w_g1, gla_w_g2, gla_b_g, gla_o_norm, gla_w_o, mla_w_down, mla_q_a_norm, mla_kv_a_norm, mla_w_uq, mla_w_ukv, mla_q_norm, mla_k_nope_norm, mla_k_rope_norm, mla_w_o, mlp_w1, mlp_w2):
    nA, nB, nC = gqa_w_qkv.shape[0], gla_w_in.shape[0], mla_w_down.shape[0]
    gqa = [_prep_gqa(gqa_w_qkv[j], gqa_q_norm[j], gqa_k_norm[j]) for j in range(nA)]
    mla = [_prep_mla(mla_w_down[j], mla_w_uq[j], mla_q_norm[j], mla_k_rope_norm[j]) for j in range(nC)]
    gates = [_prep_gla_gates(gla_w_g1[j], gla_w_g2[j]) for j in range(nB)]
    p = dict(
        norm1_g=norm1_g, norm2_g=norm2_g,
        gqa_w_qkv=[g[0] for g in gqa], gqa_q_norm=[g[1] for g in gqa], gqa_k_norm=[g[2] for g in gqa],
        gqa_w_o=gqa_w_o.astype(BF16),
        gla_w_in=gla_w_in.astype(BF16), gla_w_g1=[g[0] for g in gates], gla_w_g2=[g[1] for g in gates],
        gla_b_g=gla_b_g, gla_o_norm=gla_o_norm, gla_w_o=gla_w_o.astype(BF16),
        mla_w_down=[m[0] for m in mla], mla_w_uq=[m[1] for m in mla], mla_q_gain=[m[2] for m in mla],
        mla_kr_gain=[m[3] for m in mla], mla_q_a_norm=mla_q_a_norm, mla_kv_a_norm=mla_kv_a_norm,
        mla_w_ukv=mla_w_ukv.astype(BF16), mla_k_nope_norm=mla_k_nope_norm, mla_w_o=mla_w_o.astype(BF16),
        mlp_w1=mlp_w1.astype(BF16), mlp_w2=mlp_w2.astype(BF16),
    )
    bp, bs = c_prompt.shape[0], c_sample.shape[0]
    nb_pad = -(-(bp + bs) // 16) * 16
    c_all = jnp.concatenate([c_prompt, c_sample, jnp.zeros((nb_pad - bp - bs, c_prompt.shape[1]), F32)], axis=0)
    mod = _ada_modulation(c_all, ada_w, ada_b)
    y_prompt = _trunk(x_prompt, mod[:, :bp], p)
    y_sample = _trunk(x_sample, mod[:, bp:bp + bs], p)
    return (y_prompt, y_sample)
```

```python
import functools
import math

import numpy as np
import jax
import jax.numpy as jnp
from jax import lax
from jax.experimental import pallas as pl
from jax.experimental.pallas import tpu as pltpu

F32 = jnp.float32
BF16 = jnp.bfloat16

D_MODEL = 2048
DEPTH = 4
GRID_W = 64
ROPE_THETA = 10000.0
NORM_EPS = 1e-6
N_MIXERS = 3

GQA_HEAD_DIM = 128
GQA_Q_HEADS = 16
GQA_KV_HEADS = 4
GQA_GROUP = 4

GLA_HEADS = 4
GLA_DK = 1024
GLA_DV = 2048
GLA_DK_HEAD = 256
GLA_DV_HEAD = 512
GLA_GATE_RANK = 16
GLA_GATE_TAU = 16.0
GLA_BLOCK = 128

MLA_HEADS = 16
MLA_Q_RANK = 512
MLA_KV_RANK = 256
MLA_NOPE_DIM = 128
MLA_ROPE_DIM = 64
MLA_QK_DIM = 192
MLA_V_DIM = 128
MLA_QPAD = 256
MLA_DOWN_PAD = 896

D_FF = 4 * D_MODEL

LANES = 128
VMEM_LIMIT_BYTES = 56 * 1024 * 1024


def _tiles(n_rows, seq):
    tm = min(512, seq)
    assert seq % tm == 0 and n_rows % tm == 0
    return tm


def _cparams(*sem):
    return pltpu.CompilerParams(dimension_semantics=sem, vmem_limit_bytes=VMEM_LIMIT_BYTES)


def _ada_kernel(c_ref, w_ref, b_ref, o_ref):
    c = c_ref[...]
    cond = (c * (1.0 / (1.0 + jnp.exp(-c)))).astype(BF16)
    o_ref[...] = jnp.dot(cond, w_ref[...].astype(BF16), preferred_element_type=F32) + b_ref[...]


def _ada_modulation(c_all, ada_w, ada_b):
    nb, d = c_all.shape
    depth, _, n6 = ada_w.shape
    tn = 1024
    return pl.pallas_call(
        _ada_kernel,
        grid=(depth, n6 // tn),
        in_specs=[
            pl.BlockSpec((nb, d), lambda i, j: (0, 0)),
            pl.BlockSpec((None, d, tn), lambda i, j: (i, 0, j)),
            pl.BlockSpec((None, 1, tn), lambda i, j: (i, 0, j)),
        ],
        out_specs=pl.BlockSpec((None, nb, tn), lambda i, j: (i, 0, j)),
        out_shape=jax.ShapeDtypeStruct((depth, nb, n6), F32),
        compiler_params=_cparams("parallel", "parallel"),
    )(c_all, ada_w, ada_b.reshape(depth, 1, n6))


def _rms(x, g, n):
    ms = jnp.sum(x * x, axis=-1, keepdims=True) * (1.0 / n)
    return x * lax.rsqrt(ms + NORM_EPS) * g


NORM_ROWS = 16


def _store_norm(h_ref, x_ref, gain, shift):
    d = x_ref.shape[-1]
    blocks = [slice(r, r + NORM_ROWS) for r in range(0, x_ref.shape[0], NORM_ROWS)]
    inv = []
    for rows in blocks:
        x = x_ref[rows, :].astype(F32)
        inv.append(lax.rsqrt(jnp.sum(x * x, axis=-1, keepdims=True) * (1.0 / d) + NORM_EPS))
    for rows, inv_r in zip(blocks, inv):
        y = x_ref[rows, :].astype(F32) * inv_r * gain
        h_ref[rows, :] = (y if shift is None else y + shift).astype(BF16)


def _norm_mod_matmul_kernel(x_ref, g_ref, sc_ref, sh_ref, w_ref, o_ref, h_ref):
    @pl.when(pl.program_id(1) == 0)
    def _():
        _store_norm(h_ref, x_ref, g_ref[...] * (1.0 + sc_ref[...]), sh_ref[...])

    o_ref[...] = jnp.dot(h_ref[...], w_ref[...], preferred_element_type=F32).astype(o_ref.dtype)


def _norm_matmul_kernel(x_ref, g_ref, w_ref, o_ref, h_ref):
    @pl.when(pl.program_id(1) == 0)
    def _():
        _store_norm(h_ref, x_ref, g_ref[...], None)

    o_ref[...] = jnp.dot(h_ref[...], w_ref[...], preferred_element_type=F32).astype(o_ref.dtype)


def _norm_mod_matmul(x, g, sc, sh, w, seq, out_dtype, tn=None):
    n, d = x.shape
    nout = w.shape[1]
    tm = _tiles(n, seq)
    tn = nout if tn is None else tn
    tpb = seq // tm
    n_j = nout // tn
    return pl.pallas_call(
        _norm_mod_matmul_kernel,
        grid=(n // tm, n_j),
        in_specs=[
            pl.BlockSpec((tm, d), lambda i, j: (i, 0)),
            pl.BlockSpec((1, d), lambda i, j: (0, 0)),
            pl.BlockSpec((None, 1, d), lambda i, j: (i // tpb, 0, 0)),
            pl.BlockSpec((None, 1, d), lambda i, j: (i // tpb, 0, 0)),
            pl.BlockSpec((d, tn), lambda i, j: (0, _serpentine(i, j, n_j))),
        ],
        out_specs=pl.BlockSpec((tm, tn), lambda i, j: (i, _serpentine(i, j, n_j))),
        out_shape=jax.ShapeDtypeStruct((n, nout), out_dtype),
        scratch_shapes=[pltpu.VMEM((tm, d), BF16)],
        compiler_params=_cparams("parallel", "arbitrary"),
    )(x, g.reshape(1, d), sc, sh, w)


def _norm_matmul(x, col_block, width, g, w, seq, out_dtype):
    n = x.shape[0]
    nout = w.shape[1]
    tm = _tiles(n, seq)
    return pl.pallas_call(
        _norm_matmul_kernel,
        grid=(n // tm, 1),
        in_specs=[
            pl.BlockSpec((tm, width), lambda i, j: (i, col_block)),
            pl.BlockSpec((1, width), lambda i, j: (0, 0)),
            pl.BlockSpec((width, nout), lambda i, j: (0, 0)),
        ],
        out_specs=pl.BlockSpec((tm, nout), lambda i, j: (i, 0)),
        out_shape=jax.ShapeDtypeStruct((n, nout), out_dtype),
        scratch_shapes=[pltpu.VMEM((tm, width), BF16)],
        compiler_params=_cparams("parallel", "arbitrary"),
    )(x, g.reshape(1, width), w)


def _oproj_kernel(a_ref, w_ref, x_ref, gate_ref, o_ref):
    y = jnp.dot(a_ref[...], w_ref[...], preferred_element_type=F32)
    o_ref[...] = x_ref[...] + gate_ref[...] * y


def _oproj_residual(a, w, x, gate, seq):
    n, k = a.shape
    d = w.shape[1]
    tm = _tiles(n, seq)
    tpb = seq // tm
    return pl.pallas_call(
        _oproj_kernel,
        grid=(n // tm,),
        in_specs=[
            pl.BlockSpec((tm, k), lambda i: (i, 0)),
            pl.BlockSpec((k, d), lambda i: (0, 0)),
            pl.BlockSpec((tm, d), lambda i: (i, 0)),
            pl.BlockSpec((None, 1, d), lambda i: (i // tpb, 0, 0)),
        ],
        out_specs=pl.BlockSpec((tm, d), lambda i: (i, 0)),
        out_shape=jax.ShapeDtypeStruct((n, d), F32),
        compiler_params=_cparams("parallel"),
    )(a, w, x, gate)


def _mlp_kernel(x_ref, g_ref, sc_ref, sh_ref, gate_ref, w1_ref, w2_ref, o_ref, h_ref):
    j = pl.program_id(1)
    last = pl.num_programs(1) - 1

    def hidden():
        a = jnp.maximum(jnp.dot(h_ref[...], w1_ref[...], preferred_element_type=F32), 0.0)
        return jnp.dot((a * a).astype(BF16), w2_ref[...], preferred_element_type=F32)

    @pl.when(j == 0)
    def _():
        _store_norm(h_ref, x_ref, g_ref[...] * (1.0 + sc_ref[...]), sh_ref[...])
        o_ref[...] = hidden()

    @pl.when((j > 0) & (j < last))
    def _():
        o_ref[...] += hidden()

    @pl.when(j == last)
    def _():
        o_ref[...] = x_ref[...] + gate_ref[...] * (o_ref[...] + hidden())


def _serpentine(i, j, n_j):
    return jnp.where(lax.rem(i, 2) == 0, j, n_j - 1 - j)


def _mlp_residual(x, g, sc, sh, gate, w1, w2, seq):
    n, d = x.shape
    dff = w1.shape[1]
    tm = _tiles(n, seq)
    tf = min(1024, dff)
    assert dff % tf == 0 and dff // tf >= 2
    n_j = dff // tf
    tpb = seq // tm
    mod_spec = pl.BlockSpec((None, 1, d), lambda i, j: (i // tpb, 0, 0))
    return pl.pallas_call(
        _mlp_kernel,
        grid=(n // tm, n_j),
        in_specs=[
            pl.BlockSpec((tm, d), lambda i, j: (i, 0)),
            pl.BlockSpec((1, d), lambda i, j: (0, 0)),
            mod_spec, mod_spec, mod_spec,
            pl.BlockSpec((d, tf), lambda i, j: (0, _serpentine(i, j, n_j))),
            pl.BlockSpec((tf, d), lambda i, j: (_serpentine(i, j, n_j), 0)),
        ],
        out_specs=pl.BlockSpec((tm, d), lambda i, j: (i, 0)),
        out_shape=jax.ShapeDtypeStruct((n, d), F32),
        scratch_shapes=[pltpu.VMEM((tm, d), BF16)],
        compiler_params=_cparams("parallel", "arbitrary"),
    )(x, g.reshape(1, d), sc, sh, gate, w1, w2)


def _axial_angles(n_tok, rot_dim):
    rows = n_tok // GRID_W
    row = jnp.repeat(jnp.arange(rows, dtype=F32), GRID_W)
    col = jnp.tile(jnp.arange(GRID_W, dtype=F32), rows)
    n_freq = rot_dim // 4
    inv = ROPE_THETA ** (-jnp.arange(n_freq, dtype=F32) / n_freq)
    ang = jnp.concatenate([row[:, None] * inv, col[:, None] * inv], axis=-1)
    return jnp.cos(ang), jnp.sin(ang)


def _gqa_rope_tables(n_tok):
    cos, sin = _axial_angles(n_tok, GQA_HEAD_DIM)
    return jnp.concatenate([cos, cos], -1), jnp.concatenate([-sin, sin], -1)


def _mla_rope_tables(n_tok):
    cos, sin = _axial_angles(n_tok, MLA_ROPE_DIM)
    z = jnp.zeros_like(cos)
    return jnp.concatenate([cos, z, cos, z], -1), jnp.concatenate([-sin, z, sin, z], -1)


_HALF_SPLIT_128 = np.concatenate([np.arange(0, 128, 2), np.arange(1, 128, 2)])


def _mla_rot_layout(base):
    idx = np.zeros(128, np.int32)
    msk = np.zeros(128, np.float32)
    idx[0:32] = base + np.arange(0, 64, 2)
    idx[64:96] = base + np.arange(1, 64, 2)
    msk[0:32] = 1.0
    msk[64:96] = 1.0
    return idx, msk


def _mla_q_layout():
    ridx, rmsk = _mla_rot_layout(MLA_NOPE_DIM)
    idx = np.concatenate([np.arange(MLA_NOPE_DIM), ridx])
    msk = np.concatenate([np.ones(MLA_NOPE_DIM, np.float32), rmsk])
    return idx, msk


def _rope(x, cos, sin):
    return x * cos + pltpu.roll(x, 64, 1) * sin


LOG2E = math.log2(math.e)


MAX_UNSHIFTED_SCORE = 40.0


def _flash_t(q_t, kn_ref, vt_ref, n_keys, tk, bounded):
    m_cols = q_t.shape[1]
    m = jnp.full((1, m_cols), -jnp.inf, F32)
    l8 = jnp.zeros((8, m_cols), F32)
    acc = jnp.zeros((vt_ref.shape[0], m_cols), F32)
    for c in range(n_keys // tk):
        s = jnp.dot(kn_ref[c * tk:(c + 1) * tk, :], q_t, preferred_element_type=F32)
        if bounded:
            p = jnp.exp2(s)
            l8 = l8 + jnp.sum(p.reshape(tk // 8, 8, m_cols), axis=0)
            acc = acc + jnp.dot(vt_ref[:, c * tk:(c + 1) * tk], p.astype(BF16), preferred_element_type=F32)
        else:
            m_new = jnp.maximum(m, jnp.max(s, axis=0, keepdims=True))
            alpha = jnp.exp2(m - m_new)
            p = jnp.exp2(s - m_new)
            l8 = alpha * l8 + jnp.sum(p.reshape(tk // 8, 8, m_cols), axis=0)
            acc = alpha * acc + jnp.dot(vt_ref[:, c * tk:(c + 1) * tk], p.astype(BF16),
                                        preferred_element_type=F32)
            m = m_new
    return acc * (1.0 / jnp.sum(l8, axis=0, keepdims=True))


def _store_vt(vt_ref, v_load, n_keys, blk):
    for c in range(n_keys // blk):
        vt_ref[:, c * blk:(c + 1) * blk] = v_load(c * blk, blk).astype(F32).T.astype(BF16)


def _gqa_attn_kernel(q_ref, k_ref, v_ref, cq_ref, sq_ref, ck_ref, sk_ref, qg_ref, kg_ref, o_ref, kn_ref, vt_ref,
                     *, tk, kprep, bounded):
    n_keys = k_ref.shape[0]
    tq = q_ref.shape[0]

    @pl.when(pl.program_id(2) == 0)
    def _():
        def prep(c, _):
            rows = pl.ds(pl.multiple_of(c * kprep, kprep), kprep)
            k = _rms(k_ref[rows, :].astype(F32), kg_ref[...], GQA_HEAD_DIM)
            kn_ref[rows, :] = _rope(k, ck_ref[rows, :], sk_ref[rows, :]).astype(BF16)
            return 0
        lax.fori_loop(0, n_keys // kprep, prep, 0)
        _store_vt(vt_ref, lambda r0, nr: v_ref[r0:r0 + nr, :], n_keys, kprep)

    scale = GQA_HEAD_DIM ** -0.5 * LOG2E
    qs = []
    for h in range(GQA_GROUP):
        q = _rms(q_ref[:, h * GQA_HEAD_DIM:(h + 1) * GQA_HEAD_DIM].astype(F32), qg_ref[...], GQA_HEAD_DIM)
        qs.append((_rope(q, cq_ref[...], sq_ref[...]) * scale).T.astype(BF16))
    q_t = jnp.concatenate(qs, axis=1)
    out_t = _flash_t(q_t, kn_ref, vt_ref, n_keys, tk, bounded)
    for h in range(GQA_GROUP):
        o_ref[:, h * GQA_HEAD_DIM:(h + 1) * GQA_HEAD_DIM] = out_t[:, h * tq:(h + 1) * tq].T.astype(o_ref.dtype)


def _dispatch_on_score_bound(bound, attention, *args):
    return lax.cond(bound <= MAX_UNSHIFTED_SCORE,
                    functools.partial(attention, bounded=True),
                    functools.partial(attention, bounded=False), *args)


def _gqa_score_bound(q_gain, k_gain):
    return (GQA_HEAD_DIM * jnp.max(jnp.abs(q_gain)) * jnp.max(jnp.abs(k_gain))) * (GQA_HEAD_DIM ** -0.5 * LOG2E)


def _mla_score_bound(q_gain, kn_gain, kr_gain):
    q_norm = math.sqrt(MLA_QK_DIM) * jnp.max(jnp.abs(q_gain))
    k_norm = jnp.sqrt(MLA_NOPE_DIM * jnp.max(kn_gain * kn_gain) + MLA_ROPE_DIM * jnp.max(kr_gain * kr_gain))
    return q_norm * k_norm * (MLA_QK_DIM ** -0.5 * LOG2E)


def _gqa_attention(qkv, cos, sin, q_gain, k_gain, *, bounded):
    b, t, _ = qkv.shape
    tq = min(512 if t <= 2048 else 256, t)
    tk = min(512 if t <= 2048 else 1024, t)
    kprep = min(512, t)
    gw = GQA_GROUP * GQA_HEAD_DIM
    k_col0 = GQA_Q_HEADS
    v_col0 = GQA_Q_HEADS + GQA_KV_HEADS
    tab_q = pl.BlockSpec((tq, LANES), lambda bi, g, qi: (qi, 0))
    tab_k = pl.BlockSpec((t, LANES), lambda bi, g, qi: (0, 0), pipeline_mode=pl.Buffered(1))
    gain = pl.BlockSpec((1, LANES), lambda bi, g, qi: (0, 0))
    return pl.pallas_call(
        functools.partial(_gqa_attn_kernel, tk=tk, kprep=kprep, bounded=bounded),
        grid=(b, GQA_KV_HEADS, t // tq),
        in_specs=[
            pl.BlockSpec((None, tq, gw), lambda bi, g, qi: (bi, qi, g)),
            pl.BlockSpec((None, t, GQA_HEAD_DIM), lambda bi, g, qi: (bi, 0, k_col0 + g)),
            pl.BlockSpec((None, t, GQA_HEAD_DIM), lambda bi, g, qi: (bi, 0, v_col0 + g)),
            tab_q, tab_q, tab_k, tab_k, gain, gain,
        ],
        out_specs=pl.BlockSpec((None, tq, gw), lambda bi, g, qi: (bi, qi, g)),
        out_shape=jax.ShapeDtypeStruct((b, t, GQA_Q_HEADS * GQA_HEAD_DIM), BF16),
        scratch_shapes=[pltpu.VMEM((t, GQA_HEAD_DIM), BF16), pltpu.VMEM((GQA_HEAD_DIM, t), BF16)],
        compiler_params=_cparams("parallel", "parallel", "arbitrary"),
    )(qkv, qkv, qkv, cos, sin, cos, sin, q_gain.reshape(1, LANES), k_gain.reshape(1, LANES))


def _mla_attn_kernel(q_ref, kv_ref, kr_ref, cq_ref, sq_ref, ck_ref, sk_ref, qg_ref, kng_ref, krg_ref,
                     o_ref, kn_ref, vt_ref, *, tk, kprep, bounded):
    n_keys = kv_ref.shape[0]

    @pl.when(pl.program_id(2) == 0)
    def _():
        def prep(c, _):
            rows = pl.ds(pl.multiple_of(c * kprep, kprep), kprep)
            kn = _rms(kv_ref[rows, 0:MLA_NOPE_DIM].astype(F32), kng_ref[...], MLA_NOPE_DIM)
            kn_ref[rows, 0:MLA_NOPE_DIM] = kn.astype(BF16)
            kr = _rms(kr_ref[rows, :].astype(F32), krg_ref[...], MLA_ROPE_DIM)
            kn_ref[rows, MLA_NOPE_DIM:MLA_QPAD] = _rope(kr, ck_ref[rows, :], sk_ref[rows, :]).astype(BF16)
            return 0
        lax.fori_loop(0, n_keys // kprep, prep, 0)
        _store_vt(vt_ref, lambda r0, nr: kv_ref[r0:r0 + nr, MLA_NOPE_DIM:MLA_QPAD], n_keys, kprep)

    scale = MLA_QK_DIM ** -0.5 * LOG2E
    q = q_ref[...].astype(F32)
    g = qg_ref[...]
    inv = lax.rsqrt(jnp.sum(q * q, axis=-1, keepdims=True) * (1.0 / MLA_QK_DIM) + NORM_EPS)
    q_nope = q[:, 0:MLA_NOPE_DIM] * inv * g[:, 0:MLA_NOPE_DIM]
    q_rot = _rope(q[:, MLA_NOPE_DIM:MLA_QPAD] * inv * g[:, MLA_NOPE_DIM:MLA_QPAD], cq_ref[...], sq_ref[...])
    q_t = jnp.concatenate([(q_nope * scale).T, (q_rot * scale).T], axis=0).astype(BF16)
    out_t = _flash_t(q_t, kn_ref, vt_ref, n_keys, tk, bounded)
    o_ref[...] = out_t.T.astype(o_ref.dtype)


def _mla_attention(q, kv, down, cos, sin, q_gain, kn_gain, kr_gain, *, bounded):
    b, t, _ = q.shape
    tq = min(2048 if t <= 2048 else 1024, t)
    tk = min(512, t)
    kprep = min(512, t)
    kr_col = (MLA_Q_RANK + MLA_KV_RANK) // LANES
    tab_q = pl.BlockSpec((tq, LANES), lambda bi, h, qi: (qi, 0))
    tab_k = pl.BlockSpec((t, LANES), lambda bi, h, qi: (0, 0), pipeline_mode=pl.Buffered(1))
    return pl.pallas_call(
        functools.partial(_mla_attn_kernel, tk=tk, kprep=kprep, bounded=bounded),
        grid=(b, MLA_HEADS, t // tq),
        in_specs=[
            pl.BlockSpec((None, tq, MLA_QPAD), lambda bi, h, qi: (bi, qi, h)),
            pl.BlockSpec((None, t, MLA_QPAD), lambda bi, h, qi: (bi, 0, h)),
            pl.BlockSpec((None, t, LANES), lambda bi, h, qi: (bi, 0, kr_col)),
            tab_q, tab_q, tab_k, tab_k,
            pl.BlockSpec((1, MLA_QPAD), lambda bi, h, qi: (0, 0)),
            pl.BlockSpec((1, LANES), lambda bi, h, qi: (0, 0)),
            pl.BlockSpec((1, LANES), lambda bi, h, qi: (0, 0)),
        ],
        out_specs=pl.BlockSpec((None, tq, MLA_V_DIM), lambda bi, h, qi: (bi, qi, h)),
        out_shape=jax.ShapeDtypeStruct((b, t, MLA_HEADS * MLA_V_DIM), BF16),
        scratch_shapes=[pltpu.VMEM((t, MLA_QPAD), BF16), pltpu.VMEM((MLA_V_DIM, t), BF16)],
        compiler_params=_cparams("parallel", "parallel", "arbitrary"),
    )(q, kv, down, cos, sin, cos, sin, q_gain.reshape(1, MLA_QPAD), kn_gain.reshape(1, LANES),
      kr_gain.reshape(1, LANES))


def _split3(x):
    hi = x.astype(BF16)
    r1 = x - hi.astype(F32)
    mid = r1.astype(BF16)
    lo = (r1 - mid.astype(F32)).astype(BF16)
    return hi, mid, lo


def _gla_kernel(*refs, reverse, n_chunks):
    if reverse:
        (q_ref, k_ref, v_ref, low_ref, wg_ref, bg_ref, of_ref, r_ref, on_ref, o_ref, s_ref) = refs
    else:
        (q_ref, k_ref, v_ref, low_ref, wg_ref, bg_ref, o_ref, s_ref) = refs
    L = GLA_BLOCK

    @pl.when(pl.program_id(2) == 0)
    def _():
        s_ref[...] = jnp.zeros_like(s_ref)

    row = lax.broadcasted_iota(jnp.int32, (L, L), 0)
    col = lax.broadcasted_iota(jnp.int32, (L, L), 1)
    keep = (col >= row) if reverse else (col <= row)
    tri = jnp.where(keep, 1.0, 0.0).astype(BF16)
    tn_dims = (((0,), (0,)), ((), ()))
    nt_dims = (((1,), (1,)), ((), ()))
    mid = L // 2 if reverse else L // 2 - 1
    edge = 0 if reverse else L - 1

    order = list(range(n_chunks - 1, -1, -1) if reverse else range(n_chunks))
    rows = {c: slice(c * L, (c + 1) * L) for c in order}

    z = jnp.dot(low_ref[...].astype(BF16), wg_ref[...], preferred_element_type=F32) + bg_ref[...]
    cum = {}
    for c in order:
        zc = z[rows[c], :]
        la = (jnp.minimum(zc, 0.0) - jnp.log1p(jnp.exp(-jnp.abs(zc)))) * (1.0 / GLA_GATE_TAU)
        cum[c] = sum(jnp.dot(tri, p, preferred_element_type=F32) for p in _split3(la))
    qd, qs, kd, ke, de = {}, {}, {}, {}, {}
    for c in order:
        ref_row = cum[c][mid:mid + 1, :]
        tot_row = cum[c][edge:edge + 1, :]
        q = q_ref[rows[c], :].astype(F32) * (GLA_DK_HEAD ** -0.5)
        k = k_ref[rows[c], :].astype(F32)
        q_mid = q * jnp.exp(cum[c] - ref_row)
        qd[c] = q_mid.astype(BF16)
        qs[c] = (q_mid * jnp.exp(ref_row)).astype(BF16)
        kd[c] = (k * jnp.exp(ref_row - cum[c])).astype(BF16)
        ke[c] = (k * jnp.exp(tot_row - cum[c])).astype(BF16)
        col_de = jnp.broadcast_to(jnp.exp(tot_row), (LANES, GLA_DK_HEAD)).T
        de[c] = jnp.concatenate([col_de] * (GLA_DV_HEAD // LANES), axis=1)
    a = {c: jnp.where(keep, lax.dot_general(qd[c], kd[c], nt_dims, preferred_element_type=F32), 0.0).astype(BF16)
         for c in order}
    o_intra = {c: jnp.dot(a[c], v_ref[rows[c], :], preferred_element_type=F32) for c in order}
    kv = {c: lax.dot_general(ke[c], v_ref[rows[c], :], tn_dims, preferred_element_type=F32) for c in order}

    s = s_ref[...]
    for c in order:
        o = o_intra[c] + jnp.dot(qs[c], s.astype(BF16), preferred_element_type=F32)
        s = de[c] * s + kv[c]
        if reverse:
            o = o + of_ref[rows[c], :]
            o = _rms(o, on_ref[...], GLA_DV_HEAD)
            r = r_ref[rows[c], :].astype(F32)
            o = o * (r * (1.0 / (1.0 + jnp.exp(-r))))
        o_ref[rows[c], :] = o.astype(o_ref.dtype)
    s_ref[...] = s


def _gla_direction(proj, low, wg, bg, o_fwd, o_norm, reverse):
    b, t, _ = proj.shape
    tb = min(1024, t)
    assert t % tb == 0 and tb % GLA_BLOCK == 0
    nblk = t // tb
    blk = (lambda i: nblk - 1 - i) if reverse else (lambda i: i)
    k_col0 = GLA_DK // GLA_DK_HEAD
    v_col0 = 2 * GLA_DK // GLA_DV_HEAD
    r_col0 = (2 * GLA_DK + GLA_DV) // GLA_DV_HEAD
    in_specs = [
        pl.BlockSpec((None, tb, GLA_DK_HEAD), lambda bi, h, i: (bi, blk(i), h)),
        pl.BlockSpec((None, tb, GLA_DK_HEAD), lambda bi, h, i: (bi, blk(i), k_col0 + h)),
        pl.BlockSpec((None, tb, GLA_DV_HEAD), lambda bi, h, i: (bi, blk(i), v_col0 + h)),
        pl.BlockSpec((None, tb, LANES), lambda bi, h, i: (bi, blk(i), 0)),
        pl.BlockSpec((LANES, GLA_DK_HEAD), lambda bi, h, i: (0, h)),
        pl.BlockSpec((1, GLA_DK_HEAD), lambda bi, h, i: (0, h)),
    ]
    args = [proj, proj, proj, low, wg, bg]
    if reverse:
        in_specs += [
            pl.BlockSpec((None, tb, GLA_DV_HEAD), lambda bi, h, i: (bi, blk(i), h)),
            pl.BlockSpec((None, tb, GLA_DV_HEAD), lambda bi, h, i: (bi, blk(i), r_col0 + h)),
            pl.BlockSpec((1, GLA_DV_HEAD), lambda bi, h, i: (0, 0)),
        ]
        args += [o_fwd, proj, o_norm.reshape(1, GLA_DV_HEAD)]
    return pl.pallas_call(
        functools.partial(_gla_kernel, reverse=reverse, n_chunks=tb // GLA_BLOCK),
        grid=(b, GLA_HEADS, nblk),
        in_specs=in_specs,
        out_specs=pl.BlockSpec((None, tb, GLA_DV_HEAD), lambda bi, h, i: (bi, blk(i), h)),
        out_shape=jax.ShapeDtypeStruct((b, t, GLA_DV), BF16 if reverse else F32),
        scratch_shapes=[pltpu.VMEM((GLA_DK_HEAD, GLA_DV_HEAD), F32)],
        compiler_params=_cparams("parallel", "parallel", "arbitrary"),
    )(*args)


def _prep_gqa(w_qkv, q_norm, k_norm):
    n_rot = GQA_Q_HEADS + GQA_KV_HEADS
    cols = (np.arange(n_rot)[:, None] * GQA_HEAD_DIM + _HALF_SPLIT_128[None, :]).reshape(-1)
    cols = np.concatenate([cols, np.arange(n_rot * GQA_HEAD_DIM, w_qkv.shape[1])])
    return w_qkv[:, cols].astype(BF16), q_norm[_HALF_SPLIT_128], k_norm[_HALF_SPLIT_128]


def _prep_mla(w_down, w_uq, q_norm, k_rope_norm):
    qidx, qmsk = _mla_q_layout()
    cols = (np.arange(MLA_HEADS)[:, None] * MLA_QK_DIM + qidx[None, :]).reshape(-1)
    w_uq_p = (w_uq[:, cols] * jnp.asarray(np.tile(qmsk, MLA_HEADS))).astype(BF16)
    q_gain = q_norm[qidx] * jnp.asarray(qmsk)
    ridx, rmsk = _mla_rot_layout(0)
    kr0 = MLA_Q_RANK + MLA_KV_RANK
    w_kr = w_down[:, kr0 + ridx] * jnp.asarray(rmsk)
    w_down_p = jnp.concatenate([w_down[:, :kr0], w_kr], axis=1).astype(BF16)
    kr_gain = k_rope_norm[ridx] * jnp.asarray(rmsk)
    return w_down_p, w_uq_p, q_gain, kr_gain


def _prep_gla_gates(w_g1, w_g2):
    r = GLA_GATE_RANK
    d = w_g1.shape[1]
    w1 = jnp.zeros((d, LANES), F32).at[:, 0:r].set(w_g1[0]).at[:, r:2 * r].set(w_g1[1])
    w2 = jnp.zeros((2, LANES, GLA_DK), F32).at[0, 0:r].set(w_g2[0]).at[1, r:2 * r].set(w_g2[1])
    return w1.astype(BF16), w2.astype(BF16)


def _trunk(x3, mod, p):
    b, t, d = x3.shape
    n = b * t
    x = x3.reshape(n, d)
    for i in range(DEPTH):
        m = mod[i].reshape(b, 6, 1, d)
        sh1, sc1, g1, sh2, sc2, g2 = (m[:, s] for s in range(6))
        kind, j = i % N_MIXERS, i // N_MIXERS
        if kind == 0:
            qkv = _norm_mod_matmul(x, p['norm1_g'][i], sc1, sh1, p['gqa_w_qkv'][j], t, BF16)
            cos, sin = _gqa_rope_tables(t)
            a = _dispatch_on_score_bound(_gqa_score_bound(p['gqa_q_norm'][j], p['gqa_k_norm'][j]), _gqa_attention,
                                         qkv.reshape(b, t, -1), cos, sin, p['gqa_q_norm'][j], p['gqa_k_norm'][j])
            x = _oproj_residual(a.reshape(n, -1), p['gqa_w_o'][j], x, g1, t)
        elif kind == 1:
            proj = _norm_mod_matmul(x, p['norm1_g'][i], sc1, sh1, p['gla_w_in'][j], t, BF16, tn=2048)
            low = _norm_mod_matmul(x, p['norm1_g'][i], sc1, sh1, p['gla_w_g1'][j], t, F32)
            proj3, low3 = proj.reshape(b, t, -1), low.reshape(b, t, -1)
            bg = p['gla_b_g'][j]
            o_f = _gla_direction(proj3, low3, p['gla_w_g2'][j][0], bg[0:1], None, None, reverse=False)
            a = _gla_direction(proj3, low3, p['gla_w_g2'][j][1], bg[1:2], o_f, p['gla_o_norm'][j], reverse=True)
            x = _oproj_residual(a.reshape(n, -1), p['gla_w_o'][j], x, g1, t)
        else:
            down = _norm_mod_matmul(x, p['norm1_g'][i], sc1, sh1, p['mla_w_down'][j], t, F32)
            q = _norm_matmul(down, 0, MLA_Q_RANK, p['mla_q_a_norm'][j], p['mla_w_uq'][j], t, BF16)
            kv = _norm_matmul(down, MLA_Q_RANK // MLA_KV_RANK, MLA_KV_RANK, p['mla_kv_a_norm'][j],
                              p['mla_w_ukv'][j], t, BF16)
            cos, sin = _mla_rope_tables(t)
            gains = (p['mla_q_gain'][j], p['mla_k_nope_norm'][j], p['mla_kr_gain'][j])
            a = _dispatch_on_score_bound(_mla_score_bound(*gains), _mla_attention, q.reshape(b, t, -1),
                                         kv.reshape(b, t, -1), down.reshape(b, t, -1), cos, sin, *gains)
            x = _oproj_residual(a.reshape(n, -1), p['mla_w_o'][j], x, g1, t)
        x = _mlp_residual(x, p['norm2_g'][i], sc2, sh2, g2, p['mlp_w1'][i], p['mlp_w2'][i], t)
    return x.reshape(b, t, d)


def kernel(x_prompt, x_sample, c_prompt, c_sample, norm1_g, norm2_g, ada_w, ada_b, gqa_w_qkv, gqa_q_norm, gqa_k_norm, gqa_w_o, gla_w_in, gla_w_g1, gla_w_g2, gla_b_g, gla_o_norm, gla_w_o, mla_w_down, mla_q_a_norm, mla_kv_a_norm, mla_w_uq, mla_w_ukv, mla_q_norm, mla_k_nope_norm, mla_k_rope_norm, mla_w_o, mlp_w1, mlp_w2):
    nA, nB, nC = gqa_w_qkv.shape[0], gla_w_in.shape[0], mla_w_down.shape[0]
    gqa = [_prep_gqa(gqa_w_qkv[j], gqa_q_norm[j], gqa_k_norm[j]) for j in range(nA)]
    mla = [_prep_mla(mla_w_down[j], mla_w_uq[j], mla_q_norm[j], mla_k_rope_norm[j]) for j in range(nC)]
    gates = [_prep_gla_gates(gla_w_g1[j], gla_w_g2[j]) for j in range(nB)]
    p = dict(
        norm1_g=norm1_g, norm2_g=norm2_g,
        gqa_w_qkv=[g[0] for g in gqa], gqa_q_norm=[g[1] for g in gqa], gqa_k_norm=[g[2] for g in gqa],
        gqa_w_o=gqa_w_o.astype(BF16),
        gla_w_in=gla_w_in.astype(BF16), gla_w_g1=[g[0] for g in gates], gla_w_g2=[g[1] for g in gates],
        gla_b_g=gla_b_g, gla_o_norm=gla_o_norm, gla_w_o=gla_w_o.astype(BF16),
        mla_w_down=[m[0] for m in mla], mla_w_uq=[m[1] for m in mla], mla_q_gain=[m[2] for m in mla],
        mla_kr_gain=[m[3] for m in mla], mla_q_a_norm=mla_q_a_norm, mla_kv_a_norm=mla_kv_a_norm,
        mla_w_ukv=mla_w_ukv.astype(BF16), mla_k_nope_norm=mla_k_nope_norm, mla_w_o=mla_w_o.astype(BF16),
        mlp_w1=mlp_w1.astype(BF16), mlp_w2=mlp_w2.astype(BF16),
    )
    bp, bs = c_prompt.shape[0], c_sample.shape[0]
    nb_pad = -(-(bp + bs) // 16) * 16
    c_all = jnp.concatenate([c_prompt, c_sample, jnp.zeros((nb_pad - bp - bs, c_prompt.shape[1]), F32)], axis=0)
    mod = _ada_modulation(c_all, ada_w, ada_b)
    y_prompt = _trunk(x_prompt, mod[:, :bp], p)
    y_sample = _trunk(x_sample, mod[:, bp:bp + bs], p)
    return (y_prompt, y_sample)
```

```python
import functools
import math

import numpy as np
import jax
import jax.numpy as jnp
from jax import lax
from jax.experimental import pallas as pl
from jax.experimental.pallas import tpu as pltpu

F32 = jnp.float32
BF16 = jnp.bfloat16

D_MODEL = 2048
DEPTH = 4
GRID_W = 64
ROPE_THETA = 10000.0
NORM_EPS = 1e-6
N_MIXERS = 3

GQA_HEAD_DIM = 128
GQA_Q_HEADS = 16
GQA_KV_HEADS = 4
GQA_GROUP = 4

GLA_HEADS = 4
GLA_DK = 1024
GLA_DV = 2048
GLA_DK_HEAD = 256
GLA_DV_HEAD = 512
GLA_GATE_RANK = 16
GLA_GATE_TAU = 16.0
GLA_BLOCK = 128

MLA_HEADS = 16
MLA_Q_RANK = 512
MLA_KV_RANK = 256
MLA_NOPE_DIM = 128
MLA_ROPE_DIM = 64
MLA_QK_DIM = 192
MLA_V_DIM = 128
MLA_QPAD = 256
MLA_DOWN_PAD = 896

D_FF = 4 * D_MODEL

LANES = 128
VMEM_LIMIT_BYTES = 56 * 1024 * 1024


def _tiles(n_rows, seq):
    tm = min(512, seq)
    assert seq % tm == 0 and n_rows % tm == 0
    return tm


def _cparams(*sem):
    return pltpu.CompilerParams(dimension_semantics=sem, vmem_limit_bytes=VMEM_LIMIT_BYTES)


def _ada_kernel(c_ref, w_ref, b_ref, o_ref):
    c = c_ref[...]
    cond = (c * (1.0 / (1.0 + jnp.exp(-c)))).astype(BF16)
    o_ref[...] = jnp.dot(cond, w_ref[...].astype(BF16), preferred_element_type=F32) + b_ref[...]


def _ada_modulation(c_all, ada_w, ada_b):
    nb, d = c_all.shape
    depth, _, n6 = ada_w.shape
    tn = 1024
    return pl.pallas_call(
        _ada_kernel,
        grid=(depth, n6 // tn),
        in_specs=[
            pl.BlockSpec((nb, d), lambda i, j: (0, 0)),
            pl.BlockSpec((None, d, tn), lambda i, j: (i, 0, j)),
            pl.BlockSpec((None, 1, tn), lambda i, j: (i, 0, j)),
        ],
        out_specs=pl.BlockSpec((None, nb, tn), lambda i, j: (i, 0, j)),
        out_shape=jax.ShapeDtypeStruct((depth, nb, n6), F32),
        compiler_params=_cparams("parallel", "parallel"),
    )(c_all, ada_w, ada_b.reshape(depth, 1, n6))


def _rms(x, g, n):
    ms = jnp.sum(x * x, axis=-1, keepdims=True) * (1.0 / n)
    return x * lax.rsqrt(ms + NORM_EPS) * g


NORM_ROWS = 16


def _store_norm(h_ref, x_ref, gain, shift):
    d = x_ref.shape[-1]
    blocks = [slice(r, r + NORM_ROWS) for r in range(0, x_ref.shape[0], NORM_ROWS)]
    inv = []
    for rows in blocks:
        x = x_ref[rows, :].astype(F32)
        inv.append(lax.rsqrt(jnp.sum(x * x, axis=-1, keepdims=True) * (1.0 / d) + NORM_EPS))
    for rows, inv_r in zip(blocks, inv):
        y = x_ref[rows, :].astype(F32) * inv_r * gain
        h_ref[rows, :] = (y if shift is None else y + shift).astype(BF16)


def _norm_mod_matmul_kernel(x_ref, g_ref, sc_ref, sh_ref, w_ref, o_ref, h_ref):
    @pl.when(pl.program_id(1) == 0)
    def _():
        _store_norm(h_ref, x_ref, g_ref[...] * (1.0 + sc_ref[...]), sh_ref[...])

    o_ref[...] = jnp.dot(h_ref[...], w_ref[...], preferred_element_type=F32).astype(o_ref.dtype)


def _norm_matmul_kernel(x_ref, g_ref, w_ref, o_ref, h_ref):
    @pl.when(pl.program_id(1) == 0)
    def _():
        _store_norm(h_ref, x_ref, g_ref[...], None)

    o_ref[...] = jnp.dot(h_ref[...], w_ref[...], preferred_element_type=F32).astype(o_ref.dtype)


def _norm_mod_matmul(x, g, sc, sh, w, seq, out_dtype, tn=None):
    n, d = x.shape
    nout = w.shape[1]
    tm = _tiles(n, seq)
    tn = nout if tn is None else tn
    tpb = seq // tm
    n_j = nout // tn
    return pl.pallas_call(
        _norm_mod_matmul_kernel,
        grid=(n // tm, n_j),
        in_specs=[
            pl.BlockSpec((tm, d), lambda i, j: (i, 0)),
            pl.BlockSpec((1, d), lambda i, j: (0, 0)),
            pl.BlockSpec((None, 1, d), lambda i, j: (i // tpb, 0, 0)),
            pl.BlockSpec((None, 1, d), lambda i, j: (i // tpb, 0, 0)),
            pl.BlockSpec((d, tn), lambda i, j: (0, _serpentine(i, j, n_j))),
        ],
        out_specs=pl.BlockSpec((tm, tn), lambda i, j: (i, _serpentine(i, j, n_j))),
        out_shape=jax.ShapeDtypeStruct((n, nout), out_dtype),
        scratch_shapes=[pltpu.VMEM((tm, d), BF16)],
        compiler_params=_cparams("parallel", "arbitrary"),
    )(x, g.reshape(1, d), sc, sh, w)


def _norm_matmul(x, col_block, width, g, w, seq, out_dtype):
    n = x.shape[0]
    nout = w.shape[1]
    tm = _tiles(n, seq)
    return pl.pallas_call(
        _norm_matmul_kernel,
        grid=(n // tm, 1),
        in_specs=[
            pl.BlockSpec((tm, width), lambda i, j: (i, col_block)),
            pl.BlockSpec((1, width), lambda i, j: (0, 0)),
            pl.BlockSpec((width, nout), lambda i, j: (0, 0)),
        ],
        out_specs=pl.BlockSpec((tm, nout), lambda i, j: (i, 0)),
        out_shape=jax.ShapeDtypeStruct((n, nout), out_dtype),
        scratch_shapes=[pltpu.VMEM((tm, width), BF16)],
        compiler_params=_cparams("parallel", "arbitrary"),
    )(x, g.reshape(1, width), w)


def _oproj_kernel(a_ref, w_ref, x_ref, gate_ref, o_ref):
    y = jnp.dot(a_ref[...], w_ref[...], preferred_element_type=F32)
    o_ref[...] = x_ref[...] + gate_ref[...] * y


def _oproj_residual(a, w, x, gate, seq):
    n, k = a.shape
    d = w.shape[1]
    tm = _tiles(n, seq)
    tpb = seq // tm
    return pl.pallas_call(
        _oproj_kernel,
        grid=(n // tm,),
        in_specs=[
            pl.BlockSpec((tm, k), lambda i: (i, 0)),
            pl.BlockSpec((k, d), lambda i: (0, 0)),
            pl.BlockSpec((tm, d), lambda i: (i, 0)),
            pl.BlockSpec((None, 1, d), lambda i: (i // tpb, 0, 0)),
        ],
        out_specs=pl.BlockSpec((tm, d), lambda i: (i, 0)),
        out_shape=jax.ShapeDtypeStruct((n, d), F32),
        compiler_params=_cparams("parallel"),
    )(a, w, x, gate)


def _mlp_kernel(x_ref, g_ref, sc_ref, sh_ref, gate_ref, w1_ref, w2_ref, o_ref, h_ref):
    j = pl.program_id(1)
    last = pl.num_programs(1) - 1

    def hidden():
        a = jnp.maximum(jnp.dot(h_ref[...], w1_ref[...], preferred_element_type=F32), 0.0)
        return jnp.dot((a * a).astype(BF16), w2_ref[...], preferred_element_type=F32)

    @pl.when(j == 0)
    def _():
        _store_norm(h_ref, x_ref, g_ref[...] * (1.0 + sc_ref[...]), sh_ref[...])
        o_ref[...] = hidden()

    @pl.when((j > 0) & (j < last))
    def _():
        o_ref[...] += hidden()

    @pl.when(j == last)
    def _():
        o_ref[...] = x_ref[...] + gate_ref[...] * (o_ref[...] + hidden())


def _serpentine(i, j, n_j):
    return jnp.where(lax.rem(i, 2) == 0, j, n_j - 1 - j)


def _mlp_residual(x, g, sc, sh, gate, w1, w2, seq):
    n, d = x.shape
    dff = w1.shape[1]
    tm = _tiles(n, seq)
    tf = min(1024, dff)
    assert dff % tf == 0 and dff // tf >= 2
    n_j = dff // tf
    tpb = seq // tm
    mod_spec = pl.BlockSpec((None, 1, d), lambda i, j: (i // tpb, 0, 0))
    return pl.pallas_call(
        _mlp_kernel,
        grid=(n // tm, n_j),
        in_specs=[
            pl.BlockSpec((tm, d), lambda i, j: (i, 0)),
            pl.BlockSpec((1, d), lambda i, j: (0, 0)),
            mod_spec, mod_spec, mod_spec,
            pl.BlockSpec((d, tf), lambda i, j: (0, _serpentine(i, j, n_j))),
            pl.BlockSpec((tf, d), lambda i, j: (_serpentine(i, j, n_j), 0)),
        ],
        out_specs=pl.BlockSpec((tm, d), lambda i, j: (i, 0)),
        out_shape=jax.ShapeDtypeStruct((n, d), F32),
        scratch_shapes=[pltpu.VMEM((tm, d), BF16)],
        compiler_params=_cparams("parallel", "arbitrary"),
    )(x, g.reshape(1, d), sc, sh, gate, w1, w2)


def _axial_angles(n_tok, rot_dim):
    rows = n_tok // GRID_W
    row = jnp.repeat(jnp.arange(rows, dtype=F32), GRID_W)
    col = jnp.tile(jnp.arange(GRID_W, dtype=F32), rows)
    n_freq = rot_dim // 4
    inv = ROPE_THETA ** (-jnp.arange(n_freq, dtype=F32) / n_freq)
    ang = jnp.concatenate([row[:, None] * inv, col[:, None] * inv], axis=-1)
    return jnp.cos(ang), jnp.sin(ang)


def _gqa_rope_tables(n_tok):
    cos, sin = _axial_angles(n_tok, GQA_HEAD_DIM)
    return jnp.concatenate([cos, cos], -1), jnp.concatenate([-sin, sin], -1)


def _mla_rope_tables(n_tok):
    cos, sin = _axial_angles(n_tok, MLA_ROPE_DIM)
    z = jnp.zeros_like(cos)
    return jnp.concatenate([cos, z, cos, z], -1), jnp.concatenate([-sin, z, sin, z], -1)


_HALF_SPLIT_128 = np.concatenate([np.arange(0, 128, 2), np.arange(1, 128, 2)])


def _mla_rot_layout(base):
    idx = np.zeros(128, np.int32)
    msk = np.zeros(128, np.float32)
    idx[0:32] = base + np.arange(0, 64, 2)
    idx[64:96] = base + np.arange(1, 64, 2)
    msk[0:32] = 1.0
    msk[64:96] = 1.0
    return idx, msk


def _mla_q_layout():
    ridx, rmsk = _mla_rot_layout(MLA_NOPE_DIM)
    idx = np.concatenate([np.arange(MLA_NOPE_DIM), ridx])
    msk = np.concatenate([np.ones(MLA_NOPE_DIM, np.float32), rmsk])
    return idx, msk


def _rope(x, cos, sin):
    return x * cos + pltpu.roll(x, 64, 1) * sin


LOG2E = math.log2(math.e)


MAX_UNSHIFTED_SCORE = 40.0


def _flash_t(q_t, kn_ref, vt_ref, n_keys, tk, bounded):
    m_cols = q_t.shape[1]
    m = jnp.full((1, m_cols), -jnp.inf, F32)
    l8 = jnp.zeros((8, m_cols), F32)
    acc = jnp.zeros((vt_ref.shape[0], m_cols), F32)
    for c in range(n_keys // tk):
        s = jnp.dot(kn_ref[c * tk:(c + 1) * tk, :], q_t, preferred_element_type=F32)
        if bounded:
            p = jnp.exp2(s)
            l8 = l8 + jnp.sum(p.reshape(tk // 8, 8, m_cols), axis=0)
            acc = acc + jnp.dot(vt_ref[:, c * tk:(c + 1) * tk], p.astype(BF16), preferred_element_type=F32)
        else:
            m_new = jnp.maximum(m, jnp.max(s, axis=0, keepdims=True))
            alpha = jnp.exp2(m - m_new)
            p = jnp.exp2(s - m_new)
            l8 = alpha * l8 + jnp.sum(p.reshape(tk // 8, 8, m_cols), axis=0)
            acc = alpha * acc + jnp.dot(vt_ref[:, c * tk:(c + 1) * tk], p.astype(BF16),
                                        preferred_element_type=F32)
            m = m_new
    return acc * (1.0 / jnp.sum(l8, axis=0, keepdims=True))


def _store_vt(vt_ref, v_load, n_keys, blk):
    for c in range(n_keys // blk):
        vt_ref[:, c * blk:(c + 1) * blk] = v_load(c * blk, blk).astype(F32).T.astype(BF16)


def _gqa_attn_kernel(q_ref, k_ref, v_ref, cq_ref, sq_ref, ck_ref, sk_ref, qg_ref, kg_ref, o_ref, kn_ref, vt_ref,
                     *, tk, kprep, bounded):
    n_keys = k_ref.shape[0]
    tq = q_ref.shape[0]

    @pl.when(pl.program_id(2) == 0)
    def _():
        def prep(c, _):
            rows = pl.ds(pl.multiple_of(c * kprep, kprep), kprep)
            k = _rms(k_ref[rows, :].astype(F32), kg_ref[...], GQA_HEAD_DIM)
            kn_ref[rows, :] = _rope(k, ck_ref[rows, :], sk_ref[rows, :]).astype(BF16)
            return 0
        lax.fori_loop(0, n_keys // kprep, prep, 0)
        _store_vt(vt_ref, lambda r0, nr: v_ref[r0:r0 + nr, :], n_keys, kprep)

    scale = GQA_HEAD_DIM ** -0.5 * LOG2E
    qs = []
    for h in range(GQA_GROUP):
        q = _rms(q_ref[:, h * GQA_HEAD_DIM:(h + 1) * GQA_HEAD_DIM].astype(F32), qg_ref[...], GQA_HEAD_DIM)
        qs.append((_rope(q, cq_ref[...], sq_ref[...]) * scale).T.astype(BF16))
    q_t = jnp.concatenate(qs, axis=1)
    out_t = _flash_t(q_t, kn_ref, vt_ref, n_keys, tk, bounded)
    for h in range(GQA_GROUP):
        o_ref[:, h * GQA_HEAD_DIM:(h + 1) * GQA_HEAD_DIM] = out_t[:, h * tq:(h + 1) * tq].T.astype(o_ref.dtype)


def _dispatch_on_score_bound(bound, attention, *args):
    return lax.cond(bound <= MAX_UNSHIFTED_SCORE,
                    functools.partial(attention, bounded=True),
                    functools.partial(attention, bounded=False), *args)


def _gqa_score_bound(q_gain, k_gain):
    return (GQA_HEAD_DIM * jnp.max(jnp.abs(q_gain)) * jnp.max(jnp.abs(k_gain))) * (GQA_HEAD_DIM ** -0.5 * LOG2E)


def _mla_score_bound(q_gain, kn_gain, kr_gain):
    q_norm = math.sqrt(MLA_QK_DIM) * jnp.max(jnp.abs(q_gain))
    k_norm = jnp.sqrt(MLA_NOPE_DIM * jnp.max(kn_gain * kn_gain) + MLA_ROPE_DIM * jnp.max(kr_gain * kr_gain))
    return q_norm * k_norm * (MLA_QK_DIM ** -0.5 * LOG2E)


def _gqa_attention(qkv, cos, sin, q_gain, k_gain, *, bounded):
    b, t, _ = qkv.shape
    tq = min(512 if t <= 2048 else 256, t)
    tk = min(1024 if t <= 2048 else 2048, t)
    kprep = min(512, t)
    gw = GQA_GROUP * GQA_HEAD_DIM
    k_col0 = GQA_Q_HEADS
    v_col0 = GQA_Q_HEADS + GQA_KV_HEADS
    tab_q = pl.BlockSpec((tq, LANES), lambda bi, g, qi: (qi, 0))
    tab_k = pl.BlockSpec((t, LANES), lambda bi, g, qi: (0, 0), pipeline_mode=pl.Buffered(1))
    gain = pl.BlockSpec((1, LANES), lambda bi, g, qi: (0, 0))
    return pl.pallas_call(
        functools.partial(_gqa_attn_kernel, tk=tk, kprep=kprep, bounded=bounded),
        grid=(b, GQA_KV_HEADS, t // tq),
        in_specs=[
            pl.BlockSpec((None, tq, gw), lambda bi, g, qi: (bi, qi, g)),
            pl.BlockSpec((None, t, GQA_HEAD_DIM), lambda bi, g, qi: (bi, 0, k_col0 + g)),
            pl.BlockSpec((None, t, GQA_HEAD_DIM), lambda bi, g, qi: (bi, 0, v_col0 + g)),
            tab_q, tab_q, tab_k, tab_k, gain, gain,
        ],
        out_specs=pl.BlockSpec((None, tq, gw), lambda bi, g, qi: (bi, qi, g)),
        out_shape=jax.ShapeDtypeStruct((b, t, GQA_Q_HEADS * GQA_HEAD_DIM), BF16),
        scratch_shapes=[pltpu.VMEM((t, GQA_HEAD_DIM), BF16), pltpu.VMEM((GQA_HEAD_DIM, t), BF16)],
        compiler_params=_cparams("parallel", "parallel", "arbitrary"),
    )(qkv, qkv, qkv, cos, sin, cos, sin, q_gain.reshape(1, LANES), k_gain.reshape(1, LANES))


def _mla_attn_kernel(q_ref, kv_ref, kr_ref, cq_ref, sq_ref, ck_ref, sk_ref, qg_ref, kng_ref, krg_ref,
                     o_ref, kn_ref, vt_ref, *, tk, kprep, bounded):
    n_keys = kv_ref.shape[0]

    @pl.when(pl.program_id(2) == 0)
    def _():
        def prep(c, _):
            rows = pl.ds(pl.multiple_of(c * kprep, kprep), kprep)
            kn = _rms(kv_ref[rows, 0:MLA_NOPE_DIM].astype(F32), kng_ref[...], MLA_NOPE_DIM)
            kn_ref[rows, 0:MLA_NOPE_DIM] = kn.astype(BF16)
            kr = _rms(kr_ref[rows, :].astype(F32), krg_ref[...], MLA_ROPE_DIM)
            kn_ref[rows, MLA_NOPE_DIM:MLA_QPAD] = _rope(kr, ck_ref[rows, :], sk_ref[rows, :]).astype(BF16)
            return 0
        lax.fori_loop(0, n_keys // kprep, prep, 0)
        _store_vt(vt_ref, lambda r0, nr: kv_ref[r0:r0 + nr, MLA_NOPE_DIM:MLA_QPAD], n_keys, kprep)

    scale = MLA_QK_DIM ** -0.5 * LOG2E
    q = q_ref[...].astype(F32)
    g = qg_ref[...]
    inv = lax.rsqrt(jnp.sum(q * q, axis=-1, keepdims=True) * (1.0 / MLA_QK_DIM) + NORM_EPS)
    q_nope = q[:, 0:MLA_NOPE_DIM] * inv * g[:, 0:MLA_NOPE_DIM]
    q_rot = _rope(q[:, MLA_NOPE_DIM:MLA_QPAD] * inv * g[:, MLA_NOPE_DIM:MLA_QPAD], cq_ref[...], sq_ref[...])
    q_t = jnp.concatenate([(q_nope * scale).T, (q_rot * scale).T], axis=0).astype(BF16)
    out_t = _flash_t(q_t, kn_ref, vt_ref, n_keys, tk, bounded)
    o_ref[...] = out_t.T.astype(o_ref.dtype)


def _mla_attention(q, kv, down, cos, sin, q_gain, kn_gain, kr_gain, *, bounded):
    b, t, _ = q.shape
    tq = min(2048 if t <= 2048 else 1024, t)
    tk = min(1024, t)
    kprep = min(512, t)
    kr_col = (MLA_Q_RANK + MLA_KV_RANK) // LANES
    tab_q = pl.BlockSpec((tq, LANES), lambda bi, h, qi: (qi, 0))
    tab_k = pl.BlockSpec((t, LANES), lambda bi, h, qi: (0, 0), pipeline_mode=pl.Buffered(1))
    return pl.pallas_call(
        functools.partial(_mla_attn_kernel, tk=tk, kprep=kprep, bounded=bounded),
        grid=(b, MLA_HEADS, t // tq),
        in_specs=[
            pl.BlockSpec((None, tq, MLA_QPAD), lambda bi, h, qi: (bi, qi, h)),
            pl.BlockSpec((None, t, MLA_QPAD), lambda bi, h, qi: (bi, 0, h)),
            pl.BlockSpec((None, t, LANES), lambda bi, h, qi: (bi, 0, kr_col)),
            tab_q, tab_q, tab_k, tab_k,
            pl.BlockSpec((1, MLA_QPAD), lambda bi, h, qi: (0, 0)),
            pl.BlockSpec((1, LANES), lambda bi, h, qi: (0, 0)),
            pl.BlockSpec((1, LANES), lambda bi, h, qi: (0, 0)),
        ],
        out_specs=pl.BlockSpec((None, tq, MLA_V_DIM), lambda bi, h, qi: (bi, qi, h)),
        out_shape=jax.ShapeDtypeStruct((b, t, MLA_HEADS * MLA_V_DIM), BF16),
        scratch_shapes=[pltpu.VMEM((t, MLA_QPAD), BF16), pltpu.VMEM((MLA_V_DIM, t), BF16)],
        compiler_params=_cparams("parallel", "parallel", "arbitrary"),
    )(q, kv, down, cos, sin, cos, sin, q_gain.reshape(1, MLA_QPAD), kn_gain.reshape(1, LANES),
      kr_gain.reshape(1, LANES))


def _split3(x):
    hi = x.astype(BF16)
    r1 = x - hi.astype(F32)
    mid = r1.astype(BF16)
    lo = (r1 - mid.astype(F32)).astype(BF16)
    return hi, mid, lo


def _gla_kernel(*refs, reverse, n_chunks):
    if reverse:
        (q_ref, k_ref, v_ref, low_ref, wg_ref, bg_ref, of_ref, r_ref, on_ref, o_ref, s_ref) = refs
    else:
        (q_ref, k_ref, v_ref, low_ref, wg_ref, bg_ref, o_ref, s_ref) = refs
    L = GLA_BLOCK

    @pl.when(pl.program_id(2) == 0)
    def _():
        s_ref[...] = jnp.zeros_like(s_ref)

    row = lax.broadcasted_iota(jnp.int32, (L, L), 0)
    col = lax.broadcasted_iota(jnp.int32, (L, L), 1)
    keep = (col >= row) if reverse else (col <= row)
    tri = jnp.where(keep, 1.0, 0.0).astype(BF16)
    tn_dims = (((0,), (0,)), ((), ()))
    nt_dims = (((1,), (1,)), ((), ()))
    mid = L // 2 if reverse else L // 2 - 1
    edge = 0 if reverse else L - 1

    order = list(range(n_chunks - 1, -1, -1) if reverse else range(n_chunks))
    rows = {c: slice(c * L, (c + 1) * L) for c in order}

    z = jnp.dot(low_ref[...].astype(BF16), wg_ref[...], preferred_element_type=F32) + bg_ref[...]
    cum = {}
    for c in order:
        zc = z[rows[c], :]
        la = (jnp.minimum(zc, 0.0) - jnp.log1p(jnp.exp(-jnp.abs(zc)))) * (1.0 / GLA_GATE_TAU)
        cum[c] = sum(jnp.dot(tri, p, preferred_element_type=F32) for p in _split3(la))
    qd, qs, kd, ke, de = {}, {}, {}, {}, {}
    for c in order:
        ref_row = cum[c][mid:mid + 1, :]
        tot_row = cum[c][edge:edge + 1, :]
        q = q_ref[rows[c], :].astype(F32) * (GLA_DK_HEAD ** -0.5)
        k = k_ref[rows[c], :].astype(F32)
        q_mid = q * jnp.exp(cum[c] - ref_row)
        qd[c] = q_mid.astype(BF16)
        qs[c] = (q_mid * jnp.exp(ref_row)).astype(BF16)
        kd[c] = (k * jnp.exp(ref_row - cum[c])).astype(BF16)
        ke[c] = (k * jnp.exp(tot_row - cum[c])).astype(BF16)
        col_de = jnp.broadcast_to(jnp.exp(tot_row), (LANES, GLA_DK_HEAD)).T
        de[c] = jnp.concatenate([col_de] * (GLA_DV_HEAD // LANES), axis=1)
    a = {c: jnp.where(keep, lax.dot_general(qd[c], kd[c], nt_dims, preferred_element_type=F32), 0.0).astype(BF16)
         for c in order}
    o_intra = {c: jnp.dot(a[c], v_ref[rows[c], :], preferred_element_type=F32) for c in order}
    kv = {c: lax.dot_general(ke[c], v_ref[rows[c], :], tn_dims, preferred_element_type=F32) for c in order}

    s = s_ref[...]
    for c in order:
        o = o_intra[c] + jnp.dot(qs[c], s.astype(BF16), preferred_element_type=F32)
        s = de[c] * s + kv[c]
        if reverse:
            o = o + of_ref[rows[c], :]
            o = _rms(o, on_ref[...], GLA_DV_HEAD)
            r = r_ref[rows[c], :].astype(F32)
            o = o * (r * (1.0 / (1.0 + jnp.exp(-r))))
        o_ref[rows[c], :] = o.astype(o_ref.dtype)
    s_ref[...] = s


def _gla_direction(proj, low, wg, bg, o_fwd, o_norm, reverse):
    b, t, _ = proj.shape
    tb = min(1024, t)
    assert t % tb == 0 and tb % GLA_BLOCK == 0
    nblk = t // tb
    blk = (lambda i: nblk - 1 - i) if reverse else (lambda i: i)
    k_col0 = GLA_DK // GLA_DK_HEAD
    v_col0 = 2 * GLA_DK // GLA_DV_HEAD
    r_col0 = (2 * GLA_DK + GLA_DV) // GLA_DV_HEAD
    in_specs = [
        pl.BlockSpec((None, tb, GLA_DK_HEAD), lambda bi, h, i: (bi, blk(i), h)),
        pl.BlockSpec((None, tb, GLA_DK_HEAD), lambda bi, h, i: (bi, blk(i), k_col0 + h)),
        pl.BlockSpec((None, tb, GLA_DV_HEAD), lambda bi, h, i: (bi, blk(i), v_col0 + h)),
        pl.BlockSpec((None, tb, LANES), lambda bi, h, i: (bi, blk(i), 0)),
        pl.BlockSpec((LANES, GLA_DK_HEAD), lambda bi, h, i: (0, h)),
        pl.BlockSpec((1, GLA_DK_HEAD), lambda bi, h, i: (0, h)),
    ]
    args = [proj, proj, proj, low, wg, bg]
    if reverse:
        in_specs += [
            pl.BlockSpec((None, tb, GLA_DV_HEAD), lambda bi, h, i: (bi, blk(i), h)),
            pl.BlockSpec((None, tb, GLA_DV_HEAD), lambda bi, h, i: (bi, blk(i), r_col0 + h)),
            pl.BlockSpec((1, GLA_DV_HEAD), lambda bi, h, i: (0, 0)),
        ]
        args += [o_fwd, proj, o_norm.reshape(1, GLA_DV_HEAD)]
    return pl.pallas_call(
        functools.partial(_gla_kernel, reverse=reverse, n_chunks=tb // GLA_BLOCK),
        grid=(b, GLA_HEADS, nblk),
        in_specs=in_specs,
        out_specs=pl.BlockSpec((None, tb, GLA_DV_HEAD), lambda bi, h, i: (bi, blk(i), h)),
        out_shape=jax.ShapeDtypeStruct((b, t, GLA_DV), BF16 if reverse else F32),
        scratch_shapes=[pltpu.VMEM((GLA_DK_HEAD, GLA_DV_HEAD), F32)],
        compiler_params=_cparams("parallel", "parallel", "arbitrary"),
    )(*args)


def _prep_gqa(w_qkv, q_norm, k_norm):
    n_rot = GQA_Q_HEADS + GQA_KV_HEADS
    cols = (np.arange(n_rot)[:, None] * GQA_HEAD_DIM + _HALF_SPLIT_128[None, :]).reshape(-1)
    cols = np.concatenate([cols, np.arange(n_rot * GQA_HEAD_DIM, w_qkv.shape[1])])
    return w_qkv[:, cols].astype(BF16), q_norm[_HALF_SPLIT_128], k_norm[_HALF_SPLIT_128]


def _prep_mla(w_down, w_uq, q_norm, k_rope_norm):
    qidx, qmsk = _mla_q_layout()
    cols = (np.arange(MLA_HEADS)[:, None] * MLA_QK_DIM + qidx[None, :]).reshape(-1)
    w_uq_p = (w_uq[:, cols] * jnp.asarray(np.tile(qmsk, MLA_HEADS))).astype(BF16)
    q_gain = q_norm[qidx] * jnp.asarray(qmsk)
    ridx, rmsk = _mla_rot_layout(0)
    kr0 = MLA_Q_RANK + MLA_KV_RANK
    w_kr = w_down[:, kr0 + ridx] * jnp.asarray(rmsk)
    w_down_p = jnp.concatenate([w_down[:, :kr0], w_kr], axis=1).astype(BF16)
    kr_gain = k_rope_norm[ridx] * jnp.asarray(rmsk)
    return w_down_p, w_uq_p, q_gain, kr_gain


def _prep_gla_gates(w_g1, w_g2):
    r = GLA_GATE_RANK
    d = w_g1.shape[1]
    w1 = jnp.zeros((d, LANES), F32).at[:, 0:r].set(w_g1[0]).at[:, r:2 * r].set(w_g1[1])
    w2 = jnp.zeros((2, LANES, GLA_DK), F32).at[0, 0:r].set(w_g2[0]).at[1, r:2 * r].set(w_g2[1])
    return w1.astype(BF16), w2.astype(BF16)


def _trunk(x3, mod, p):
    b, t, d = x3.shape
    n = b * t
    x = x3.reshape(n, d)
    for i in range(DEPTH):
        m = mod[i].reshape(b, 6, 1, d)
        sh1, sc1, g1, sh2, sc2, g2 = (m[:, s] for s in range(6))
        kind, j = i % N_MIXERS, i // N_MIXERS
        if kind == 0:
            qkv = _norm_mod_matmul(x, p['norm1_g'][i], sc1, sh1, p['gqa_w_qkv'][j], t, BF16)
            cos, sin = _gqa_rope_tables(t)
            a = _dispatch_on_score_bound(_gqa_score_bound(p['gqa_q_norm'][j], p['gqa_k_norm'][j]), _gqa_attention,
                                         qkv.reshape(b, t, -1), cos, sin, p['gqa_q_norm'][j], p['gqa_k_norm'][j])
            x = _oproj_residual(a.reshape(n, -1), p['gqa_w_o'][j], x, g1, t)
        elif kind == 1:
            proj = _norm_mod_matmul(x, p['norm1_g'][i], sc1, sh1, p['gla_w_in'][j], t, BF16, tn=2048)
            low = _norm_mod_matmul(x, p['norm1_g'][i], sc1, sh1, p['gla_w_g1'][j], t, F32)
            proj3, low3 = proj.reshape(b, t, -1), low.reshape(b, t, -1)
            bg = p['gla_b_g'][j]
            o_f = _gla_direction(proj3, low3, p['gla_w_g2'][j][0], bg[0:1], None, None, reverse=False)
            a = _gla_direction(proj3, low3, p['gla_w_g2'][j][1], bg[1:2], o_f, p['gla_o_norm'][j], reverse=True)
            x = _oproj_residual(a.reshape(n, -1), p['gla_w_o'][j], x, g1, t)
        else:
            down = _norm_mod_matmul(x, p['norm1_g'][i], sc1, sh1, p['mla_w_down'][j], t, F32)
            q = _norm_matmul(down, 0, MLA_Q_RANK, p['mla_q_a_norm'][j], p['mla_w_uq'][j], t, BF16)
            kv = _norm_matmul(down, MLA_Q_RANK // MLA_KV_RANK, MLA_KV_RANK, p['mla_kv_a_norm'][j],
                              p['mla_w_ukv'][j], t, BF16)
            cos, sin = _mla_rope_tables(t)
            gains = (p['mla_q_gain'][j], p['mla_k_nope_norm'][j], p['mla_kr_gain'][j])
            a = _dispatch_on_score_bound(_mla_score_bound(*gains), _mla_attention, q.reshape(b, t, -1),
                                         kv.reshape(b, t, -1), down.reshape(b, t, -1), cos, sin, *gains)
            x = _oproj_residual(a.reshape(n, -1), p['mla_w_o'][j], x, g1, t)
        x = _mlp_residual(x, p['norm2_g'][i], sc2, sh2, g2, p['mlp_w1'][i], p['mlp_w2'][i], t)
    return x.reshape(b, t, d)


def kernel(x_prompt, x_sample, c_prompt, c_sample, norm1_g, norm2_g, ada_w, ada_b, gqa_w_qkv, gqa_q_norm, gqa_k_norm, gqa_w_o, gla_w_in, gla_w_g1, gla_w_g2, gla_b_g, gla_o_norm, gla_w_o, mla_w_down, mla_q_a_norm, mla_kv_a_norm, mla_w_uq, mla_w_ukv, mla_q_norm, mla_k_nope_norm, mla_k_rope_norm, mla_w_o, mlp_w1, mlp_w2):
    nA, nB, nC = gqa_w_qkv.shape[0], gla_w_in.shape[0], mla_w_down.shape[0]
    gqa = [_prep_gqa(gqa_w_qkv[j], gqa_q_norm[j], gqa_k_norm[j]) for j in range(nA)]
    mla = [_prep_mla(mla_w_down[j], mla_w_uq[j], mla_q_norm[j], mla_k_rope_norm[j]) for j in range(nC)]
    gates = [_prep_gla_gates(gla_w_g1[j], gla_w_g2[j]) for j in range(nB)]
    p = dict(
        norm1_g=norm1_g, norm2_g=norm2_g,
        gqa_w_qkv=[g[0] for g in gqa], gqa_q_norm=[g[1] for g in gqa], gqa_k_norm=[g[2] for g in gqa],
        gqa_w_o=gqa_w_o.astype(BF16),
        gla_w_in=gla_w_in.astype(BF16), gla_w_g1=[g[0] for g in gates], gla_w_g2=[g[1] for g in gates],
        gla_b_g=gla_b_g, gla_o_norm=gla_o_norm, gla_w_o=gla_w_o.astype(BF16),
        mla_w_down=[m[0] for m in mla], mla_w_uq=[m[1] for m in mla], mla_q_gain=[m[2] for m in mla],
        mla_kr_gain=[m[3] for m in mla], mla_q_a_norm=mla_q_a_norm, mla_kv_a_norm=mla_kv_a_norm,
        mla_w_ukv=mla_w_ukv.astype(BF16), mla_k_nope_norm=mla_k_nope_norm, mla_w_o=mla_w_o.astype(BF16),
        mlp_w1=mlp_w1.astype(BF16), mlp_w2=mlp_w2.astype(BF16),
    )
    bp, bs = c_prompt.shape[0], c_sample.shape[0]
    nb_pad = -(-(bp + bs) // 16) * 16
    c_all = jnp.concatenate([c_prompt, c_sample, jnp.zeros((nb_pad - bp - bs, c_prompt.shape[1]), F32)], axis=0)
    mod = _ada_modulation(c_all, ada_w, ada_b)
    y_prompt = _trunk(x_prompt, mod[:, :bp], p)
    y_sample = _trunk(x_sample, mod[:, bp:bp + bs], p)
    return (y_prompt, y_sample)
```

```python
import functools
import math

import numpy as np
import jax
import jax.numpy as jnp
from jax import lax
from jax.experimental import pallas as pl
from jax.experimental.pallas import tpu as pltpu

F32 = jnp.float32
BF16 = jnp.bfloat16

D_MODEL = 2048
DEPTH = 4
GRID_W = 64
ROPE_THETA = 10000.0
NORM_EPS = 1e-6
N_MIXERS = 3

GQA_HEAD_DIM = 128
GQA_Q_HEADS = 16
GQA_KV_HEADS = 4
GQA_GROUP = 4

GLA_HEADS = 4
GLA_DK = 1024
GLA_DV = 2048
GLA_DK_HEAD = 256
GLA_DV_HEAD = 512
GLA_GATE_RANK = 16
GLA_GATE_TAU = 16.0
GLA_BLOCK = 128

MLA_HEADS = 16
MLA_Q_RANK = 512
MLA_KV_RANK = 256
MLA_NOPE_DIM = 128
MLA_ROPE_DIM = 64
MLA_QK_DIM = 192
MLA_V_DIM = 128
MLA_QPAD = 256
MLA_DOWN_PAD = 896

D_FF = 4 * D_MODEL

LANES = 128
VMEM_LIMIT_BYTES = 56 * 1024 * 1024


def _tiles(n_rows, seq):
    tm = min(512, seq)
    assert seq % tm == 0 and n_rows % tm == 0
    return tm


def _cparams(*sem):
    return pltpu.CompilerParams(dimension_semantics=sem, vmem_limit_bytes=VMEM_LIMIT_BYTES)


def _ada_kernel(c_ref, w_ref, b_ref, o_ref):
    c = c_ref[...]
    cond = (c * (1.0 / (1.0 + jnp.exp(-c)))).astype(BF16)
    o_ref[...] = jnp.dot(cond, w_ref[...].astype(BF16), preferred_element_type=F32) + b_ref[...]


def _ada_modulation(c_all, ada_w, ada_b):
    nb, d = c_all.shape
    depth, _, n6 = ada_w.shape
    tn = 1024
    return pl.pallas_call(
        _ada_kernel,
        grid=(depth, n6 // tn),
        in_specs=[
            pl.BlockSpec((nb, d), lambda i, j: (0, 0)),
            pl.BlockSpec((None, d, tn), lambda i, j: (i, 0, j)),
            pl.BlockSpec((None, 1, tn), lambda i, j: (i, 0, j)),
        ],
        out_specs=pl.BlockSpec((None, nb, tn), lambda i, j: (i, 0, j)),
        out_shape=jax.ShapeDtypeStruct((depth, nb, n6), F32),
        compiler_params=_cparams("parallel", "parallel"),
    )(c_all, ada_w, ada_b.reshape(depth, 1, n6))


def _rms(x, g, n):
    ms = jnp.sum(x * x, axis=-1, keepdims=True) * (1.0 / n)
    return x * lax.rsqrt(ms + NORM_EPS) * g


NORM_ROWS = 16


def _store_norm(h_ref, x_ref, gain, shift):
    d = x_ref.shape[-1]
    blocks = [slice(r, r + NORM_ROWS) for r in range(0, x_ref.shape[0], NORM_ROWS)]
    inv = []
    for rows in blocks:
        x = x_ref[rows, :].astype(F32)
        inv.append(lax.rsqrt(jnp.sum(x * x, axis=-1, keepdims=True) * (1.0 / d) + NORM_EPS))
    for rows, inv_r in zip(blocks, inv):
        y = x_ref[rows, :].astype(F32) * inv_r * gain
        h_ref[rows, :] = (y if shift is None else y + shift).astype(BF16)


def _norm_mod_matmul_kernel(x_ref, g_ref, sc_ref, sh_ref, w_ref, o_ref, h_ref):
    @pl.when(pl.program_id(1) == 0)
    def _():
        _store_norm(h_ref, x_ref, g_ref[...] * (1.0 + sc_ref[...]), sh_ref[...])

    o_ref[...] = jnp.dot(h_ref[...], w_ref[...], preferred_element_type=F32).astype(o_ref.dtype)


def _norm_matmul_kernel(x_ref, g_ref, w_ref, o_ref, h_ref):
    @pl.when(pl.program_id(1) == 0)
    def _():
        _store_norm(h_ref, x_ref, g_ref[...], None)

    o_ref[...] = jnp.dot(h_ref[...], w_ref[...], preferred_element_type=F32).astype(o_ref.dtype)


def _norm_mod_matmul(x, g, sc, sh, w, seq, out_dtype, tn=None):
    n, d = x.shape
    nout = w.shape[1]
    tm = _tiles(n, seq)
    tn = nout if tn is None else tn
    tpb = seq // tm
    n_j = nout // tn
    return pl.pallas_call(
        _norm_mod_matmul_kernel,
        grid=(n // tm, n_j),
        in_specs=[
            pl.BlockSpec((tm, d), lambda i, j: (i, 0)),
            pl.BlockSpec((1, d), lambda i, j: (0, 0)),
            pl.BlockSpec((None, 1, d), lambda i, j: (i // tpb, 0, 0)),
            pl.BlockSpec((None, 1, d), lambda i, j: (i // tpb, 0, 0)),
            pl.BlockSpec((d, tn), lambda i, j: (0, _serpentine(i, j, n_j))),
        ],
        out_specs=pl.BlockSpec((tm, tn), lambda i, j: (i, _serpentine(i, j, n_j))),
        out_shape=jax.ShapeDtypeStruct((n, nout), out_dtype),
        scratch_shapes=[pltpu.VMEM((tm, d), BF16)],
        compiler_params=_cparams("parallel", "arbitrary"),
    )(x, g.reshape(1, d), sc, sh, w)


def _norm_matmul(x, col_block, width, g, w, seq, out_dtype):
    n = x.shape[0]
    nout = w.shape[1]
    tm = _tiles(n, seq)
    return pl.pallas_call(
        _norm_matmul_kernel,
        grid=(n // tm, 1),
        in_specs=[
            pl.BlockSpec((tm, width), lambda i, j: (i, col_block)),
            pl.BlockSpec((1, width), lambda i, j: (0, 0)),
            pl.BlockSpec((width, nout), lambda i, j: (0, 0)),
        ],
        out_specs=pl.BlockSpec((tm, nout), lambda i, j: (i, 0)),
        out_shape=jax.ShapeDtypeStruct((n, nout), out_dtype),
        scratch_shapes=[pltpu.VMEM((tm, width), BF16)],
        compiler_params=_cparams("parallel", "arbitrary"),
    )(x, g.reshape(1, width), w)


def _oproj_kernel(a_ref, w_ref, x_ref, gate_ref, o_ref):
    y = jnp.dot(a_ref[...], w_ref[...], preferred_element_type=F32)
    o_ref[...] = x_ref[...] + gate_ref[...] * y


def _oproj_residual(a, w, x, gate, seq):
    n, k = a.shape
    d = w.shape[1]
    tm = _tiles(n, seq)
    tpb = seq // tm
    return pl.pallas_call(
        _oproj_kernel,
        grid=(n // tm,),
        in_specs=[
            pl.BlockSpec((tm, k), lambda i: (i, 0)),
            pl.BlockSpec((k, d), lambda i: (0, 0)),
            pl.BlockSpec((tm, d), lambda i: (i, 0)),
            pl.BlockSpec((None, 1, d), lambda i: (i // tpb, 0, 0)),
        ],
        out_specs=pl.BlockSpec((tm, d), lambda i: (i, 0)),
        out_shape=jax.ShapeDtypeStruct((n, d), F32),
        compiler_params=_cparams("parallel"),
    )(a, w, x, gate)


def _mlp_kernel(x_ref, g_ref, sc_ref, sh_ref, gate_ref, w1_ref, w2_ref, o_ref, h_ref):
    j = pl.program_id(1)
    last = pl.num_programs(1) - 1

    def hidden():
        a = jnp.maximum(jnp.dot(h_ref[...], w1_ref[...], preferred_element_type=F32), 0.0)
        return jnp.dot((a * a).astype(BF16), w2_ref[...], preferred_element_type=F32)

    @pl.when(j == 0)
    def _():
        _store_norm(h_ref, x_ref, g_ref[...] * (1.0 + sc_ref[...]), sh_ref[...])
        o_ref[...] = hidden()

    @pl.when((j > 0) & (j < last))
    def _():
        o_ref[...] += hidden()

    @pl.when(j == last)
    def _():
        o_ref[...] = x_ref[...] + gate_ref[...] * (o_ref[...] + hidden())


def _serpentine(i, j, n_j):
    return jnp.where(lax.rem(i, 2) == 0, j, n_j - 1 - j)


def _mlp_residual(x, g, sc, sh, gate, w1, w2, seq):
    n, d = x.shape
    dff = w1.shape[1]
    tm = _tiles(n, seq)
    tf = min(1024, dff)
    assert dff % tf == 0 and dff // tf >= 2
    n_j = dff // tf
    tpb = seq // tm
    mod_spec = pl.BlockSpec((None, 1, d), lambda i, j: (i // tpb, 0, 0))
    return pl.pallas_call(
        _mlp_kernel,
        grid=(n // tm, n_j),
        in_specs=[
            pl.BlockSpec((tm, d), lambda i, j: (i, 0)),
            pl.BlockSpec((1, d), lambda i, j: (0, 0)),
            mod_spec, mod_spec, mod_spec,
            pl.BlockSpec((d, tf), lambda i, j: (0, _serpentine(i, j, n_j))),
            pl.BlockSpec((tf, d), lambda i, j: (_serpentine(i, j, n_j), 0)),
        ],
        out_specs=pl.BlockSpec((tm, d), lambda i, j: (i, 0)),
        out_shape=jax.ShapeDtypeStruct((n, d), F32),
        scratch_shapes=[pltpu.VMEM((tm, d), BF16)],
        compiler_params=_cparams("parallel", "arbitrary"),
    )(x, g.reshape(1, d), sc, sh, gate, w1, w2)


def _axial_angles(n_tok, rot_dim):
    rows = n_tok // GRID_W
    row = jnp.repeat(jnp.arange(rows, dtype=F32), GRID_W)
    col = jnp.tile(jnp.arange(GRID_W, dtype=F32), rows)
    n_freq = rot_dim // 4
    inv = ROPE_THETA ** (-jnp.arange(n_freq, dtype=F32) / n_freq)
    ang = jnp.concatenate([row[:, None] * inv, col[:, None] * inv], axis=-1)
    return jnp.cos(ang), jnp.sin(ang)


def _gqa_rope_tables(n_tok):
    cos, sin = _axial_angles(n_tok, GQA_HEAD_DIM)
    return jnp.concatenate([cos, cos], -1), jnp.concatenate([-sin, sin], -1)


def _mla_rope_tables(n_tok):
    cos, sin = _axial_angles(n_tok, MLA_ROPE_DIM)
    z = jnp.zeros_like(cos)
    return jnp.concatenate([cos, z, cos, z], -1), jnp.concatenate([-sin, z, sin, z], -1)


_HALF_SPLIT_128 = np.concatenate([np.arange(0, 128, 2), np.arange(1, 128, 2)])


def _mla_rot_layout(base):
    idx = np.zeros(128, np.int32)
    msk = np.zeros(128, np.float32)
    idx[0:32] = base + np.arange(0, 64, 2)
    idx[64:96] = base + np.arange(1, 64, 2)
    msk[0:32] = 1.0
    msk[64:96] = 1.0
    return idx, msk


def _mla_q_layout():
    ridx, rmsk = _mla_rot_layout(MLA_NOPE_DIM)
    idx = np.concatenate([np.arange(MLA_NOPE_DIM), ridx])
    msk = np.concatenate([np.ones(MLA_NOPE_DIM, np.float32), rmsk])
    return idx, msk


def _rope(x, cos, sin):
    return x * cos + pltpu.roll(x, 64, 1) * sin


LOG2E = math.log2(math.e)


MAX_UNSHIFTED_SCORE = 40.0


def _flash_t(q_t, kn_ref, vt_ref, n_keys, tk, bounded):
    m_cols = q_t.shape[1]
    m = jnp.full((1, m_cols), -jnp.inf, F32)
    l8 = jnp.zeros((8, m_cols), F32)
    acc = jnp.zeros((vt_ref.shape[0], m_cols), F32)
    for c in range(n_keys // tk):
        s = jnp.dot(kn_ref[c * tk:(c + 1) * tk, :], q_t, preferred_element_type=F32)
        if bounded:
            p = jnp.exp2(s)
            l8 = l8 + jnp.sum(p.reshape(tk // 8, 8, m_cols), axis=0)
            acc = acc + jnp.dot(vt_ref[:, c * tk:(c + 1) * tk], p.astype(BF16), preferred_element_type=F32)
        else:
            m_new = jnp.maximum(m, jnp.max(s, axis=0, keepdims=True))
            alpha = jnp.exp2(m - m_new)
            p = jnp.exp2(s - m_new)
            l8 = alpha * l8 + jnp.sum(p.reshape(tk // 8, 8, m_cols), axis=0)
            acc = alpha * acc + jnp.dot(vt_ref[:, c * tk:(c + 1) * tk], p.astype(BF16),
                                        preferred_element_type=F32)
            m = m_new
    return acc * (1.0 / jnp.sum(l8, axis=0, keepdims=True))


def _store_vt(vt_ref, v_load, n_keys, blk):
    for c in range(n_keys // blk):
        vt_ref[:, c * blk:(c + 1) * blk] = v_load(c * blk, blk).astype(F32).T.astype(BF16)


def _gqa_attn_kernel(q_ref, k_ref, v_ref, cq_ref, sq_ref, ck_ref, sk_ref, qg_ref, kg_ref, o_ref, kn_ref, vt_ref,
                     *, tk, kprep, bounded):
    n_keys = k_ref.shape[0]
    tq = q_ref.shape[0]

    @pl.when(pl.program_id(2) == 0)
    def _():
        def prep(c, _):
            rows = pl.ds(pl.multiple_of(c * kprep, kprep), kprep)
            k = _rms(k_ref[rows, :].astype(F32), kg_ref[...], GQA_HEAD_DIM)
            kn_ref[rows, :] = _rope(k, ck_ref[rows, :], sk_ref[rows, :]).astype(BF16)
            return 0
        lax.fori_loop(0, n_keys // kprep, prep, 0)
        _store_vt(vt_ref, lambda r0, nr: v_ref[r0:r0 + nr, :], n_keys, kprep)

    scale = GQA_HEAD_DIM ** -0.5 * LOG2E
    qs = []
    for h in range(GQA_GROUP):
        q = _rms(q_ref[:, h * GQA_HEAD_DIM:(h + 1) * GQA_HEAD_DIM].astype(F32), qg_ref[...], GQA_HEAD_DIM)
        qs.append((_rope(q, cq_ref[...], sq_ref[...]) * scale).T.astype(BF16))
    q_t = jnp.concatenate(qs, axis=1)
    out_t = _flash_t(q_t, kn_ref, vt_ref, n_keys, tk, bounded)
    for h in range(GQA_GROUP):
        o_ref[:, h * GQA_HEAD_DIM:(h + 1) * GQA_HEAD_DIM] = out_t[:, h * tq:(h + 1) * tq].T.astype(o_ref.dtype)


def _dispatch_on_score_bound(bound, attention, *args):
    return lax.cond(bound <= MAX_UNSHIFTED_SCORE,
                    functools.partial(attention, bounded=True),
                    functools.partial(attention, bounded=False), *args)


def _gqa_score_bound(q_gain, k_gain):
    return (GQA_HEAD_DIM * jnp.max(jnp.abs(q_gain)) * jnp.max(jnp.abs(k_gain))) * (GQA_HEAD_DIM ** -0.5 * LOG2E)


def _mla_score_bound(q_gain, kn_gain, kr_gain):
    q_norm = math.sqrt(MLA_QK_DIM) * jnp.max(jnp.abs(q_gain))
    k_norm = jnp.sqrt(MLA_NOPE_DIM * jnp.max(kn_gain * kn_gain) + MLA_ROPE_DIM * jnp.max(kr_gain * kr_gain))
    return q_norm * k_norm * (MLA_QK_DIM ** -0.5 * LOG2E)


def _gqa_attention(qkv, cos, sin, q_gain, k_gain, *, bounded):
    b, t, _ = qkv.shape
    tq = min(512 if t <= 2048 else 256, t)
    tk = min(1024 if t <= 2048 else 4096, t)
    kprep = min(512, t)
    gw = GQA_GROUP * GQA_HEAD_DIM
    k_col0 = GQA_Q_HEADS
    v_col0 = GQA_Q_HEADS + GQA_KV_HEADS
    tab_q = pl.BlockSpec((tq, LANES), lambda bi, g, qi: (qi, 0))
    tab_k = pl.BlockSpec((t, LANES), lambda bi, g, qi: (0, 0), pipeline_mode=pl.Buffered(1))
    gain = pl.BlockSpec((1, LANES), lambda bi, g, qi: (0, 0))
    return pl.pallas_call(
        functools.partial(_gqa_attn_kernel, tk=tk, kprep=kprep, bounded=bounded),
        grid=(b, GQA_KV_HEADS, t // tq),
        in_specs=[
            pl.BlockSpec((None, tq, gw), lambda bi, g, qi: (bi, qi, g)),
            pl.BlockSpec((None, t, GQA_HEAD_DIM), lambda bi, g, qi: (bi, 0, k_col0 + g)),
            pl.BlockSpec((None, t, GQA_HEAD_DIM), lambda bi, g, qi: (bi, 0, v_col0 + g)),
            tab_q, tab_q, tab_k, tab_k, gain, gain,
        ],
        out_specs=pl.BlockSpec((None, tq, gw), lambda bi, g, qi: (bi, qi, g)),
        out_shape=jax.ShapeDtypeStruct((b, t, GQA_Q_HEADS * GQA_HEAD_DIM), BF16),
        scratch_shapes=[pltpu.VMEM((t, GQA_HEAD_DIM), BF16), pltpu.VMEM((GQA_HEAD_DIM, t), BF16)],
        compiler_params=_cparams("parallel", "parallel", "arbitrary"),
    )(qkv, qkv, qkv, cos, sin, cos, sin, q_gain.reshape(1, LANES), k_gain.reshape(1, LANES))


def _mla_attn_kernel(q_ref, kv_ref, kr_ref, cq_ref, sq_ref, ck_ref, sk_ref, qg_ref, kng_ref, krg_ref,
                     o_ref, kn_ref, vt_ref, *, tk, kprep, bounded):
    n_keys = kv_ref.shape[0]

    @pl.when(pl.program_id(2) == 0)
    def _():
        def prep(c, _):
            rows = pl.ds(pl.multiple_of(c * kprep, kprep), kprep)
            kn = _rms(kv_ref[rows, 0:MLA_NOPE_DIM].astype(F32), kng_ref[...], MLA_NOPE_DIM)
            kn_ref[rows, 0:MLA_NOPE_DIM] = kn.astype(BF16)
            kr = _rms(kr_ref[rows, :].astype(F32), krg_ref[...], MLA_ROPE_DIM)
            kn_ref[rows, MLA_NOPE_DIM:MLA_QPAD] = _rope(kr, ck_ref[rows, :], sk_ref[rows, :]).astype(BF16)
            return 0
        lax.fori_loop(0, n_keys // kprep, prep, 0)
        _store_vt(vt_ref, lambda r0, nr: kv_ref[r0:r0 + nr, MLA_NOPE_DIM:MLA_QPAD], n_keys, kprep)

    scale = MLA_QK_DIM ** -0.5 * LOG2E
    q = q_ref[...].astype(F32)
    g = qg_ref[...]
    inv = lax.rsqrt(jnp.sum(q * q, axis=-1, keepdims=True) * (1.0 / MLA_QK_DIM) + NORM_EPS)
    q_nope = q[:, 0:MLA_NOPE_DIM] * inv * g[:, 0:MLA_NOPE_DIM]
    q_rot = _rope(q[:, MLA_NOPE_DIM:MLA_QPAD] * inv * g[:, MLA_NOPE_DIM:MLA_QPAD], cq_ref[...], sq_ref[...])
    q_t = jnp.concatenate([(q_nope * scale).T, (q_rot * scale).T], axis=0).astype(BF16)
    out_t = _flash_t(q_t, kn_ref, vt_ref, n_keys, tk, bounded)
    o_ref[...] = out_t.T.astype(o_ref.dtype)


def _mla_attention(q, kv, down, cos, sin, q_gain, kn_gain, kr_gain, *, bounded):
    b, t, _ = q.shape
    tq = min(2048 if t <= 2048 else 1024, t)
    tk = min(1024 if t <= 2048 else 2048, t)
    kprep = min(512, t)
    kr_col = (MLA_Q_RANK + MLA_KV_RANK) // LANES
    tab_q = pl.BlockSpec((tq, LANES), lambda bi, h, qi: (qi, 0))
    tab_k = pl.BlockSpec((t, LANES), lambda bi, h, qi: (0, 0), pipeline_mode=pl.Buffered(1))
    return pl.pallas_call(
        functools.partial(_mla_attn_kernel, tk=tk, kprep=kprep, bounded=bounded),
        grid=(b, MLA_HEADS, t // tq),
        in_specs=[
            pl.BlockSpec((None, tq, MLA_QPAD), lambda bi, h, qi: (bi, qi, h)),
            pl.BlockSpec((None, t, MLA_QPAD), lambda bi, h, qi: (bi, 0, h)),
            pl.BlockSpec((None, t, LANES), lambda bi, h, qi: (bi, 0, kr_col)),
            tab_q, tab_q, tab_k, tab_k,
            pl.BlockSpec((1, MLA_QPAD), lambda bi, h, qi: (0, 0)),
            pl.BlockSpec((1, LANES), lambda bi, h, qi: (0, 0)),
            pl.BlockSpec((1, LANES), lambda bi, h, qi: (0, 0)),
        ],
        out_specs=pl.BlockSpec((None, tq, MLA_V_DIM), lambda bi, h, qi: (bi, qi, h)),
        out_shape=jax.ShapeDtypeStruct((b, t, MLA_HEADS * MLA_V_DIM), BF16),
        scratch_shapes=[pltpu.VMEM((t, MLA_QPAD), BF16), pltpu.VMEM((MLA_V_DIM, t), BF16)],
        compiler_params=_cparams("parallel", "parallel", "arbitrary"),
    )(q, kv, down, cos, sin, cos, sin, q_gain.reshape(1, MLA_QPAD), kn_gain.reshape(1, LANES),
      kr_gain.reshape(1, LANES))


def _split3(x):
    hi = x.astype(BF16)
    r1 = x - hi.astype(F32)
    mid = r1.astype(BF16)
    lo = (r1 - mid.astype(F32)).astype(BF16)
    return hi, mid, lo


def _gla_kernel(*refs, reverse, n_chunks):
    if reverse:
        (q_ref, k_ref, v_ref, low_ref, wg_ref, bg_ref, of_ref, r_ref, on_ref, o_ref, s_ref) = refs
    else:
        (q_ref, k_ref, v_ref, low_ref, wg_ref, bg_ref, o_ref, s_ref) = refs
    L = GLA_BLOCK

    @pl.when(pl.program_id(2) == 0)
    def _():
        s_ref[...] = jnp.zeros_like(s_ref)

    row = lax.broadcasted_iota(jnp.int32, (L, L), 0)
    col = lax.broadcasted_iota(jnp.int32, (L, L), 1)
    keep = (col >= row) if reverse else (col <= row)
    tri = jnp.where(keep, 1.0, 0.0).astype(BF16)
    tn_dims = (((0,), (0,)), ((), ()))
    nt_dims = (((1,), (1,)), ((), ()))
    mid = L // 2 if reverse else L // 2 - 1
    edge = 0 if reverse else L - 1

    order = list(range(n_chunks - 1, -1, -1) if reverse else range(n_chunks))
    rows = {c: slice(c * L, (c + 1) * L) for c in order}

    z = jnp.dot(low_ref[...].astype(BF16), wg_ref[...], preferred_element_type=F32) + bg_ref[...]
    cum = {}
    for c in order:
        zc = z[rows[c], :]
        la = (jnp.minimum(zc, 0.0) - jnp.log1p(jnp.exp(-jnp.abs(zc)))) * (1.0 / GLA_GATE_TAU)
        cum[c] = sum(jnp.dot(tri, p, preferred_element_type=F32) for p in _split3(la))
    qd, qs, kd, ke, de = {}, {}, {}, {}, {}
    for c in order:
        ref_row = cum[c][mid:mid + 1, :]
        tot_row = cum[c][edge:edge + 1, :]
        q = q_ref[rows[c], :].astype(F32) * (GLA_DK_HEAD ** -0.5)
        k = k_ref[rows[c], :].astype(F32)
        q_mid = q * jnp.exp(cum[c] - ref_row)
        qd[c] = q_mid.astype(BF16)
        qs[c] = (q_mid * jnp.exp(ref_row)).astype(BF16)
        kd[c] = (k * jnp.exp(ref_row - cum[c])).astype(BF16)
        ke[c] = (k * jnp.exp(tot_row - cum[c])).astype(BF16)
        col_de = jnp.broadcast_to(jnp.exp(tot_row), (LANES, GLA_DK_HEAD)).T
        de[c] = jnp.concatenate([col_de] * (GLA_DV_HEAD // LANES), axis=1)
    a = {c: jnp.where(keep, lax.dot_general(qd[c], kd[c], nt_dims, preferred_element_type=F32), 0.0).astype(BF16)
         for c in order}
    o_intra = {c: jnp.dot(a[c], v_ref[rows[c], :], preferred_element_type=F32) for c in order}
    kv = {c: lax.dot_general(ke[c], v_ref[rows[c], :], tn_dims, preferred_element_type=F32) for c in order}

    s = s_ref[...]
    for c in order:
        o = o_intra[c] + jnp.dot(qs[c], s.astype(BF16), preferred_element_type=F32)
        s = de[c] * s + kv[c]
        if reverse:
            o = o + of_ref[rows[c], :]
            o = _rms(o, on_ref[...], GLA_DV_HEAD)
            r = r_ref[rows[c], :].astype(F32)
            o = o * (r * (1.0 / (1.0 + jnp.exp(-r))))
        o_ref[rows[c], :] = o.astype(o_ref.dtype)
    s_ref[...] = s


def _gla_direction(proj, low, wg, bg, o_fwd, o_norm, reverse):
    b, t, _ = proj.shape
    tb = min(1024, t)
    assert t % tb == 0 and tb % GLA_BLOCK == 0
    nblk = t // tb
    blk = (lambda i: nblk - 1 - i) if reverse else (lambda i: i)
    k_col0 = GLA_DK // GLA_DK_HEAD
    v_col0 = 2 * GLA_DK // GLA_DV_HEAD
    r_col0 = (2 * GLA_DK + GLA_DV) // GLA_DV_HEAD
    in_specs = [
        pl.BlockSpec((None, tb, GLA_DK_HEAD), lambda bi, h, i: (bi, blk(i), h)),
        pl.BlockSpec((None, tb, GLA_DK_HEAD), lambda bi, h, i: (bi, blk(i), k_col0 + h)),
        pl.BlockSpec((None, tb, GLA_DV_HEAD), lambda bi, h, i: (bi, blk(i), v_col0 + h)),
        pl.BlockSpec((None, tb, LANES), lambda bi, h, i: (bi, blk(i), 0)),
        pl.BlockSpec((LANES, GLA_DK_HEAD), lambda bi, h, i: (0, h)),
        pl.BlockSpec((1, GLA_DK_HEAD), lambda bi, h, i: (0, h)),
    ]
    args = [proj, proj, proj, low, wg, bg]
    if reverse:
        in_specs += [
            pl.BlockSpec((None, tb, GLA_DV_HEAD), lambda bi, h, i: (bi, blk(i), h)),
            pl.BlockSpec((None, tb, GLA_DV_HEAD), lambda bi, h, i: (bi, blk(i), r_col0 + h)),
            pl.BlockSpec((1, GLA_DV_HEAD), lambda bi, h, i: (0, 0)),
        ]
        args += [o_fwd, proj, o_norm.reshape(1, GLA_DV_HEAD)]
    return pl.pallas_call(
        functools.partial(_gla_kernel, reverse=reverse, n_chunks=tb // GLA_BLOCK),
        grid=(b, GLA_HEADS, nblk),
        in_specs=in_specs,
        out_specs=pl.BlockSpec((None, tb, GLA_DV_HEAD), lambda bi, h, i: (bi, blk(i), h)),
        out_shape=jax.ShapeDtypeStruct((b, t, GLA_DV), BF16 if reverse else F32),
        scratch_shapes=[pltpu.VMEM((GLA_DK_HEAD, GLA_DV_HEAD), F32)],
        compiler_params=_cparams("parallel", "parallel", "arbitrary"),
    )(*args)


def _prep_gqa(w_qkv, q_norm, k_norm):
    n_rot = GQA_Q_HEADS + GQA_KV_HEADS
    cols = (np.arange(n_rot)[:, None] * GQA_HEAD_DIM + _HALF_SPLIT_128[None, :]).reshape(-1)
    cols = np.concatenate([cols, np.arange(n_rot * GQA_HEAD_DIM, w_qkv.shape[1])])
    return w_qkv[:, cols].astype(BF16), q_norm[_HALF_SPLIT_128], k_norm[_HALF_SPLIT_128]


def _prep_mla(w_down, w_uq, q_norm, k_rope_norm):
    qidx, qmsk = _mla_q_layout()
    cols = (np.arange(MLA_HEADS)[:, None] * MLA_QK_DIM + qidx[None, :]).reshape(-1)
    w_uq_p = (w_uq[:, cols] * jnp.asarray(np.tile(qmsk, MLA_HEADS))).astype(BF16)
    q_gain = q_norm[qidx] * jnp.asarray(qmsk)
    ridx, rmsk = _mla_rot_layout(0)
    kr0 = MLA_Q_RANK + MLA_KV_RANK
    w_kr = w_down[:, kr0 + ridx] * jnp.asarray(rmsk)
    w_down_p = jnp.concatenate([w_down[:, :kr0], w_kr], axis=1).astype(BF16)
    kr_gain = k_rope_norm[ridx] * jnp.asarray(rmsk)
    return w_down_p, w_uq_p, q_gain, kr_gain


def _prep_gla_gates(w_g1, w_g2):
    r = GLA_GATE_RANK
    d = w_g1.shape[1]
    w1 = jnp.zeros((d, LANES), F32).at[:, 0:r].set(w_g1[0]).at[:, r:2 * r].set(w_g1[1])
    w2 = jnp.zeros((2, LANES, GLA_DK), F32).at[0, 0:r].set(w_g2[0]).at[1, r:2 * r].set(w_g2[1])
    return w1.astype(BF16), w2.astype(BF16)


def _trunk(x3, mod, p):
    b, t, d = x3.shape
    n = b * t
    x = x3.reshape(n, d)
    for i in range(DEPTH):
        m = mod[i].reshape(b, 6, 1, d)
        sh1, sc1, g1, sh2, sc2, g2 = (m[:, s] for s in range(6))
        kind, j = i % N_MIXERS, i // N_MIXERS
        if kind == 0:
            qkv = _norm_mod_matmul(x, p['norm1_g'][i], sc1, sh1, p['gqa_w_qkv'][j], t, BF16)
            cos, sin = _gqa_rope_tables(t)
            a = _dispatch_on_score_bound(_gqa_score_bound(p['gqa_q_norm'][j], p['gqa_k_norm'][j]), _gqa_attention,
                                         qkv.reshape(b, t, -1), cos, sin, p['gqa_q_norm'][j], p['gqa_k_norm'][j])
            x = _oproj_residual(a.reshape(n, -1), p['gqa_w_o'][j], x, g1, t)
        elif kind == 1:
            proj = _norm_mod_matmul(x, p['norm1_g'][i], sc1, sh1, p['gla_w_in'][j], t, BF16, tn=2048)
            low = _norm_mod_matmul(x, p['norm1_g'][i], sc1, sh1, p['gla_w_g1'][j], t, F32)
            proj3, low3 = proj.reshape(b, t, -1), low.reshape(b, t, -1)
            bg = p['gla_b_g'][j]
            o_f = _gla_direction(proj3, low3, p['gla_w_g2'][j][0], bg[0:1], None, None, reverse=False)
            a = _gla_direction(proj3, low3, p['gla_w_g2'][j][1], bg[1:2], o_f, p['gla_o_norm'][j], reverse=True)
            x = _oproj_residual(a.reshape(n, -1), p['gla_w_o'][j], x, g1, t)
        else:
            down = _norm_mod_matmul(x, p['norm1_g'][i], sc1, sh1, p['mla_w_down'][j], t, F32)
            q = _norm_matmul(down, 0, MLA_Q_RANK, p['mla_q_a_norm'][j], p['mla_w_uq'][j], t, BF16)
            kv = _norm_matmul(down, MLA_Q_RANK // MLA_KV_RANK, MLA_KV_RANK, p['mla_kv_a_norm'][j],
                              p['mla_w_ukv'][j], t, BF16)
            cos, sin = _mla_rope_tables(t)
            gains = (p['mla_q_gain'][j], p['mla_k_nope_norm'][j], p['mla_kr_gain'][j])
            a = _dispatch_on_score_bound(_mla_score_bound(*gains), _mla_attention, q.reshape(b, t, -1),
                                         kv.reshape(b, t, -1), down.reshape(b, t, -1), cos, sin, *gains)
            x = _oproj_residual(a.reshape(n, -1), p['mla_w_o'][j], x, g1, t)
        x = _mlp_residual(x, p['norm2_g'][i], sc2, sh2, g2, p['mlp_w1'][i], p['mlp_w2'][i], t)
    return x.reshape(b, t, d)


def kernel(x_prompt, x_sample, c_prompt, c_sample, norm1_g, norm2_g, ada_w, ada_b, gqa_w_qkv, gqa_q_norm, gqa_k_norm, gqa_w_o, gla_w_in, gla_w_g1, gla_w_g2, gla_b_g, gla_o_norm, gla_w_o, mla_w_down, mla_q_a_norm, mla_kv_a_norm, mla_w_uq, mla_w_ukv, mla_q_norm, mla_k_nope_norm, mla_k_rope_norm, mla_w_o, mlp_w1, mlp_w2):
    nA, nB, nC = gqa_w_qkv.shape[0], gla_w_in.shape[0], mla_w_down.shape[0]
    gqa = [_prep_gqa(gqa_w_qkv[j], gqa_q_norm[j], gqa_k_norm[j]) for j in range(nA)]
    mla = [_prep_mla(mla_w_down[j], mla_w_uq[j], mla_q_norm[j], mla_k_rope_norm[j]) for j in range(nC)]
    gates = [_prep_gla_gates(gla_w_g1[j], gla_w_g2[j]) for j in range(nB)]
    p = dict(
        norm1_g=norm1_g, norm2_g=norm2_g,
        gqa_w_qkv=[g[0] for g in gqa], gqa_q_norm=[g[1] for g in gqa], gqa_k_norm=[g[2] for g in gqa],
        gqa_w_o=gqa_w_o.astype(BF16),
        gla_w_in=gla_w_in.astype(BF16), gla_w_g1=[g[0] for g in gates], gla_w_g2=[g[1] for g in gates],
        gla_b_g=gla_b_g, gla_o_norm=gla_o_norm, gla_w_o=gla_w_o.astype(BF16),
        mla_w_down=[m[0] for m in mla], mla_w_uq=[m[1] for m in mla], mla_q_gain=[m[2] for m in mla],
        mla_kr_gain=[m[3] for m in mla], mla_q_a_norm=mla_q_a_norm, mla_kv_a_norm=mla_kv_a_norm,
        mla_w_ukv=mla_w_ukv.astype(BF16), mla_k_nope_norm=mla_k_nope_norm, mla_w_o=mla_w_o.astype(BF16),
        mlp_w1=mlp_w1.astype(BF16), mlp_w2=mlp_w2.astype(BF16),
    )
    bp, bs = c_prompt.shape[0], c_sample.shape[0]
    nb_pad = -(-(bp + bs) // 16) * 16
    c_all = jnp.concatenate([c_prompt, c_sample, jnp.zeros((nb_pad - bp - bs, c_prompt.shape[1]), F32)], axis=0)
    mod = _ada_modulation(c_all, ada_w, ada_b)
    y_prompt = _trunk(x_prompt, mod[:, :bp], p)
    y_sample = _trunk(x_sample, mod[:, bp:bp + bs], p)
    return (y_prompt, y_sample)
```

```python
import functools
import math

import numpy as np
import jax
import jax.numpy as jnp
from jax import lax
from jax.experimental import pallas as pl
from jax.experimental.pallas import tpu as pltpu

F32 = jnp.float32
BF16 = jnp.bfloat16

D_MODEL = 2048
DEPTH = 4
GRID_W = 64
ROPE_THETA = 10000.0
NORM_EPS = 1e-6
N_MIXERS = 3

GQA_HEAD_DIM = 128
GQA_Q_HEADS = 16
GQA_KV_HEADS = 4
GQA_GROUP = 4

GLA_HEADS = 4
GLA_DK = 1024
GLA_DV = 2048
GLA_DK_HEAD = 256
GLA_DV_HEAD = 512
GLA_GATE_RANK = 16
GLA_GATE_TAU = 16.0
GLA_BLOCK = 128

MLA_HEADS = 16
MLA_Q_RANK = 512
MLA_KV_RANK = 256
MLA_NOPE_DIM = 128
MLA_ROPE_DIM = 64
MLA_QK_DIM = 192
MLA_V_DIM = 128
MLA_QPAD = 256
MLA_DOWN_PAD = 896

D_FF = 4 * D_MODEL

LANES = 128
VMEM_LIMIT_BYTES = 56 * 1024 * 1024


def _tiles(n_rows, seq):
    tm = min(512, seq)
    assert seq % tm == 0 and n_rows % tm == 0
    return tm


SHORT_SEQ = 2048


def _attention_tiles(t, heads_per_step, long_seq_score_bytes):
    short = t <= SHORT_SEQ
    m_cols = 2048 if short else 1024
    score_bytes = 8 * 1024 * 1024 if short else long_seq_score_bytes
    tq = min(m_cols // heads_per_step, t)
    tk = min(score_bytes // (4 * m_cols), t)
    assert t % tq == 0 and t % tk == 0
    return tq, tk


def _cparams(*sem):
    return pltpu.CompilerParams(dimension_semantics=sem, vmem_limit_bytes=VMEM_LIMIT_BYTES)


def _ada_kernel(c_ref, w_ref, b_ref, o_ref):
    c = c_ref[...]
    cond = (c * (1.0 / (1.0 + jnp.exp(-c)))).astype(BF16)
    o_ref[...] = jnp.dot(cond, w_ref[...].astype(BF16), preferred_element_type=F32) + b_ref[...]


def _ada_modulation(c_all, ada_w, ada_b):
    nb, d = c_all.shape
    depth, _, n6 = ada_w.shape
    tn = 1024
    return pl.pallas_call(
        _ada_kernel,
        grid=(depth, n6 // tn),
        in_specs=[
            pl.BlockSpec((nb, d), lambda i, j: (0, 0)),
            pl.BlockSpec((None, d, tn), lambda i, j: (i, 0, j)),
            pl.BlockSpec((None, 1, tn), lambda i, j: (i, 0, j)),
        ],
        out_specs=pl.BlockSpec((None, nb, tn), lambda i, j: (i, 0, j)),
        out_shape=jax.ShapeDtypeStruct((depth, nb, n6), F32),
        compiler_params=_cparams("parallel", "parallel"),
    )(c_all, ada_w, ada_b.reshape(depth, 1, n6))


def _rms(x, g, n):
    ms = jnp.sum(x * x, axis=-1, keepdims=True) * (1.0 / n)
    return x * lax.rsqrt(ms + NORM_EPS) * g


NORM_ROWS = 16


def _store_norm(h_ref, x_ref, gain, shift):
    d = x_ref.shape[-1]
    blocks = [slice(r, r + NORM_ROWS) for r in range(0, x_ref.shape[0], NORM_ROWS)]
    inv = []
    for rows in blocks:
        x = x_ref[rows, :].astype(F32)
        inv.append(lax.rsqrt(jnp.sum(x * x, axis=-1, keepdims=True) * (1.0 / d) + NORM_EPS))
    for rows, inv_r in zip(blocks, inv):
        y = x_ref[rows, :].astype(F32) * inv_r * gain
        h_ref[rows, :] = (y if shift is None else y + shift).astype(BF16)


def _norm_mod_matmul_kernel(x_ref, g_ref, sc_ref, sh_ref, w_ref, o_ref, h_ref):
    @pl.when(pl.program_id(1) == 0)
    def _():
        _store_norm(h_ref, x_ref, g_ref[...] * (1.0 + sc_ref[...]), sh_ref[...])

    o_ref[...] = jnp.dot(h_ref[...], w_ref[...], preferred_element_type=F32).astype(o_ref.dtype)


def _norm_matmul_kernel(x_ref, g_ref, w_ref, o_ref, h_ref):
    @pl.when(pl.program_id(1) == 0)
    def _():
        _store_norm(h_ref, x_ref, g_ref[...], None)

    o_ref[...] = jnp.dot(h_ref[...], w_ref[...], preferred_element_type=F32).astype(o_ref.dtype)


def _norm_mod_matmul(x, g, sc, sh, w, seq, out_dtype, tn=None):
    n, d = x.shape
    nout = w.shape[1]
    tm = _tiles(n, seq)
    tn = nout if tn is None else tn
    tpb = seq // tm
    n_j = nout // tn
    return pl.pallas_call(
        _norm_mod_matmul_kernel,
        grid=(n // tm, n_j),
        in_specs=[
            pl.BlockSpec((tm, d), lambda i, j: (i, 0)),
            pl.BlockSpec((1, d), lambda i, j: (0, 0)),
            pl.BlockSpec((None, 1, d), lambda i, j: (i // tpb, 0, 0)),
            pl.BlockSpec((None, 1, d), lambda i, j: (i // tpb, 0, 0)),
            pl.BlockSpec((d, tn), lambda i, j: (0, _serpentine(i, j, n_j))),
        ],
        out_specs=pl.BlockSpec((tm, tn), lambda i, j: (i, _serpentine(i, j, n_j))),
        out_shape=jax.ShapeDtypeStruct((n, nout), out_dtype),
        scratch_shapes=[pltpu.VMEM((tm, d), BF16)],
        compiler_params=_cparams("parallel", "arbitrary"),
    )(x, g.reshape(1, d), sc, sh, w)


def _norm_matmul(x, col_block, width, g, w, seq, out_dtype):
    n = x.shape[0]
    nout = w.shape[1]
    tm = _tiles(n, seq)
    return pl.pallas_call(
        _norm_matmul_kernel,
        grid=(n // tm, 1),
        in_specs=[
            pl.BlockSpec((tm, width), lambda i, j: (i, col_block)),
            pl.BlockSpec((1, width), lambda i, j: (0, 0)),
            pl.BlockSpec((width, nout), lambda i, j: (0, 0)),
        ],
        out_specs=pl.BlockSpec((tm, nout), lambda i, j: (i, 0)),
        out_shape=jax.ShapeDtypeStruct((n, nout), out_dtype),
        scratch_shapes=[pltpu.VMEM((tm, width), BF16)],
        compiler_params=_cparams("parallel", "arbitrary"),
    )(x, g.reshape(1, width), w)


def _oproj_kernel(a_ref, w_ref, x_ref, gate_ref, o_ref):
    y = jnp.dot(a_ref[...], w_ref[...], preferred_element_type=F32)
    o_ref[...] = x_ref[...] + gate_ref[...] * y


def _oproj_residual(a, w, x, gate, seq):
    n, k = a.shape
    d = w.shape[1]
    tm = _tiles(n, seq)
    tpb = seq // tm
    return pl.pallas_call(
        _oproj_kernel,
        grid=(n // tm,),
        in_specs=[
            pl.BlockSpec((tm, k), lambda i: (i, 0)),
            pl.BlockSpec((k, d), lambda i: (0, 0)),
            pl.BlockSpec((tm, d), lambda i: (i, 0)),
            pl.BlockSpec((None, 1, d), lambda i: (i // tpb, 0, 0)),
        ],
        out_specs=pl.BlockSpec((tm, d), lambda i: (i, 0)),
        out_shape=jax.ShapeDtypeStruct((n, d), F32),
        compiler_params=_cparams("parallel"),
    )(a, w, x, gate)


def _mlp_kernel(x_ref, g_ref, sc_ref, sh_ref, gate_ref, w1_ref, w2_ref, o_ref, h_ref):
    j = pl.program_id(1)
    last = pl.num_programs(1) - 1

    def hidden():
        a = jnp.maximum(jnp.dot(h_ref[...], w1_ref[...], preferred_element_type=F32), 0.0)
        return jnp.dot((a * a).astype(BF16), w2_ref[...], preferred_element_type=F32)

    @pl.when(j == 0)
    def _():
        _store_norm(h_ref, x_ref, g_ref[...] * (1.0 + sc_ref[...]), sh_ref[...])
        o_ref[...] = hidden()

    @pl.when((j > 0) & (j < last))
    def _():
        o_ref[...] += hidden()

    @pl.when(j == last)
    def _():
        o_ref[...] = x_ref[...] + gate_ref[...] * (o_ref[...] + hidden())


def _serpentine(i, j, n_j):
    return jnp.where(lax.rem(i, 2) == 0, j, n_j - 1 - j)


def _mlp_residual(x, g, sc, sh, gate, w1, w2, seq):
    n, d = x.shape
    dff = w1.shape[1]
    tm = _tiles(n, seq)
    tf = min(2048, dff)
    assert dff % tf == 0 and dff // tf >= 2
    n_j = dff // tf
    tpb = seq // tm
    mod_spec = pl.BlockSpec((None, 1, d), lambda i, j: (i // tpb, 0, 0))
    return pl.pallas_call(
        _mlp_kernel,
        grid=(n // tm, n_j),
        in_specs=[
            pl.BlockSpec((tm, d), lambda i, j: (i, 0)),
            pl.BlockSpec((1, d), lambda i, j: (0, 0)),
            mod_spec, mod_spec, mod_spec,
            pl.BlockSpec((d, tf), lambda i, j: (0, _serpentine(i, j, n_j))),
            pl.BlockSpec((tf, d), lambda i, j: (_serpentine(i, j, n_j), 0)),
        ],
        out_specs=pl.BlockSpec((tm, d), lambda i, j: (i, 0)),
        out_shape=jax.ShapeDtypeStruct((n, d), F32),
        scratch_shapes=[pltpu.VMEM((tm, d), BF16)],
        compiler_params=_cparams("parallel", "arbitrary"),
    )(x, g.reshape(1, d), sc, sh, gate, w1, w2)


def _axial_angles(n_tok, rot_dim):
    rows = n_tok // GRID_W
    row = jnp.repeat(jnp.arange(rows, dtype=F32), GRID_W)
    col = jnp.tile(jnp.arange(GRID_W, dtype=F32), rows)
    n_freq = rot_dim // 4
    inv = ROPE_THETA ** (-jnp.arange(n_freq, dtype=F32) / n_freq)
    ang = jnp.concatenate([row[:, None] * inv, col[:, None] * inv], axis=-1)
    return jnp.cos(ang), jnp.sin(ang)


def _gqa_rope_tables(n_tok):
    cos, sin = _axial_angles(n_tok, GQA_HEAD_DIM)
    return jnp.concatenate([cos, cos], -1), jnp.concatenate([-sin, sin], -1)


def _mla_rope_tables(n_tok):
    cos, sin = _axial_angles(n_tok, MLA_ROPE_DIM)
    z = jnp.zeros_like(cos)
    return jnp.concatenate([cos, z, cos, z], -1), jnp.concatenate([-sin, z, sin, z], -1)


_HALF_SPLIT_128 = np.concatenate([np.arange(0, 128, 2), np.arange(1, 128, 2)])


def _mla_rot_layout(base):
    idx = np.zeros(128, np.int32)
    msk = np.zeros(128, np.float32)
    idx[0:32] = base + np.arange(0, 64, 2)
    idx[64:96] = base + np.arange(1, 64, 2)
    msk[0:32] = 1.0
    msk[64:96] = 1.0
    return idx, msk


def _mla_q_layout():
    ridx, rmsk = _mla_rot_layout(MLA_NOPE_DIM)
    idx = np.concatenate([np.arange(MLA_NOPE_DIM), ridx])
    msk = np.concatenate([np.ones(MLA_NOPE_DIM, np.float32), rmsk])
    return idx, msk


def _rope(x, cos, sin):
    return x * cos + pltpu.roll(x, 64, 1) * sin


LOG2E = math.log2(math.e)


MAX_UNSHIFTED_SCORE = 40.0


def _flash_t(q_t, kn_ref, vt_ref, n_keys, tk, bounded):
    m_cols = q_t.shape[1]
    m = jnp.full((1, m_cols), -jnp.inf, F32)
    l8 = jnp.zeros((8, m_cols), F32)
    acc = jnp.zeros((vt_ref.shape[0], m_cols), F32)
    for c in range(n_keys // tk):
        s = jnp.dot(kn_ref[c * tk:(c + 1) * tk, :], q_t, preferred_element_type=F32)
        if bounded:
            p = jnp.exp2(s)
            l8 = l8 + jnp.sum(p.reshape(tk // 8, 8, m_cols), axis=0)
            acc = acc + jnp.dot(vt_ref[:, c * tk:(c + 1) * tk], p.astype(BF16), preferred_element_type=F32)
        else:
            m_new = jnp.maximum(m, jnp.max(s, axis=0, keepdims=True))
            alpha = jnp.exp2(m - m_new)
            p = jnp.exp2(s - m_new)
            l8 = alpha * l8 + jnp.sum(p.reshape(tk // 8, 8, m_cols), axis=0)
            acc = alpha * acc + jnp.dot(vt_ref[:, c * tk:(c + 1) * tk], p.astype(BF16),
                                        preferred_element_type=F32)
            m = m_new
    return acc * (1.0 / jnp.sum(l8, axis=0, keepdims=True))


def _store_vt(vt_ref, v_load, n_keys, blk):
    for c in range(n_keys // blk):
        vt_ref[:, c * blk:(c + 1) * blk] = v_load(c * blk, blk).astype(F32).T.astype(BF16)


def _gqa_attn_kernel(q_ref, k_ref, v_ref, cq_ref, sq_ref, ck_ref, sk_ref, qg_ref, kg_ref, o_ref, kn_ref, vt_ref,
                     *, tk, kprep, bounded):
    n_keys = k_ref.shape[0]
    tq = q_ref.shape[0]

    @pl.when(pl.program_id(2) == 0)
    def _():
        def prep(c, _):
            rows = pl.ds(pl.multiple_of(c * kprep, kprep), kprep)
            k = _rms(k_ref[rows, :].astype(F32), kg_ref[...], GQA_HEAD_DIM)
            kn_ref[rows, :] = _rope(k, ck_ref[rows, :], sk_ref[rows, :]).astype(BF16)
            return 0
        lax.fori_loop(0, n_keys // kprep, prep, 0)
        _store_vt(vt_ref, lambda r0, nr: v_ref[r0:r0 + nr, :], n_keys, kprep)

    scale = GQA_HEAD_DIM ** -0.5 * LOG2E
    qs = []
    for h in range(GQA_GROUP):
        q = _rms(q_ref[:, h * GQA_HEAD_DIM:(h + 1) * GQA_HEAD_DIM].astype(F32), qg_ref[...], GQA_HEAD_DIM)
        qs.append((_rope(q, cq_ref[...], sq_ref[...]) * scale).T.astype(BF16))
    q_t = jnp.concatenate(qs, axis=1)
    out_t = _flash_t(q_t, kn_ref, vt_ref, n_keys, tk, bounded)
    for h in range(GQA_GROUP):
        o_ref[:, h * GQA_HEAD_DIM:(h + 1) * GQA_HEAD_DIM] = out_t[:, h * tq:(h + 1) * tq].T.astype(o_ref.dtype)


def _dispatch_on_score_bound(bound, attention, *args):
    return lax.cond(bound <= MAX_UNSHIFTED_SCORE,
                    functools.partial(attention, bounded=True),
                    functools.partial(attention, bounded=False), *args)


def _gqa_score_bound(q_gain, k_gain):
    return (GQA_HEAD_DIM * jnp.max(jnp.abs(q_gain)) * jnp.max(jnp.abs(k_gain))) * (GQA_HEAD_DIM ** -0.5 * LOG2E)


def _mla_score_bound(q_gain, kn_gain, kr_gain):
    q_norm = math.sqrt(MLA_QK_DIM) * jnp.max(jnp.abs(q_gain))
    k_norm = jnp.sqrt(MLA_NOPE_DIM * jnp.max(kn_gain * kn_gain) + MLA_ROPE_DIM * jnp.max(kr_gain * kr_gain))
    return q_norm * k_norm * (MLA_QK_DIM ** -0.5 * LOG2E)


def _gqa_attention(qkv, cos, sin, q_gain, k_gain, *, bounded):
    b, t, _ = qkv.shape
    tq, tk = _attention_tiles(t, GQA_GROUP, long_seq_score_bytes=16 * 1024 * 1024)
    kprep = min(512, t)
    gw = GQA_GROUP * GQA_HEAD_DIM
    k_col0 = GQA_Q_HEADS
    v_col0 = GQA_Q_HEADS + GQA_KV_HEADS
    tab_q = pl.BlockSpec((tq, LANES), lambda bi, g, qi: (qi, 0))
    tab_k = pl.BlockSpec((t, LANES), lambda bi, g, qi: (0, 0), pipeline_mode=pl.Buffered(1))
    gain = pl.BlockSpec((1, LANES), lambda bi, g, qi: (0, 0))
    return pl.pallas_call(
        functools.partial(_gqa_attn_kernel, tk=tk, kprep=kprep, bounded=bounded),
        grid=(b, GQA_KV_HEADS, t // tq),
        in_specs=[
            pl.BlockSpec((None, tq, gw), lambda bi, g, qi: (bi, qi, g)),
            pl.BlockSpec((None, t, GQA_HEAD_DIM), lambda bi, g, qi: (bi, 0, k_col0 + g)),
            pl.BlockSpec((None, t, GQA_HEAD_DIM), lambda bi, g, qi: (bi, 0, v_col0 + g)),
            tab_q, tab_q, tab_k, tab_k, gain, gain,
        ],
        out_specs=pl.BlockSpec((None, tq, gw), lambda bi, g, qi: (bi, qi, g)),
        out_shape=jax.ShapeDtypeStruct((b, t, GQA_Q_HEADS * GQA_HEAD_DIM), BF16),
        scratch_shapes=[pltpu.VMEM((t, GQA_HEAD_DIM), BF16), pltpu.VMEM((GQA_HEAD_DIM, t), BF16)],
        compiler_params=_cparams("parallel", "parallel", "arbitrary"),
    )(qkv, qkv, qkv, cos, sin, cos, sin, q_gain.reshape(1, LANES), k_gain.reshape(1, LANES))


def _mla_attn_kernel(q_ref, kv_ref, kr_ref, cq_ref, sq_ref, ck_ref, sk_ref, qg_ref, kng_ref, krg_ref,
                     o_ref, kn_ref, vt_ref, *, tk, kprep, bounded):
    n_keys = kv_ref.shape[0]

    @pl.when(pl.program_id(2) == 0)
    def _():
        def prep(c, _):
            rows = pl.ds(pl.multiple_of(c * kprep, kprep), kprep)
            kn = _rms(kv_ref[rows, 0:MLA_NOPE_DIM].astype(F32), kng_ref[...], MLA_NOPE_DIM)
            kn_ref[rows, 0:MLA_NOPE_DIM] = kn.astype(BF16)
            kr = _rms(kr_ref[rows, :].astype(F32), krg_ref[...], MLA_ROPE_DIM)
            kn_ref[rows, MLA_NOPE_DIM:MLA_QPAD] = _rope(kr, ck_ref[rows, :], sk_ref[rows, :]).astype(BF16)
            return 0
        lax.fori_loop(0, n_keys // kprep, prep, 0)
        _store_vt(vt_ref, lambda r0, nr: kv_ref[r0:r0 + nr, MLA_NOPE_DIM:MLA_QPAD], n_keys, kprep)

    scale = MLA_QK_DIM ** -0.5 * LOG2E
    q = q_ref[...].astype(F32)
    g = qg_ref[...]
    inv = lax.rsqrt(jnp.sum(q * q, axis=-1, keepdims=True) * (1.0 / MLA_QK_DIM) + NORM_EPS)
    q_nope = q[:, 0:MLA_NOPE_DIM] * inv * g[:, 0:MLA_NOPE_DIM]
    q_rot = _rope(q[:, MLA_NOPE_DIM:MLA_QPAD] * inv * g[:, MLA_NOPE_DIM:MLA_QPAD], cq_ref[...], sq_ref[...])
    q_t = jnp.concatenate([(q_nope * scale).T, (q_rot * scale).T], axis=0).astype(BF16)
    out_t = _flash_t(q_t, kn_ref, vt_ref, n_keys, tk, bounded)
    o_ref[...] = out_t.T.astype(o_ref.dtype)


def _mla_attention(q, kv, down, cos, sin, q_gain, kn_gain, kr_gain, *, bounded):
    b, t, _ = q.shape
    tq, tk = _attention_tiles(t, 1, long_seq_score_bytes=8 * 1024 * 1024)
    kprep = min(512, t)
    kr_col = (MLA_Q_RANK + MLA_KV_RANK) // LANES
    tab_q = pl.BlockSpec((tq, LANES), lambda bi, h, qi: (qi, 0))
    tab_k = pl.BlockSpec((t, LANES), lambda bi, h, qi: (0, 0), pipeline_mode=pl.Buffered(1))
    return pl.pallas_call(
        functools.partial(_mla_attn_kernel, tk=tk, kprep=kprep, bounded=bounded),
        grid=(b, MLA_HEADS, t // tq),
        in_specs=[
            pl.BlockSpec((None, tq, MLA_QPAD), lambda bi, h, qi: (bi, qi, h)),
            pl.BlockSpec((None, t, MLA_QPAD), lambda bi, h, qi: (bi, 0, h)),
            pl.BlockSpec((None, t, LANES), lambda bi, h, qi: (bi, 0, kr_col)),
            tab_q, tab_q, tab_k, tab_k,
            pl.BlockSpec((1, MLA_QPAD), lambda bi, h, qi: (0, 0)),
            pl.BlockSpec((1, LANES), lambda bi, h, qi: (0, 0)),
            pl.BlockSpec((1, LANES), lambda bi, h, qi: (0, 0)),
        ],
        out_specs=pl.BlockSpec((None, tq, MLA_V_DIM), lambda bi, h, qi: (bi, qi, h)),
        out_shape=jax.ShapeDtypeStruct((b, t, MLA_HEADS * MLA_V_DIM), BF16),
        scratch_shapes=[pltpu.VMEM((t, MLA_QPAD), BF16), pltpu.VMEM((MLA_V_DIM, t), BF16)],
        compiler_params=_cparams("parallel", "parallel", "arbitrary"),
    )(q, kv, down, cos, sin, cos, sin, q_gain.reshape(1, MLA_QPAD), kn_gain.reshape(1, LANES),
      kr_gain.reshape(1, LANES))


def _split3(x):
    hi = x.astype(BF16)
    r1 = x - hi.astype(F32)
    mid = r1.astype(BF16)
    lo = (r1 - mid.astype(F32)).astype(BF16)
    return hi, mid, lo


def _gla_kernel(*refs, reverse, n_chunks):
    if reverse:
        (q_ref, k_ref, v_ref, low_ref, wg_ref, bg_ref, of_ref, r_ref, on_ref, o_ref, s_ref) = refs
    else:
        (q_ref, k_ref, v_ref, low_ref, wg_ref, bg_ref, o_ref, s_ref) = refs
    L = GLA_BLOCK

    @pl.when(pl.program_id(2) == 0)
    def _():
        s_ref[...] = jnp.zeros_like(s_ref)

    row = lax.broadcasted_iota(jnp.int32, (L, L), 0)
    col = lax.broadcasted_iota(jnp.int32, (L, L), 1)
    keep = (col >= row) if reverse else (col <= row)
    tri = jnp.where(keep, 1.0, 0.0).astype(BF16)
    tn_dims = (((0,), (0,)), ((), ()))
    nt_dims = (((1,), (1,)), ((), ()))
    mid = L // 2 if reverse else L // 2 - 1
    edge = 0 if reverse else L - 1

    order = list(range(n_chunks - 1, -1, -1) if reverse else range(n_chunks))
    rows = {c: slice(c * L, (c + 1) * L) for c in order}

    z = jnp.dot(low_ref[...].astype(BF16), wg_ref[...], preferred_element_type=F32) + bg_ref[...]
    cum = {}
    for c in order:
        zc = z[rows[c], :]
        la = (jnp.minimum(zc, 0.0) - jnp.log1p(jnp.exp(-jnp.abs(zc)))) * (1.0 / GLA_GATE_TAU)
        cum[c] = sum(jnp.dot(tri, p, preferred_element_type=F32) for p in _split3(la))
    qd, qs, kd, ke, de = {}, {}, {}, {}, {}
    for c in order:
        ref_row = cum[c][mid:mid + 1, :]
        tot_row = cum[c][edge:edge + 1, :]
        q = q_ref[rows[c], :].astype(F32) * (GLA_DK_HEAD ** -0.5)
        k = k_ref[rows[c], :].astype(F32)
        q_mid = q * jnp.exp(cum[c] - ref_row)
        qd[c] = q_mid.astype(BF16)
        qs[c] = (q_mid * jnp.exp(ref_row)).astype(BF16)
        kd[c] = (k * jnp.exp(ref_row - cum[c])).astype(BF16)
        ke[c] = (k * jnp.exp(tot_row - cum[c])).astype(BF16)
        col_de = jnp.broadcast_to(jnp.exp(tot_row), (LANES, GLA_DK_HEAD)).T
        de[c] = jnp.concatenate([col_de] * (GLA_DV_HEAD // LANES), axis=1)
    a = {c: jnp.where(keep, lax.dot_general(qd[c], kd[c], nt_dims, preferred_element_type=F32), 0.0).astype(BF16)
         for c in order}
    o_intra = {c: jnp.dot(a[c], v_ref[rows[c], :], preferred_element_type=F32) for c in order}
    kv = {c: lax.dot_general(ke[c], v_ref[rows[c], :], tn_dims, preferred_element_type=F32) for c in order}

    s = s_ref[...]
    for c in order:
        o = o_intra[c] + jnp.dot(qs[c], s.astype(BF16), preferred_element_type=F32)
        s = de[c] * s + kv[c]
        if reverse:
            o = o + of_ref[rows[c], :]
            o = _rms(o, on_ref[...], GLA_DV_HEAD)
            r = r_ref[rows[c], :].astype(F32)
            o = o * (r * (1.0 / (1.0 + jnp.exp(-r))))
        o_ref[rows[c], :] = o.astype(o_ref.dtype)
    s_ref[...] = s


def _gla_direction(proj, low, wg, bg, o_fwd, o_norm, reverse):
    b, t, _ = proj.shape
    tb = min(1024, t)
    assert t % tb == 0 and tb % GLA_BLOCK == 0
    nblk = t // tb
    blk = (lambda i: nblk - 1 - i) if reverse else (lambda i: i)
    k_col0 = GLA_DK // GLA_DK_HEAD
    v_col0 = 2 * GLA_DK // GLA_DV_HEAD
    r_col0 = (2 * GLA_DK + GLA_DV) // GLA_DV_HEAD
    in_specs = [
        pl.BlockSpec((None, tb, GLA_DK_HEAD), lambda bi, h, i: (bi, blk(i), h)),
        pl.BlockSpec((None, tb, GLA_DK_HEAD), lambda bi, h, i: (bi, blk(i), k_col0 + h)),
        pl.BlockSpec((None, tb, GLA_DV_HEAD), lambda bi, h, i: (bi, blk(i), v_col0 + h)),
        pl.BlockSpec((None, tb, LANES), lambda bi, h, i: (bi, blk(i), 0)),
        pl.BlockSpec((LANES, GLA_DK_HEAD), lambda bi, h, i: (0, h)),
        pl.BlockSpec((1, GLA_DK_HEAD), lambda bi, h, i: (0, h)),
    ]
    args = [proj, proj, proj, low, wg, bg]
    if reverse:
        in_specs += [
            pl.BlockSpec((None, tb, GLA_DV_HEAD), lambda bi, h, i: (bi, blk(i), h)),
            pl.BlockSpec((None, tb, GLA_DV_HEAD), lambda bi, h, i: (bi, blk(i), r_col0 + h)),
            pl.BlockSpec((1, GLA_DV_HEAD), lambda bi, h, i: (0, 0)),
        ]
        args += [o_fwd, proj, o_norm.reshape(1, GLA_DV_HEAD)]
    return pl.pallas_call(
        functools.partial(_gla_kernel, reverse=reverse, n_chunks=tb // GLA_BLOCK),
        grid=(b, GLA_HEADS, nblk),
        in_specs=in_specs,
        out_specs=pl.BlockSpec((None, tb, GLA_DV_HEAD), lambda bi, h, i: (bi, blk(i), h)),
        out_shape=jax.ShapeDtypeStruct((b, t, GLA_DV), BF16 if reverse else F32),
        scratch_shapes=[pltpu.VMEM((GLA_DK_HEAD, GLA_DV_HEAD), F32)],
        compiler_params=_cparams("parallel", "parallel", "arbitrary"),
    )(*args)


def _prep_gqa(w_qkv, q_norm, k_norm):
    n_rot = GQA_Q_HEADS + GQA_KV_HEADS
    cols = (np.arange(n_rot)[:, None] * GQA_HEAD_DIM + _HALF_SPLIT_128[None, :]).reshape(-1)
    cols = np.concatenate([cols, np.arange(n_rot * GQA_HEAD_DIM, w_qkv.shape[1])])
    return w_qkv[:, cols].astype(BF16), q_norm[_HALF_SPLIT_128], k_norm[_HALF_SPLIT_128]


def _prep_mla(w_down, w_uq, q_norm, k_rope_norm):
    qidx, qmsk = _mla_q_layout()
    cols = (np.arange(MLA_HEADS)[:, None] * MLA_QK_DIM + qidx[None, :]).reshape(-1)
    w_uq_p = (w_uq[:, cols] * jnp.asarray(np.tile(qmsk, MLA_HEADS))).astype(BF16)
    q_gain = q_norm[qidx] * jnp.asarray(qmsk)
    ridx, rmsk = _mla_rot_layout(0)
    kr0 = MLA_Q_RANK + MLA_KV_RANK
    w_kr = w_down[:, kr0 + ridx] * jnp.asarray(rmsk)
    w_down_p = jnp.concatenate([w_down[:, :kr0], w_kr], axis=1).astype(BF16)
    kr_gain = k_rope_norm[ridx] * jnp.asarray(rmsk)
    return w_down_p, w_uq_p, q_gain, kr_gain


def _prep_gla_gates(w_g1, w_g2):
    r = GLA_GATE_RANK
    d = w_g1.shape[1]
    w1 = jnp.zeros((d, LANES), F32).at[:, 0:r].set(w_g1[0]).at[:, r:2 * r].set(w_g1[1])
    w2 = jnp.zeros((2, LANES, GLA_DK), F32).at[0, 0:r].set(w_g2[0]).at[1, r:2 * r].set(w_g2[1])
    return w1.astype(BF16), w2.astype(BF16)


def _trunk(x3, mod, p):
    b, t, d = x3.shape
    n = b * t
    x = x3.reshape(n, d)
    for i in range(DEPTH):
        m = mod[i].reshape(b, 6, 1, d)
        sh1, sc1, g1, sh2, sc2, g2 = (m[:, s] for s in range(6))
        kind, j = i % N_MIXERS, i // N_MIXERS
        if kind == 0:
            qkv = _norm_mod_matmul(x, p['norm1_g'][i], sc1, sh1, p['gqa_w_qkv'][j], t, BF16)
            cos, sin = _gqa_rope_tables(t)
            a = _dispatch_on_score_bound(_gqa_score_bound(p['gqa_q_norm'][j], p['gqa_k_norm'][j]), _gqa_attention,
                                         qkv.reshape(b, t, -1), cos, sin, p['gqa_q_norm'][j], p['gqa_k_norm'][j])
            x = _oproj_residual(a.reshape(n, -1), p['gqa_w_o'][j], x, g1, t)
        elif kind == 1:
            proj = _norm_mod_matmul(x, p['norm1_g'][i], sc1, sh1, p['gla_w_in'][j], t, BF16, tn=2048)
            low = _norm_mod_matmul(x, p['norm1_g'][i], sc1, sh1, p['gla_w_g1'][j], t, F32)
            proj3, low3 = proj.reshape(b, t, -1), low.reshape(b, t, -1)
            bg = p['gla_b_g'][j]
            o_f = _gla_direction(proj3, low3, p['gla_w_g2'][j][0], bg[0:1], None, None, reverse=False)
            a = _gla_direction(proj3, low3, p['gla_w_g2'][j][1], bg[1:2], o_f, p['gla_o_norm'][j], reverse=True)
            x = _oproj_residual(a.reshape(n, -1), p['gla_w_o'][j], x, g1, t)
        else:
            down = _norm_mod_matmul(x, p['norm1_g'][i], sc1, sh1, p['mla_w_down'][j], t, F32)
            q = _norm_matmul(down, 0, MLA_Q_RANK, p['mla_q_a_norm'][j], p['mla_w_uq'][j], t, BF16)
            kv = _norm_matmul(down, MLA_Q_RANK // MLA_KV_RANK, MLA_KV_RANK, p['mla_kv_a_norm'][j],
                              p['mla_w_ukv'][j], t, BF16)
            cos, sin = _mla_rope_tables(t)
            gains = (p['mla_q_gain'][j], p['mla_k_nope_norm'][j], p['mla_kr_gain'][j])
            a = _dispatch_on_score_bound(_mla_score_bound(*gains), _mla_attention, q.reshape(b, t, -1),
                                         kv.reshape(b, t, -1), down.reshape(b, t, -1), cos, sin, *gains)
            x = _oproj_residual(a.reshape(n, -1), p['mla_w_o'][j], x, g1, t)
        x = _mlp_residual(x, p['norm2_g'][i], sc2, sh2, g2, p['mlp_w1'][i], p['mlp_w2'][i], t)
    return x.reshape(b, t, d)


def kernel(x_prompt, x_sample, c_prompt, c_sample, norm1_g, norm2_g, ada_w, ada_b, gqa_w_qkv, gqa_q_norm, gqa_k_norm, gqa_w_o, gla_w_in, gla_w_g1, gla_w_g2, gla_b_g, gla_o_norm, gla_w_o, mla_w_down, mla_q_a_norm, mla_kv_a_norm, mla_w_uq, mla_w_ukv, mla_q_norm, mla_k_nope_norm, mla_k_rope_norm, mla_w_o, mlp_w1, mlp_w2):
    nA, nB, nC = gqa_w_qkv.shape[0], gla_w_in.shape[0], mla_w_down.shape[0]
    gqa = [_prep_gqa(gqa_w_qkv[j], gqa_q_norm[j], gqa_k_norm[j]) for j in range(nA)]
    mla = [_prep_mla(mla_w_down[j], mla_w_uq[j], mla_q_norm[j], mla_k_rope_norm[j]) for j in range(nC)]
    gates = [_prep_gla_gates(gla_w_g1[j], gla_w_g2[j]) for j in range(nB)]
    p = dict(
        norm1_g=norm1_g, norm2_g=norm2_g,
        gqa_w_qkv=[g[0] for g in gqa], gqa_q_norm=[g[1] for g in gqa], gqa_k_norm=[g[2] for g in gqa],
        gqa_w_o=gqa_w_o.astype(BF16),
        gla_w_in=gla_w_in.astype(BF16), gla_w_g1=[g[0] for g in gates], gla_w_g2=[g[1] for g in gates],
        gla_b_g=gla_b_g, gla_o_norm=gla_o_norm, gla_w_o=gla_w_o.astype(BF16),
        mla_w_down=[m[0] for m in mla], mla_w_uq=[m[1] for m in mla], mla_q_gain=[m[2] for m in mla],
        mla_kr_gain=[m[3] for m in mla], mla_q_a_norm=mla_q_a_norm, mla_kv_a_norm=mla_kv_a_norm,
        mla_w_ukv=mla_w_ukv.astype(BF16), mla_k_nope_norm=mla_k_nope_norm, mla_w_o=mla_w_o.astype(BF16),
        mlp_w1=mlp_w1.astype(BF16), mlp_w2=mlp_w2.astype(BF16),
    )
    bp, bs = c_prompt.shape[0], c_sample.shape[0]
    nb_pad = -(-(bp + bs) // 16) * 16
    c_all = jnp.concatenate([c_prompt, c_sample, jnp.zeros((nb_pad - bp - bs, c_prompt.shape[1]), F32)], axis=0)
    mod = _ada_modulation(c_all, ada_w, ada_b)
    y_prompt = _trunk(x_prompt, mod[:, :bp], p)
    y_sample = _trunk(x_sample, mod[:, bp:bp + bs], p)
    return (y_prompt, y_sample)
```

```python
import functools
import math

import numpy as np
import jax
import jax.numpy as jnp
from jax import lax
from jax.experimental import pallas as pl
from jax.experimental.pallas import tpu as pltpu

F32 = jnp.float32
BF16 = jnp.bfloat16

D_MODEL = 2048
DEPTH = 4
GRID_W = 64
ROPE_THETA = 10000.0
NORM_EPS = 1e-6
N_MIXERS = 3

GQA_HEAD_DIM = 128
GQA_Q_HEADS = 16
GQA_KV_HEADS = 4
GQA_GROUP = 4

GLA_HEADS = 4
GLA_DK = 1024
GLA_DV = 2048
GLA_DK_HEAD = 256
GLA_DV_HEAD = 512
GLA_GATE_RANK = 16
GLA_GATE_TAU = 16.0
GLA_BLOCK = 128

MLA_HEADS = 16
MLA_Q_RANK = 512
MLA_KV_RANK = 256
MLA_NOPE_DIM = 128
MLA_ROPE_DIM = 64
MLA_QK_DIM = 192
MLA_V_DIM = 128
MLA_QPAD = 256
MLA_DOWN_PAD = 896

D_FF = 4 * D_MODEL

LANES = 128
VMEM_LIMIT_BYTES = 56 * 1024 * 1024


def _tiles(n_rows, seq):
    tm = min(512, seq)
    assert seq % tm == 0 and n_rows % tm == 0
    return tm


SHORT_SEQ = 2048


def _attention_tiles(t, heads_per_step, long_seq_score_bytes):
    short = t <= SHORT_SEQ
    m_cols = 2048 if short else 1024
    score_bytes = 16 * 1024 * 1024 if short else long_seq_score_bytes
    tq = min(m_cols // heads_per_step, t)
    tk = min(score_bytes // (4 * m_cols), t)
    assert t % tq == 0 and t % tk == 0
    return tq, tk


def _cparams(*sem):
    return pltpu.CompilerParams(dimension_semantics=sem, vmem_limit_bytes=VMEM_LIMIT_BYTES)


def _ada_kernel(c_ref, w_ref, b_ref, o_ref):
    c = c_ref[...]
    cond = (c * (1.0 / (1.0 + jnp.exp(-c)))).astype(BF16)
    o_ref[...] = jnp.dot(cond, w_ref[...].astype(BF16), preferred_element_type=F32) + b_ref[...]


def _ada_modulation(c_all, ada_w, ada_b):
    nb, d = c_all.shape
    depth, _, n6 = ada_w.shape
    tn = 1024
    return pl.pallas_call(
        _ada_kernel,
        grid=(depth, n6 // tn),
        in_specs=[
            pl.BlockSpec((nb, d), lambda i, j: (0, 0)),
            pl.BlockSpec((None, d, tn), lambda i, j: (i, 0, j)),
            pl.BlockSpec((None, 1, tn), lambda i, j: (i, 0, j)),
        ],
        out_specs=pl.BlockSpec((None, nb, tn), lambda i, j: (i, 0, j)),
        out_shape=jax.ShapeDtypeStruct((depth, nb, n6), F32),
        compiler_params=_cparams("parallel", "parallel"),
    )(c_all, ada_w, ada_b.reshape(depth, 1, n6))


def _rms(x, g, n):
    ms = jnp.sum(x * x, axis=-1, keepdims=True) * (1.0 / n)
    return x * lax.rsqrt(ms + NORM_EPS) * g


NORM_ROWS = 16


def _store_norm(h_ref, x_ref, gain, shift):
    d = x_ref.shape[-1]
    blocks = [slice(r, r + NORM_ROWS) for r in range(0, x_ref.shape[0], NORM_ROWS)]
    inv = []
    for rows in blocks:
        x = x_ref[rows, :].astype(F32)
        inv.append(lax.rsqrt(jnp.sum(x * x, axis=-1, keepdims=True) * (1.0 / d) + NORM_EPS))
    for rows, inv_r in zip(blocks, inv):
        y = x_ref[rows, :].astype(F32) * inv_r * gain
        h_ref[rows, :] = (y if shift is None else y + shift).astype(BF16)


def _norm_mod_matmul_kernel(x_ref, g_ref, sc_ref, sh_ref, w_ref, o_ref, h_ref):
    @pl.when(pl.program_id(1) == 0)
    def _():
        _store_norm(h_ref, x_ref, g_ref[...] * (1.0 + sc_ref[...]), sh_ref[...])

    o_ref[...] = jnp.dot(h_ref[...], w_ref[...], preferred_element_type=F32).astype(o_ref.dtype)


def _norm_matmul_kernel(x_ref, g_ref, w_ref, o_ref, h_ref):
    @pl.when(pl.program_id(1) == 0)
    def _():
        _store_norm(h_ref, x_ref, g_ref[...], None)

    o_ref[...] = jnp.dot(h_ref[...], w_ref[...], preferred_element_type=F32).astype(o_ref.dtype)


def _norm_mod_matmul(x, g, sc, sh, w, seq, out_dtype, tn=None):
    n, d = x.shape
    nout = w.shape[1]
    tm = _tiles(n, seq)
    tn = nout if tn is None else tn
    tpb = seq // tm
    n_j = nout // tn
    return pl.pallas_call(
        _norm_mod_matmul_kernel,
        grid=(n // tm, n_j),
        in_specs=[
            pl.BlockSpec((tm, d), lambda i, j: (i, 0)),
            pl.BlockSpec((1, d), lambda i, j: (0, 0)),
            pl.BlockSpec((None, 1, d), lambda i, j: (i // tpb, 0, 0)),
            pl.BlockSpec((None, 1, d), lambda i, j: (i // tpb, 0, 0)),
            pl.BlockSpec((d, tn), lambda i, j: (0, _serpentine(i, j, n_j))),
        ],
        out_specs=pl.BlockSpec((tm, tn), lambda i, j: (i, _serpentine(i, j, n_j))),
        out_shape=jax.ShapeDtypeStruct((n, nout), out_dtype),
        scratch_shapes=[pltpu.VMEM((tm, d), BF16)],
        compiler_params=_cparams("parallel", "arbitrary"),
    )(x, g.reshape(1, d), sc, sh, w)


def _norm_matmul(x, col_block, width, g, w, seq, out_dtype):
    n = x.shape[0]
    nout = w.shape[1]
    tm = _tiles(n, seq)
    return pl.pallas_call(
        _norm_matmul_kernel,
        grid=(n // tm, 1),
        in_specs=[
            pl.BlockSpec((tm, width), lambda i, j: (i, col_block)),
            pl.BlockSpec((1, width), lambda i, j: (0, 0)),
            pl.BlockSpec((width, nout), lambda i, j: (0, 0)),
        ],
        out_specs=pl.BlockSpec((tm, nout), lambda i, j: (i, 0)),
        out_shape=jax.ShapeDtypeStruct((n, nout), out_dtype),
        scratch_shapes=[pltpu.VMEM((tm, width), BF16)],
        compiler_params=_cparams("parallel", "arbitrary"),
    )(x, g.reshape(1, width), w)


def _oproj_kernel(a_ref, w_ref, x_ref, gate_ref, o_ref):
    y = jnp.dot(a_ref[...], w_ref[...], preferred_element_type=F32)
    o_ref[...] = x_ref[...] + gate_ref[...] * y


def _oproj_residual(a, w, x, gate, seq):
    n, k = a.shape
    d = w.shape[1]
    tm = _tiles(n, seq)
    tpb = seq // tm
    return pl.pallas_call(
        _oproj_kernel,
        grid=(n // tm,),
        in_specs=[
            pl.BlockSpec((tm, k), lambda i: (i, 0)),
            pl.BlockSpec((k, d), lambda i: (0, 0)),
            pl.BlockSpec((tm, d), lambda i: (i, 0)),
            pl.BlockSpec((None, 1, d), lambda i: (i // tpb, 0, 0)),
        ],
        out_specs=pl.BlockSpec((tm, d), lambda i: (i, 0)),
        out_shape=jax.ShapeDtypeStruct((n, d), F32),
        compiler_params=_cparams("parallel"),
    )(a, w, x, gate)


def _mlp_kernel(x_ref, g_ref, sc_ref, sh_ref, gate_ref, w1_ref, w2_ref, o_ref, h_ref):
    j = pl.program_id(1)
    last = pl.num_programs(1) - 1

    def hidden():
        a = jnp.maximum(jnp.dot(h_ref[...], w1_ref[...], preferred_element_type=F32), 0.0)
        return jnp.dot((a * a).astype(BF16), w2_ref[...], preferred_element_type=F32)

    @pl.when(j == 0)
    def _():
        _store_norm(h_ref, x_ref, g_ref[...] * (1.0 + sc_ref[...]), sh_ref[...])
        o_ref[...] = hidden()

    @pl.when((j > 0) & (j < last))
    def _():
        o_ref[...] += hidden()

    @pl.when(j == last)
    def _():
        o_ref[...] = x_ref[...] + gate_ref[...] * (o_ref[...] + hidden())


def _serpentine(i, j, n_j):
    return jnp.where(lax.rem(i, 2) == 0, j, n_j - 1 - j)


def _mlp_residual(x, g, sc, sh, gate, w1, w2, seq):
    n, d = x.shape
    dff = w1.shape[1]
    tm = _tiles(n, seq)
    tf = min(2048, dff)
    assert dff % tf == 0 and dff // tf >= 2
    n_j = dff // tf
    tpb = seq // tm
    mod_spec = pl.BlockSpec((None, 1, d), lambda i, j: (i // tpb, 0, 0))
    return pl.pallas_call(
        _mlp_kernel,
        grid=(n // tm, n_j),
        in_specs=[
            pl.BlockSpec((tm, d), lambda i, j: (i, 0)),
            pl.BlockSpec((1, d), lambda i, j: (0, 0)),
            mod_spec, mod_spec, mod_spec,
            pl.BlockSpec((d, tf), lambda i, j: (0, _serpentine(i, j, n_j))),
            pl.BlockSpec((tf, d), lambda i, j: (_serpentine(i, j, n_j), 0)),
        ],
        out_specs=pl.BlockSpec((tm, d), lambda i, j: (i, 0)),
        out_shape=jax.ShapeDtypeStruct((n, d), F32),
        scratch_shapes=[pltpu.VMEM((tm, d), BF16)],
        compiler_params=_cparams("parallel", "arbitrary"),
    )(x, g.reshape(1, d), sc, sh, gate, w1, w2)


def _axial_angles(n_tok, rot_dim):
    rows = n_tok // GRID_W
    row = jnp.repeat(jnp.arange(rows, dtype=F32), GRID_W)
    col = jnp.tile(jnp.arange(GRID_W, dtype=F32), rows)
    n_freq = rot_dim // 4
    inv = ROPE_THETA ** (-jnp.arange(n_freq, dtype=F32) / n_freq)
    ang = jnp.concatenate([row[:, None] * inv, col[:, None] * inv], axis=-1)
    return jnp.cos(ang), jnp.sin(ang)


def _gqa_rope_tables(n_tok):
    cos, sin = _axial_angles(n_tok, GQA_HEAD_DIM)
    return jnp.concatenate([cos, cos], -1), jnp.concatenate([-sin, sin], -1)


def _mla_rope_tables(n_tok):
    cos, sin = _axial_angles(n_tok, MLA_ROPE_DIM)
    z = jnp.zeros_like(cos)
    return jnp.concatenate([cos, z, cos, z], -1), jnp.concatenate([-sin, z, sin, z], -1)


_HALF_SPLIT_128 = np.concatenate([np.arange(0, 128, 2), np.arange(1, 128, 2)])


def _mla_rot_layout(base):
    idx = np.zeros(128, np.int32)
    msk = np.zeros(128, np.float32)
    idx[0:32] = base + np.arange(0, 64, 2)
    idx[64:96] = base + np.arange(1, 64, 2)
    msk[0:32] = 1.0
    msk[64:96] = 1.0
    return idx, msk


def _mla_q_layout():
    ridx, rmsk = _mla_rot_layout(MLA_NOPE_DIM)
    idx = np.concatenate([np.arange(MLA_NOPE_DIM), ridx])
    msk = np.concatenate([np.ones(MLA_NOPE_DIM, np.float32), rmsk])
    return idx, msk


def _rope(x, cos, sin):
    return x * cos + pltpu.roll(x, 64, 1) * sin


LOG2E = math.log2(math.e)


MAX_UNSHIFTED_SCORE = 40.0


def _flash_t(q_t, kn_ref, vt_ref, n_keys, tk, bounded):
    m_cols = q_t.shape[1]
    m = jnp.full((1, m_cols), -jnp.inf, F32)
    l8 = jnp.zeros((8, m_cols), F32)
    acc = jnp.zeros((vt_ref.shape[0], m_cols), F32)
    for c in range(n_keys // tk):
        s = jnp.dot(kn_ref[c * tk:(c + 1) * tk, :], q_t, preferred_element_type=F32)
        if bounded:
            p = jnp.exp2(s)
            l8 = l8 + jnp.sum(p.reshape(tk // 8, 8, m_cols), axis=0)
            acc = acc + jnp.dot(vt_ref[:, c * tk:(c + 1) * tk], p.astype(BF16), preferred_element_type=F32)
        else:
            m_new = jnp.maximum(m, jnp.max(s, axis=0, keepdims=True))
            alpha = jnp.exp2(m - m_new)
            p = jnp.exp2(s - m_new)
            l8 = alpha * l8 + jnp.sum(p.reshape(tk // 8, 8, m_cols), axis=0)
            acc = alpha * acc + jnp.dot(vt_ref[:, c * tk:(c + 1) * tk], p.astype(BF16),
                                        preferred_element_type=F32)
            m = m_new
    return acc * (1.0 / jnp.sum(l8, axis=0, keepdims=True))


def _store_vt(vt_ref, v_load, n_keys, blk):
    for c in range(n_keys // blk):
        vt_ref[:, c * blk:(c + 1) * blk] = v_load(c * blk, blk).astype(F32).T.astype(BF16)


def _gqa_attn_kernel(q_ref, k_ref, v_ref, cq_ref, sq_ref, ck_ref, sk_ref, qg_ref, kg_ref, o_ref, kn_ref, vt_ref,
                     *, tk, kprep, bounded):
    n_keys = k_ref.shape[0]
    tq = q_ref.shape[0]

    @pl.when(pl.program_id(2) == 0)
    def _():
        def prep(c, _):
            rows = pl.ds(pl.multiple_of(c * kprep, kprep), kprep)
            k = _rms(k_ref[rows, :].astype(F32), kg_ref[...], GQA_HEAD_DIM)
            kn_ref[rows, :] = _rope(k, ck_ref[rows, :], sk_ref[rows, :]).astype(BF16)
            return 0
        lax.fori_loop(0, n_keys // kprep, prep, 0)
        _store_vt(vt_ref, lambda r0, nr: v_ref[r0:r0 + nr, :], n_keys, kprep)

    scale = GQA_HEAD_DIM ** -0.5 * LOG2E
    qs = []
    for h in range(GQA_GROUP):
        q = _rms(q_ref[:, h * GQA_HEAD_DIM:(h + 1) * GQA_HEAD_DIM].astype(F32), qg_ref[...], GQA_HEAD_DIM)
        qs.append((_rope(q, cq_ref[...], sq_ref[...]) * scale).T.astype(BF16))
    q_t = jnp.concatenate(qs, axis=1)
    out_t = _flash_t(q_t, kn_ref, vt_ref, n_keys, tk, bounded)
    for h in range(GQA_GROUP):
        o_ref[:, h * GQA_HEAD_DIM:(h + 1) * GQA_HEAD_DIM] = out_t[:, h * tq:(h + 1) * tq].T.astype(o_ref.dtype)


def _dispatch_on_score_bound(bound, attention, *args):
    return lax.cond(bound <= MAX_UNSHIFTED_SCORE,
                    functools.partial(attention, bounded=True),
                    functools.partial(attention, bounded=False), *args)


def _gqa_score_bound(q_gain, k_gain):
    return (GQA_HEAD_DIM * jnp.max(jnp.abs(q_gain)) * jnp.max(jnp.abs(k_gain))) * (GQA_HEAD_DIM ** -0.5 * LOG2E)


def _mla_score_bound(q_gain, kn_gain, kr_gain):
    q_norm = math.sqrt(MLA_QK_DIM) * jnp.max(jnp.abs(q_gain))
    k_norm = jnp.sqrt(MLA_NOPE_DIM * jnp.max(kn_gain * kn_gain) + MLA_ROPE_DIM * jnp.max(kr_gain * kr_gain))
    return q_norm * k_norm * (MLA_QK_DIM ** -0.5 * LOG2E)


def _gqa_attention(qkv, cos, sin, q_gain, k_gain, *, bounded):
    b, t, _ = qkv.shape
    tq, tk = _attention_tiles(t, GQA_GROUP, long_seq_score_bytes=16 * 1024 * 1024)
    kprep = min(512, t)
    gw = GQA_GROUP * GQA_HEAD_DIM
    k_col0 = GQA_Q_HEADS
    v_col0 = GQA_Q_HEADS + GQA_KV_HEADS
    tab_q = pl.BlockSpec((tq, LANES), lambda bi, g, qi: (qi, 0))
    tab_k = pl.BlockSpec((t, LANES), lambda bi, g, qi: (0, 0), pipeline_mode=pl.Buffered(1))
    gain = pl.BlockSpec((1, LANES), lambda bi, g, qi: (0, 0))
    return pl.pallas_call(
        functools.partial(_gqa_attn_kernel, tk=tk, kprep=kprep, bounded=bounded),
        grid=(b, GQA_KV_HEADS, t // tq),
        in_specs=[
            pl.BlockSpec((None, tq, gw), lambda bi, g, qi: (bi, qi, g)),
            pl.BlockSpec((None, t, GQA_HEAD_DIM), lambda bi, g, qi: (bi, 0, k_col0 + g)),
            pl.BlockSpec((None, t, GQA_HEAD_DIM), lambda bi, g, qi: (bi, 0, v_col0 + g)),
            tab_q, tab_q, tab_k, tab_k, gain, gain,
        ],
        out_specs=pl.BlockSpec((None, tq, gw), lambda bi, g, qi: (bi, qi, g)),
        out_shape=jax.ShapeDtypeStruct((b, t, GQA_Q_HEADS * GQA_HEAD_DIM), BF16),
        scratch_shapes=[pltpu.VMEM((t, GQA_HEAD_DIM), BF16), pltpu.VMEM((GQA_HEAD_DIM, t), BF16)],
        compiler_params=_cparams("parallel", "parallel", "arbitrary"),
    )(qkv, qkv, qkv, cos, sin, cos, sin, q_gain.reshape(1, LANES), k_gain.reshape(1, LANES))


def _mla_attn_kernel(q_ref, kv_ref, kr_ref, cq_ref, sq_ref, ck_ref, sk_ref, qg_ref, kng_ref, krg_ref,
                     o_ref, kn_ref, vt_ref, *, tk, kprep, bounded):
    n_keys = kv_ref.shape[0]

    @pl.when(pl.program_id(2) == 0)
    def _():
        def prep(c, _):
            rows = pl.ds(pl.multiple_of(c * kprep, kprep), kprep)
            kn = _rms(kv_ref[rows, 0:MLA_NOPE_DIM].astype(F32), kng_ref[...], MLA_NOPE_DIM)
            kn_ref[rows, 0:MLA_NOPE_DIM] = kn.astype(BF16)
            kr = _rms(kr_ref[rows, :].astype(F32), krg_ref[...], MLA_ROPE_DIM)
            kn_ref[rows, MLA_NOPE_DIM:MLA_QPAD] = _rope(kr, ck_ref[rows, :], sk_ref[rows, :]).astype(BF16)
            return 0
        lax.fori_loop(0, n_keys // kprep, prep, 0)
        _store_vt(vt_ref, lambda r0, nr: kv_ref[r0:r0 + nr, MLA_NOPE_DIM:MLA_QPAD], n_keys, kprep)

    scale = MLA_QK_DIM ** -0.5 * LOG2E
    q = q_ref[...].astype(F32)
    g = qg_ref[...]
    inv = lax.rsqrt(jnp.sum(q * q, axis=-1, keepdims=True) * (1.0 / MLA_QK_DIM) + NORM_EPS)
    q_nope = q[:, 0:MLA_NOPE_DIM] * inv * g[:, 0:MLA_NOPE_DIM]
    q_rot = _rope(q[:, MLA_NOPE_DIM:MLA_QPAD] * inv * g[:, MLA_NOPE_DIM:MLA_QPAD], cq_ref[...], sq_ref[...])
    q_t = jnp.concatenate([(q_nope * scale).T, (q_rot * scale).T], axis=0).astype(BF16)
    out_t = _flash_t(q_t, kn_ref, vt_ref, n_keys, tk, bounded)
    o_ref[...] = out_t.T.astype(o_ref.dtype)


def _mla_attention(q, kv, down, cos, sin, q_gain, kn_gain, kr_gain, *, bounded):
    b, t, _ = q.shape
    tq, tk = _attention_tiles(t, 1, long_seq_score_bytes=8 * 1024 * 1024)
    kprep = min(512, t)
    kr_col = (MLA_Q_RANK + MLA_KV_RANK) // LANES
    tab_q = pl.BlockSpec((tq, LANES), lambda bi, h, qi: (qi, 0))
    tab_k = pl.BlockSpec((t, LANES), lambda bi, h, qi: (0, 0), pipeline_mode=pl.Buffered(1))
    return pl.pallas_call(
        functools.partial(_mla_attn_kernel, tk=tk, kprep=kprep, bounded=bounded),
        grid=(b, MLA_HEADS, t // tq),
        in_specs=[
            pl.BlockSpec((None, tq, MLA_QPAD), lambda bi, h, qi: (bi, qi, h)),
            pl.BlockSpec((None, t, MLA_QPAD), lambda bi, h, qi: (bi, 0, h)),
            pl.BlockSpec((None, t, LANES), lambda bi, h, qi: (bi, 0, kr_col)),
            tab_q, tab_q, tab_k, tab_k,
            pl.BlockSpec((1, MLA_QPAD), lambda bi, h, qi: (0, 0)),
            pl.BlockSpec((1, LANES), lambda bi, h, qi: (0, 0)),
            pl.BlockSpec((1, LANES), lambda bi, h, qi: (0, 0)),
        ],
        out_specs=pl.BlockSpec((None, tq, MLA_V_DIM), lambda bi, h, qi: (bi, qi, h)),
        out_shape=jax.ShapeDtypeStruct((b, t, MLA_HEADS * MLA_V_DIM), BF16),
        scratch_shapes=[pltpu.VMEM((t, MLA_QPAD), BF16), pltpu.VMEM((MLA_V_DIM, t), BF16)],
        compiler_params=_cparams("parallel", "parallel", "arbitrary"),
    )(q, kv, down, cos, sin, cos, sin, q_gain.reshape(1, MLA_QPAD), kn_gain.reshape(1, LANES),
      kr_gain.reshape(1, LANES))


def _split3(x):
    hi = x.astype(BF16)
    r1 = x - hi.astype(F32)
    mid = r1.astype(BF16)
    lo = (r1 - mid.astype(F32)).astype(BF16)
    return hi, mid, lo


def _gla_kernel(*refs, reverse, n_chunks):
    if reverse:
        (q_ref, k_ref, v_ref, low_ref, wg_ref, bg_ref, of_ref, r_ref, on_ref, o_ref, s_ref) = refs
    else:
        (q_ref, k_ref, v_ref, low_ref, wg_ref, bg_ref, o_ref, s_ref) = refs
    L = GLA_BLOCK

    @pl.when(pl.program_id(2) == 0)
    def _():
        s_ref[...] = jnp.zeros_like(s_ref)

    row = lax.broadcasted_iota(jnp.int32, (L, L), 0)
    col = lax.broadcasted_iota(jnp.int32, (L, L), 1)
    keep = (col >= row) if reverse else (col <= row)
    tri = jnp.where(keep, 1.0, 0.0).astype(BF16)
    tn_dims = (((0,), (0,)), ((), ()))
    nt_dims = (((1,), (1,)), ((), ()))
    mid = L // 2 if reverse else L // 2 - 1
    edge = 0 if reverse else L - 1

    order = list(range(n_chunks - 1, -1, -1) if reverse else range(n_chunks))
    rows = {c: slice(c * L, (c + 1) * L) for c in order}

    z = jnp.dot(low_ref[...].astype(BF16), wg_ref[...], preferred_element_type=F32) + bg_ref[...]
    cum = {}
    for c in order:
        zc = z[rows[c], :]
        la = (jnp.minimum(zc, 0.0) - jnp.log1p(jnp.exp(-jnp.abs(zc)))) * (1.0 / GLA_GATE_TAU)
        cum[c] = sum(jnp.dot(tri, p, preferred_element_type=F32) for p in _split3(la))
    qd, qs, kd, ke, de = {}, {}, {}, {}, {}
    for c in order:
        ref_row = cum[c][mid:mid + 1, :]
        tot_row = cum[c][edge:edge + 1, :]
        q = q_ref[rows[c], :].astype(F32) * (GLA_DK_HEAD ** -0.5)
        k = k_ref[rows[c], :].astype(F32)
        q_mid = q * jnp.exp(cum[c] - ref_row)
        qd[c] = q_mid.astype(BF16)
        qs[c] = (q_mid * jnp.exp(ref_row)).astype(BF16)
        kd[c] = (k * jnp.exp(ref_row - cum[c])).astype(BF16)
        ke[c] = (k * jnp.exp(tot_row - cum[c])).astype(BF16)
        col_de = jnp.broadcast_to(jnp.exp(tot_row), (LANES, GLA_DK_HEAD)).T
        de[c] = jnp.concatenate([col_de] * (GLA_DV_HEAD // LANES), axis=1)
    a = {c: jnp.where(keep, lax.dot_general(qd[c], kd[c], nt_dims, preferred_element_type=F32), 0.0).astype(BF16)
         for c in order}
    o_intra = {c: jnp.dot(a[c], v_ref[rows[c], :], preferred_element_type=F32) for c in order}
    kv = {c: lax.dot_general(ke[c], v_ref[rows[c], :], tn_dims, preferred_element_type=F32) for c in order}

    s = s_ref[...]
    for c in order:
        o = o_intra[c] + jnp.dot(qs[c], s.astype(BF16), preferred_element_type=F32)
        s = de[c] * s + kv[c]
        if reverse:
            o = o + of_ref[rows[c], :]
            o = _rms(o, on_ref[...], GLA_DV_HEAD)
            r = r_ref[rows[c], :].astype(F32)
            o = o * (r * (1.0 / (1.0 + jnp.exp(-r))))
        o_ref[rows[c], :] = o.astype(o_ref.dtype)
    s_ref[...] = s


def _gla_direction(proj, low, wg, bg, o_fwd, o_norm, reverse):
    b, t, _ = proj.shape
    tb = min(1024, t)
    assert t % tb == 0 and tb % GLA_BLOCK == 0
    nblk = t // tb
    blk = (lambda i: nblk - 1 - i) if reverse else (lambda i: i)
    k_col0 = GLA_DK // GLA_DK_HEAD
    v_col0 = 2 * GLA_DK // GLA_DV_HEAD
    r_col0 = (2 * GLA_DK + GLA_DV) // GLA_DV_HEAD
    in_specs = [
        pl.BlockSpec((None, tb, GLA_DK_HEAD), lambda bi, h, i: (bi, blk(i), h)),
        pl.BlockSpec((None, tb, GLA_DK_HEAD), lambda bi, h, i: (bi, blk(i), k_col0 + h)),
        pl.BlockSpec((None, tb, GLA_DV_HEAD), lambda bi, h, i: (bi, blk(i), v_col0 + h)),
        pl.BlockSpec((None, tb, LANES), lambda bi, h, i: (bi, blk(i), 0)),
        pl.BlockSpec((LANES, GLA_DK_HEAD), lambda bi, h, i: (0, h)),
        pl.BlockSpec((1, GLA_DK_HEAD), lambda bi, h, i: (0, h)),
    ]
    args = [proj, proj, proj, low, wg, bg]
    if reverse:
        in_specs += [
            pl.BlockSpec((None, tb, GLA_DV_HEAD), lambda bi, h, i: (bi, blk(i), h)),
            pl.BlockSpec((None, tb, GLA_DV_HEAD), lambda bi, h, i: (bi, blk(i), r_col0 + h)),
            pl.BlockSpec((1, GLA_DV_HEAD), lambda bi, h, i: (0, 0)),
        ]
        args += [o_fwd, proj, o_norm.reshape(1, GLA_DV_HEAD)]
    return pl.pallas_call(
        functools.partial(_gla_kernel, reverse=reverse, n_chunks=tb // GLA_BLOCK),
        grid=(b, GLA_HEADS, nblk),
        in_specs=in_specs,
        out_specs=pl.BlockSpec((None, tb, GLA_DV_HEAD), lambda bi, h, i: (bi, blk(i), h)),
        out_shape=jax.ShapeDtypeStruct((b, t, GLA_DV), BF16 if reverse else F32),
        scratch_shapes=[pltpu.VMEM((GLA_DK_HEAD, GLA_DV_HEAD), F32)],
        compiler_params=_cparams("parallel", "parallel", "arbitrary"),
    )(*args)


def _prep_gqa(w_qkv, q_norm, k_norm):
    n_rot = GQA_Q_HEADS + GQA_KV_HEADS
    cols = (np.arange(n_rot)[:, None] * GQA_HEAD_DIM + _HALF_SPLIT_128[None, :]).reshape(-1)
    cols = np.concatenate([cols, np.arange(n_rot * GQA_HEAD_DIM, w_qkv.shape[1])])
    return w_qkv[:, cols].astype(BF16), q_norm[_HALF_SPLIT_128], k_norm[_HALF_SPLIT_128]


def _prep_mla(w_down, w_uq, q_norm, k_rope_norm):
    qidx, qmsk = _mla_q_layout()
    cols = (np.arange(MLA_HEADS)[:, None] * MLA_QK_DIM + qidx[None, :]).reshape(-1)
    w_uq_p = (w_uq[:, cols] * jnp.asarray(np.tile(qmsk, MLA_HEADS))).astype(BF16)
    q_gain = q_norm[qidx] * jnp.asarray(qmsk)
    ridx, rmsk = _mla_rot_layout(0)
    kr0 = MLA_Q_RANK + MLA_KV_RANK
    w_kr = w_down[:, kr0 + ridx] * jnp.asarray(rmsk)
    w_down_p = jnp.concatenate([w_down[:, :kr0], w_kr], axis=1).astype(BF16)
    kr_gain = k_rope_norm[ridx] * jnp.asarray(rmsk)
    return w_down_p, w_uq_p, q_gain, kr_gain


def _prep_gla_gates(w_g1, w_g2):
    r = GLA_GATE_RANK
    d = w_g1.shape[1]
    w1 = jnp.zeros((d, LANES), F32).at[:, 0:r].set(w_g1[0]).at[:, r:2 * r].set(w_g1[1])
    w2 = jnp.zeros((2, LANES, GLA_DK), F32).at[0, 0:r].set(w_g2[0]).at[1, r:2 * r].set(w_g2[1])
    return w1.astype(BF16), w2.astype(BF16)


def _trunk(x3, mod, p):
    b, t, d = x3.shape
    n = b * t
    x = x3.reshape(n, d)
    for i in range(DEPTH):
        m = mod[i].reshape(b, 6, 1, d)
        sh1, sc1, g1, sh2, sc2, g2 = (m[:, s] for s in range(6))
        kind, j = i % N_MIXERS, i // N_MIXERS
        if kind == 0:
            qkv = _norm_mod_matmul(x, p['norm1_g'][i], sc1, sh1, p['gqa_w_qkv'][j], t, BF16)
            cos, sin = _gqa_rope_tables(t)
            a = _dispatch_on_score_bound(_gqa_score_bound(p['gqa_q_norm'][j], p['gqa_k_norm'][j]), _gqa_attention,
                                         qkv.reshape(b, t, -1), cos, sin, p['gqa_q_norm'][j], p['gqa_k_norm'][j])
            x = _oproj_residual(a.reshape(n, -1), p['gqa_w_o'][j], x, g1, t)
        elif kind == 1:
            proj = _norm_mod_matmul(x, p['norm1_g'][i], sc1, sh1, p['gla_w_in'][j], t, BF16, tn=2048)
            low = _norm_mod_matmul(x, p['norm1_g'][i], sc1, sh1, p['gla_w_g1'][j], t, F32)
            proj3, low3 = proj.reshape(b, t, -1), low.reshape(b, t, -1)
            bg = p['gla_b_g'][j]
            o_f = _gla_direction(proj3, low3, p['gla_w_g2'][j][0], bg[0:1], None, None, reverse=False)
            a = _gla_direction(proj3, low3, p['gla_w_g2'][j][1], bg[1:2], o_f, p['gla_o_norm'][j], reverse=True)
            x = _oproj_residual(a.reshape(n, -1), p['gla_w_o'][j], x, g1, t)
        else:
            down = _norm_mod_matmul(x, p['norm1_g'][i], sc1, sh1, p['mla_w_down'][j], t, F32)
            q = _norm_matmul(down, 0, MLA_Q_RANK, p['mla_q_a_norm'][j], p['mla_w_uq'][j], t, BF16)
            kv = _norm_matmul(down, MLA_Q_RANK // MLA_KV_RANK, MLA_KV_RANK, p['mla_kv_a_norm'][j],
                              p['mla_w_ukv'][j], t, BF16)
            cos, sin = _mla_rope_tables(t)
            gains = (p['mla_q_gain'][j], p['mla_k_nope_norm'][j], p['mla_kr_gain'][j])
            a = _dispatch_on_score_bound(_mla_score_bound(*gains), _mla_attention, q.reshape(b, t, -1),
                                         kv.reshape(b, t, -1), down.reshape(b, t, -1), cos, sin, *gains)
            x = _oproj_residual(a.reshape(n, -1), p['mla_w_o'][j], x, g1, t)
        x = _mlp_residual(x, p['norm2_g'][i], sc2, sh2, g2, p['mlp_w1'][i], p['mlp_w2'][i], t)
    return x.reshape(b, t, d)


def kernel(x_prompt, x_sample, c_prompt, c_sample, norm1_g, norm2_g, ada_w, ada_b, gqa_w_qkv, gqa_q_norm, gqa_k_norm, gqa_w_o, gla_w_in, gla_w_g1, gla_w_g2, gla_b_g, gla_o_norm, gla_w_o, mla_w_down, mla_q_a_norm, mla_kv_a_norm, mla_w_uq, mla_w_ukv, mla_q_norm, mla_k_nope_norm, mla_k_rope_norm, mla_w_o, mlp_w1, mlp_w2):
    nA, nB, nC = gqa_w_qkv.shape[0], gla_w_in.shape[0], mla_w_down.shape[0]
    gqa = [_prep_gqa(gqa_w_qkv[j], gqa_q_norm[j], gqa_k_norm[j]) for j in range(nA)]
    mla = [_prep_mla(mla_w_down[j], mla_w_uq[j], mla_q_norm[j], mla_k_rope_norm[j]) for j in range(nC)]
    gates = [_prep_gla_gates(gla_w_g1[j], gla_w_g2[j]) for j in range(nB)]
    p = dict(
        norm1_g=norm1_g, norm2_g=norm2_g,
        gqa_w_qkv=[g[0] for g in gqa], gqa_q_norm=[g[1] for g in gqa], gqa_k_norm=[g[2] for g in gqa],
        gqa_w_o=gqa_w_o.astype(BF16),
        gla_w_in=gla_w_in.astype(BF16), gla_w_g1=[g[0] for g in gates], gla_w_g2=[g[1] for g in gates],
        gla_b_g=gla_b_g, gla_o_norm=gla_o_norm, gla_w_o=gla_w_o.astype(BF16),
        mla_w_down=[m[0] for m in mla], mla_w_uq=[m[1] for m in mla], mla_q_gain=[m[2] for m in mla],
        mla_kr_gain=[m[3] for m in mla], mla_q_a_norm=mla_q_a_norm, mla_kv_a_norm=mla_kv_a_norm,
        mla_w_ukv=mla_w_ukv.astype(BF16), mla_k_nope_norm=mla_k_nope_norm, mla_w_o=mla_w_o.astype(BF16),
        mlp_w1=mlp_w1.astype(BF16), mlp_w2=mlp_w2.astype(BF16),
    )
    bp, bs = c_prompt.shape[0], c_sample.shape[0]
    nb_pad = -(-(bp + bs) // 16) * 16
    c_all = jnp.concatenate([c_prompt, c_sample, jnp.zeros((nb_pad - bp - bs, c_prompt.shape[1]), F32)], axis=0)
    mod = _ada_modulation(c_all, ada_w, ada_b)
    y_prompt = _trunk(x_prompt, mod[:, :bp], p)
    y_sample = _trunk(x_sample, mod[:, bp:bp + bs], p)
    return (y_prompt, y_sample)
```

```python
import functools
import math

import numpy as np
import jax
import jax.numpy as jnp
from jax import lax
from jax.experimental import pallas as pl
from jax.experimental.pallas import tpu as pltpu

F32 = jnp.float32
BF16 = jnp.bfloat16

D_MODEL = 2048
DEPTH = 4
GRID_W = 64
ROPE_THETA = 10000.0
NORM_EPS = 1e-6
N_MIXERS = 3

GQA_HEAD_DIM = 128
GQA_Q_HEADS = 16
GQA_KV_HEADS = 4
GQA_GROUP = 4

GLA_HEADS = 4
GLA_DK = 1024
GLA_DV = 2048
GLA_DK_HEAD = 256
GLA_DV_HEAD = 512
GLA_GATE_RANK = 16
GLA_GATE_TAU = 16.0
GLA_BLOCK = 128

MLA_HEADS = 16
MLA_Q_RANK = 512
MLA_KV_RANK = 256
MLA_NOPE_DIM = 128
MLA_ROPE_DIM = 64
MLA_QK_DIM = 192
MLA_V_DIM = 128
MLA_QPAD = 256
MLA_DOWN_PAD = 896

D_FF = 4 * D_MODEL

LANES = 128
VMEM_LIMIT_BYTES = 56 * 1024 * 1024


def _tiles(n_rows, seq):
    tm = min(512, seq)
    assert seq % tm == 0 and n_rows % tm == 0
    return tm


SHORT_SEQ = 2048


def _attention_tiles(t, heads_per_step, long_seq_score_bytes):
    short = t <= SHORT_SEQ
    m_cols = 2048 if short else 1024
    score_bytes = 8 * 1024 * 1024 if short else long_seq_score_bytes
    tq = min(m_cols // heads_per_step, t)
    tk = min(score_bytes // (4 * m_cols), t)
    assert t % tq == 0 and t % tk == 0
    return tq, tk


def _cparams(*sem):
    return pltpu.CompilerParams(dimension_semantics=sem, vmem_limit_bytes=VMEM_LIMIT_BYTES)


def _ada_kernel(c_ref, w_ref, b_ref, o_ref):
    c = c_ref[...]
    cond = (c * (1.0 / (1.0 + jnp.exp(-c)))).astype(BF16)
    o_ref[...] = jnp.dot(cond, w_ref[...].astype(BF16), preferred_element_type=F32) + b_ref[...]


def _ada_modulation(c_all, ada_w, ada_b):
    nb, d = c_all.shape
    depth, _, n6 = ada_w.shape
    tn = 1024
    return pl.pallas_call(
        _ada_kernel,
        grid=(depth, n6 // tn),
        in_specs=[
            pl.BlockSpec((nb, d), lambda i, j: (0, 0)),
            pl.BlockSpec((None, d, tn), lambda i, j: (i, 0, j)),
            pl.BlockSpec((None, 1, tn), lambda i, j: (i, 0, j)),
        ],
        out_specs=pl.BlockSpec((None, nb, tn), lambda i, j: (i, 0, j)),
        out_shape=jax.ShapeDtypeStruct((depth, nb, n6), F32),
        compiler_params=_cparams("parallel", "parallel"),
    )(c_all, ada_w, ada_b.reshape(depth, 1, n6))


def _rms(x, g, n):
    ms = jnp.sum(x * x, axis=-1, keepdims=True) * (1.0 / n)
    return x * lax.rsqrt(ms + NORM_EPS) * g


NORM_ROWS = 16


def _store_norm(h_ref, x_ref, gain, shift):
    d = x_ref.shape[-1]
    blocks = [slice(r, r + NORM_ROWS) for r in range(0, x_ref.shape[0], NORM_ROWS)]
    inv = []
    for rows in blocks:
        x = x_ref[rows, :].astype(F32)
        inv.append(lax.rsqrt(jnp.sum(x * x, axis=-1, keepdims=True) * (1.0 / d) + NORM_EPS))
    for rows, inv_r in zip(blocks, inv):
        y = x_ref[rows, :].astype(F32) * inv_r * gain
        h_ref[rows, :] = (y if shift is None else y + shift).astype(BF16)


def _norm_mod_matmul_kernel(x_ref, g_ref, sc_ref, sh_ref, w_ref, o_ref, h_ref):
    @pl.when(pl.program_id(1) == 0)
    def _():
        _store_norm(h_ref, x_ref, g_ref[...] * (1.0 + sc_ref[...]), sh_ref[...])

    o_ref[...] = jnp.dot(h_ref[...], w_ref[...], preferred_element_type=F32).astype(o_ref.dtype)


def _norm_matmul_kernel(x_ref, g_ref, w_ref, o_ref, h_ref):
    @pl.when(pl.program_id(1) == 0)
    def _():
        _store_norm(h_ref, x_ref, g_ref[...], None)

    o_ref[...] = jnp.dot(h_ref[...], w_ref[...], preferred_element_type=F32).astype(o_ref.dtype)


def _norm_mod_matmul(x, g, sc, sh, w, seq, out_dtype, tn=None):
    n, d = x.shape
    nout = w.shape[1]
    tm = _tiles(n, seq)
    tn = nout if tn is None else tn
    tpb = seq // tm
    n_j = nout // tn
    return pl.pallas_call(
        _norm_mod_matmul_kernel,
        grid=(n // tm, n_j),
        in_specs=[
            pl.BlockSpec((tm, d), lambda i, j: (i, 0)),
            pl.BlockSpec((1, d), lambda i, j: (0, 0)),
            pl.BlockSpec((None, 1, d), lambda i, j: (i // tpb, 0, 0)),
            pl.BlockSpec((None, 1, d), lambda i, j: (i // tpb, 0, 0)),
            pl.BlockSpec((d, tn), lambda i, j: (0, _serpentine(i, j, n_j))),
        ],
        out_specs=pl.BlockSpec((tm, tn), lambda i, j: (i, _serpentine(i, j, n_j))),
        out_shape=jax.ShapeDtypeStruct((n, nout), out_dtype),
        scratch_shapes=[pltpu.VMEM((tm, d), BF16)],
        compiler_params=_cparams("parallel", "arbitrary"),
    )(x, g.reshape(1, d), sc, sh, w)


def _norm_matmul(x, col_block, width, g, w, seq, out_dtype):
    n = x.shape[0]
    nout = w.shape[1]
    tm = _tiles(n, seq)
    return pl.pallas_call(
        _norm_matmul_kernel,
        grid=(n // tm, 1),
        in_specs=[
            pl.BlockSpec((tm, width), lambda i, j: (i, col_block)),
            pl.BlockSpec((1, width), lambda i, j: (0, 0)),
            pl.BlockSpec((width, nout), lambda i, j: (0, 0)),
        ],
        out_specs=pl.BlockSpec((tm, nout), lambda i, j: (i, 0)),
        out_shape=jax.ShapeDtypeStruct((n, nout), out_dtype),
        scratch_shapes=[pltpu.VMEM((tm, width), BF16)],
        compiler_params=_cparams("parallel", "arbitrary"),
    )(x, g.reshape(1, width), w)


def _oproj_kernel(a_ref, w_ref, x_ref, gate_ref, o_ref):
    y = jnp.dot(a_ref[...], w_ref[...], preferred_element_type=F32)
    o_ref[...] = x_ref[...] + gate_ref[...] * y


def _oproj_residual(a, w, x, gate, seq):
    n, k = a.shape
    d = w.shape[1]
    tm = _tiles(n, seq)
    tpb = seq // tm
    return pl.pallas_call(
        _oproj_kernel,
        grid=(n // tm,),
        in_specs=[
            pl.BlockSpec((tm, k), lambda i: (i, 0)),
            pl.BlockSpec((k, d), lambda i: (0, 0)),
            pl.BlockSpec((tm, d), lambda i: (i, 0)),
            pl.BlockSpec((None, 1, d), lambda i: (i // tpb, 0, 0)),
        ],
        out_specs=pl.BlockSpec((tm, d), lambda i: (i, 0)),
        out_shape=jax.ShapeDtypeStruct((n, d), F32),
        compiler_params=_cparams("parallel"),
    )(a, w, x, gate)


def _mlp_kernel(x_ref, g_ref, sc_ref, sh_ref, gate_ref, w1_ref, w2_ref, o_ref, h_ref):
    j = pl.program_id(1)
    last = pl.num_programs(1) - 1

    def hidden():
        a = jnp.maximum(jnp.dot(h_ref[...], w1_ref[...], preferred_element_type=F32), 0.0)
        return jnp.dot((a * a).astype(BF16), w2_ref[...], preferred_element_type=F32)

    @pl.when(j == 0)
    def _():
        _store_norm(h_ref, x_ref, g_ref[...] * (1.0 + sc_ref[...]), sh_ref[...])
        o_ref[...] = hidden()

    @pl.when((j > 0) & (j < last))
    def _():
        o_ref[...] += hidden()

    @pl.when(j == last)
    def _():
        o_ref[...] = x_ref[...] + gate_ref[...] * (o_ref[...] + hidden())


def _serpentine(i, j, n_j):
    return jnp.where(lax.rem(i, 2) == 0, j, n_j - 1 - j)


def _mlp_residual(x, g, sc, sh, gate, w1, w2, seq):
    n, d = x.shape
    dff = w1.shape[1]
    tm = _tiles(n, seq)
    tf = min(2048, dff)
    assert dff % tf == 0 and dff // tf >= 2
    n_j = dff // tf
    tpb = seq // tm
    mod_spec = pl.BlockSpec((None, 1, d), lambda i, j: (i // tpb, 0, 0))
    return pl.pallas_call(
        _mlp_kernel,
        grid=(n // tm, n_j),
        in_specs=[
            pl.BlockSpec((tm, d), lambda i, j: (i, 0)),
            pl.BlockSpec((1, d), lambda i, j: (0, 0)),
            mod_spec, mod_spec, mod_spec,
            pl.BlockSpec((d, tf), lambda i, j: (0, _serpentine(i, j, n_j))),
            pl.BlockSpec((tf, d), lambda i, j: (_serpentine(i, j, n_j), 0)),
        ],
        out_specs=pl.BlockSpec((tm, d), lambda i, j: (i, 0)),
        out_shape=jax.ShapeDtypeStruct((n, d), F32),
        scratch_shapes=[pltpu.VMEM((tm, d), BF16)],
        compiler_params=_cparams("parallel", "arbitrary"),
    )(x, g.reshape(1, d), sc, sh, gate, w1, w2)


def _axial_angles(n_tok, rot_dim):
    rows = n_tok // GRID_W
    row = jnp.repeat(jnp.arange(rows, dtype=F32), GRID_W)
    col = jnp.tile(jnp.arange(GRID_W, dtype=F32), rows)
    n_freq = rot_dim // 4
    inv = ROPE_THETA ** (-jnp.arange(n_freq, dtype=F32) / n_freq)
    ang = jnp.concatenate([row[:, None] * inv, col[:, None] * inv], axis=-1)
    return jnp.cos(ang), jnp.sin(ang)


def _gqa_rope_tables(n_tok):
    cos, sin = _axial_angles(n_tok, GQA_HEAD_DIM)
    return jnp.concatenate([cos, cos], -1), jnp.concatenate([-sin, sin], -1)


def _mla_rope_tables(n_tok):
    cos, sin = _axial_angles(n_tok, MLA_ROPE_DIM)
    z = jnp.zeros_like(cos)
    return jnp.concatenate([cos, z, cos, z], -1), jnp.concatenate([-sin, z, sin, z], -1)


_HALF_SPLIT_128 = np.concatenate([np.arange(0, 128, 2), np.arange(1, 128, 2)])


def _mla_rot_layout(base):
    idx = np.zeros(128, np.int32)
    msk = np.zeros(128, np.float32)
    idx[0:32] = base + np.arange(0, 64, 2)
    idx[64:96] = base + np.arange(1, 64, 2)
    msk[0:32] = 1.0
    msk[64:96] = 1.0
    return idx, msk


def _mla_q_layout():
    ridx, rmsk = _mla_rot_layout(MLA_NOPE_DIM)
    idx = np.concatenate([np.arange(MLA_NOPE_DIM), ridx])
    msk = np.concatenate([np.ones(MLA_NOPE_DIM, np.float32), rmsk])
    return idx, msk


def _rope(x, cos, sin):
    return x * cos + pltpu.roll(x, 64, 1) * sin


LOG2E = math.log2(math.e)


MAX_UNSHIFTED_SCORE = 40.0


def _flash_t(q_t, kn_ref, vt_ref, n_keys, tk, bounded):
    m_cols = q_t.shape[1]
    m = jnp.full((1, m_cols), -jnp.inf, F32)
    l8 = jnp.zeros((8, m_cols), F32)
    acc = jnp.zeros((vt_ref.shape[0], m_cols), F32)
    for c in range(n_keys // tk):
        s = jnp.dot(kn_ref[c * tk:(c + 1) * tk, :], q_t, preferred_element_type=F32)
        if bounded:
            p = jnp.exp2(s)
            l8 = l8 + jnp.sum(p.reshape(tk // 8, 8, m_cols), axis=0)
            acc = acc + jnp.dot(vt_ref[:, c * tk:(c + 1) * tk], p.astype(BF16), preferred_element_type=F32)
        else:
            m_new = jnp.maximum(m, jnp.max(s, axis=0, keepdims=True))
            alpha = jnp.exp2(m - m_new)
            p = jnp.exp2(s - m_new)
            l8 = alpha * l8 + jnp.sum(p.reshape(tk // 8, 8, m_cols), axis=0)
            acc = alpha * acc + jnp.dot(vt_ref[:, c * tk:(c + 1) * tk], p.astype(BF16),
                                        preferred_element_type=F32)
            m = m_new
    return acc * (1.0 / jnp.sum(l8, axis=0, keepdims=True))


def _store_vt(vt_ref, v_load, n_keys, blk):
    for c in range(n_keys // blk):
        vt_ref[:, c * blk:(c + 1) * blk] = v_load(c * blk, blk).astype(F32).T.astype(BF16)


def _gqa_attn_kernel(q_ref, k_ref, v_ref, cq_ref, sq_ref, ck_ref, sk_ref, qg_ref, kg_ref, o_ref, kn_ref, vt_ref,
                     *, tk, kprep, bounded):
    n_keys = k_ref.shape[0]
    tq = q_ref.shape[0]

    @pl.when(pl.program_id(2) == 0)
    def _():
        def prep(c, _):
            rows = pl.ds(pl.multiple_of(c * kprep, kprep), kprep)
            k = _rms(k_ref[rows, :].astype(F32), kg_ref[...], GQA_HEAD_DIM)
            kn_ref[rows, :] = _rope(k, ck_ref[rows, :], sk_ref[rows, :]).astype(BF16)
            return 0
        lax.fori_loop(0, n_keys // kprep, prep, 0)
        _store_vt(vt_ref, lambda r0, nr: v_ref[r0:r0 + nr, :], n_keys, kprep)

    scale = GQA_HEAD_DIM ** -0.5 * LOG2E
    qs = []
    for h in range(GQA_GROUP):
        q = _rms(q_ref[:, h * GQA_HEAD_DIM:(h + 1) * GQA_HEAD_DIM].astype(F32), qg_ref[...], GQA_HEAD_DIM)
        qs.append((_rope(q, cq_ref[...], sq_ref[...]) * scale).T.astype(BF16))
    q_t = jnp.concatenate(qs, axis=1)
    out_t = _flash_t(q_t, kn_ref, vt_ref, n_keys, tk, bounded)
    for h in range(GQA_GROUP):
        o_ref[:, h * GQA_HEAD_DIM:(h + 1) * GQA_HEAD_DIM] = out_t[:, h * tq:(h + 1) * tq].T.astype(o_ref.dtype)


def _dispatch_on_score_bound(bound, attention, *args):
    return lax.cond(bound <= MAX_UNSHIFTED_SCORE,
                    functools.partial(attention, bounded=True),
                    functools.partial(attention, bounded=False), *args)


def _gqa_score_bound(q_gain, k_gain):
    return (GQA_HEAD_DIM * jnp.max(jnp.abs(q_gain)) * jnp.max(jnp.abs(k_gain))) * (GQA_HEAD_DIM ** -0.5 * LOG2E)


def _mla_score_bound(q_gain, kn_gain, kr_gain):
    q_norm = math.sqrt(MLA_QK_DIM) * jnp.max(jnp.abs(q_gain))
    k_norm = jnp.sqrt(MLA_NOPE_DIM * jnp.max(kn_gain * kn_gain) + MLA_ROPE_DIM * jnp.max(kr_gain * kr_gain))
    return q_norm * k_norm * (MLA_QK_DIM ** -0.5 * LOG2E)


def _gqa_attention(qkv, cos, sin, q_gain, k_gain, *, bounded):
    b, t, _ = qkv.shape
    tq, tk = _attention_tiles(t, GQA_GROUP, long_seq_score_bytes=16 * 1024 * 1024)
    kprep = min(512, t)
    gw = GQA_GROUP * GQA_HEAD_DIM
    k_col0 = GQA_Q_HEADS
    v_col0 = GQA_Q_HEADS + GQA_KV_HEADS
    tab_q = pl.BlockSpec((tq, LANES), lambda bi, g, qi: (qi, 0))
    tab_k = pl.BlockSpec((t, LANES), lambda bi, g, qi: (0, 0), pipeline_mode=pl.Buffered(1))
    gain = pl.BlockSpec((1, LANES), lambda bi, g, qi: (0, 0))
    return pl.pallas_call(
        functools.partial(_gqa_attn_kernel, tk=tk, kprep=kprep, bounded=bounded),
        grid=(b, GQA_KV_HEADS, t // tq),
        in_specs=[
            pl.BlockSpec((None, tq, gw), lambda bi, g, qi: (bi, qi, g)),
            pl.BlockSpec((None, t, GQA_HEAD_DIM), lambda bi, g, qi: (bi, 0, k_col0 + g)),
            pl.BlockSpec((None, t, GQA_HEAD_DIM), lambda bi, g, qi: (bi, 0, v_col0 + g)),
            tab_q, tab_q, tab_k, tab_k, gain, gain,
        ],
        out_specs=pl.BlockSpec((None, tq, gw), lambda bi, g, qi: (bi, qi, g)),
        out_shape=jax.ShapeDtypeStruct((b, t, GQA_Q_HEADS * GQA_HEAD_DIM), BF16),
        scratch_shapes=[pltpu.VMEM((t, GQA_HEAD_DIM), BF16), pltpu.VMEM((GQA_HEAD_DIM, t), BF16)],
        compiler_params=_cparams("parallel", "parallel", "arbitrary"),
    )(qkv, qkv, qkv, cos, sin, cos, sin, q_gain.reshape(1, LANES), k_gain.reshape(1, LANES))


def _mla_attn_kernel(q_ref, kv_ref, kr_ref, cq_ref, sq_ref, ck_ref, sk_ref, qg_ref, kng_ref, krg_ref,
                     o_ref, kn_ref, vt_ref, *, tk, kprep, bounded):
    n_keys = kv_ref.shape[0]

    @pl.when(pl.program_id(2) == 0)
    def _():
        def prep(c, _):
            rows = pl.ds(pl.multiple_of(c * kprep, kprep), kprep)
            kn = _rms(kv_ref[rows, 0:MLA_NOPE_DIM].astype(F32), kng_ref[...], MLA_NOPE_DIM)
            kn_ref[rows, 0:MLA_NOPE_DIM] = kn.astype(BF16)
            kr = _rms(kr_ref[rows, :].astype(F32), krg_ref[...], MLA_ROPE_DIM)
            kn_ref[rows, MLA_NOPE_DIM:MLA_QPAD] = _rope(kr, ck_ref[rows, :], sk_ref[rows, :]).astype(BF16)
            return 0
        lax.fori_loop(0, n_keys // kprep, prep, 0)
        _store_vt(vt_ref, lambda r0, nr: kv_ref[r0:r0 + nr, MLA_NOPE_DIM:MLA_QPAD], n_keys, kprep)

    scale = MLA_QK_DIM ** -0.5 * LOG2E
    q = q_ref[...].astype(F32)
    g = qg_ref[...]
    inv = lax.rsqrt(jnp.sum(q * q, axis=-1, keepdims=True) * (1.0 / MLA_QK_DIM) + NORM_EPS)
    q_nope = q[:, 0:MLA_NOPE_DIM] * inv * g[:, 0:MLA_NOPE_DIM]
    q_rot = _rope(q[:, MLA_NOPE_DIM:MLA_QPAD] * inv * g[:, MLA_NOPE_DIM:MLA_QPAD], cq_ref[...], sq_ref[...])
    q_t = jnp.concatenate([(q_nope * scale).T, (q_rot * scale).T], axis=0).astype(BF16)
    out_t = _flash_t(q_t, kn_ref, vt_ref, n_keys, tk, bounded)
    o_ref[...] = out_t.T.astype(o_ref.dtype)


def _mla_attention(q, kv, down, cos, sin, q_gain, kn_gain, kr_gain, *, bounded):
    b, t, _ = q.shape
    tq, tk = _attention_tiles(t, 1, long_seq_score_bytes=8 * 1024 * 1024)
    kprep = min(512, t)
    kr_col = (MLA_Q_RANK + MLA_KV_RANK) // LANES
    tab_q = pl.BlockSpec((tq, LANES), lambda bi, h, qi: (qi, 0))
    tab_k = pl.BlockSpec((t, LANES), lambda bi, h, qi: (0, 0), pipeline_mode=pl.Buffered(1))
    return pl.pallas_call(
        functools.partial(_mla_attn_kernel, tk=tk, kprep=kprep, bounded=bounded),
        grid=(b, MLA_HEADS, t // tq),
        in_specs=[
            pl.BlockSpec((None, tq, MLA_QPAD), lambda bi, h, qi: (bi, qi, h)),
            pl.BlockSpec((None, t, MLA_QPAD), lambda bi, h, qi: (bi, 0, h)),
            pl.BlockSpec((None, t, LANES), lambda bi, h, qi: (bi, 0, kr_col)),
            tab_q, tab_q, tab_k, tab_k,
            pl.BlockSpec((1, MLA_QPAD), lambda bi, h, qi: (0, 0)),
            pl.BlockSpec((1, LANES), lambda bi, h, qi: (0, 0)),
            pl.BlockSpec((1, LANES), lambda bi, h, qi: (0, 0)),
        ],
        out_specs=pl.BlockSpec((None, tq, MLA_V_DIM), lambda bi, h, qi: (bi, qi, h)),
        out_shape=jax.ShapeDtypeStruct((b, t, MLA_HEADS * MLA_V_DIM), BF16),
        scratch_shapes=[pltpu.VMEM((t, MLA_QPAD), BF16), pltpu.VMEM((MLA_V_DIM, t), BF16)],
        compiler_params=_cparams("parallel", "parallel", "arbitrary"),
    )(q, kv, down, cos, sin, cos, sin, q_gain.reshape(1, MLA_QPAD), kn_gain.reshape(1, LANES),
      kr_gain.reshape(1, LANES))


def _split3(x):
    hi = x.astype(BF16)
    r1 = x - hi.astype(F32)
    mid = r1.astype(BF16)
    lo = (r1 - mid.astype(F32)).astype(BF16)
    return hi, mid, lo


def _gla_kernel(*refs, reverse, n_chunks):
    if reverse:
        (q_ref, k_ref, v_ref, low_ref, wg_ref, bg_ref, of_ref, r_ref, on_ref, o_ref, s_ref) = refs
    else:
        (q_ref, k_ref, v_ref, low_ref, wg_ref, bg_ref, o_ref, s_ref) = refs
    L = GLA_BLOCK

    @pl.when(pl.program_id(2) == 0)
    def _():
        s_ref[...] = jnp.zeros_like(s_ref)

    row = lax.broadcasted_iota(jnp.int32, (L, L), 0)
    col = lax.broadcasted_iota(jnp.int32, (L, L), 1)
    keep = (col >= row) if reverse else (col <= row)
    tri = jnp.where(keep, 1.0, 0.0).astype(BF16)
    tn_dims = (((0,), (0,)), ((), ()))
    nt_dims = (((1,), (1,)), ((), ()))
    mid = L // 2 if reverse else L // 2 - 1
    edge = 0 if reverse else L - 1

    order = list(range(n_chunks - 1, -1, -1) if reverse else range(n_chunks))
    rows = {c: slice(c * L, (c + 1) * L) for c in order}

    z = jnp.dot(low_ref[...].astype(BF16), wg_ref[...], preferred_element_type=F32) + bg_ref[...]
    cum = {}
    for c in order:
        zc = z[rows[c], :]
        la = (jnp.minimum(zc, 0.0) - jnp.log1p(jnp.exp(-jnp.abs(zc)))) * (1.0 / GLA_GATE_TAU)
        cum[c] = sum(jnp.dot(tri, p, preferred_element_type=F32) for p in _split3(la))
    qd, qs, kd, ke, de = {}, {}, {}, {}, {}
    for c in order:
        ref_row = cum[c][mid:mid + 1, :]
        tot_row = cum[c][edge:edge + 1, :]
        q = q_ref[rows[c], :].astype(F32) * (GLA_DK_HEAD ** -0.5)
        k = k_ref[rows[c], :].astype(F32)
        q_mid = q * jnp.exp(cum[c] - ref_row)
        qd[c] = q_mid.astype(BF16)
        qs[c] = (q_mid * jnp.exp(ref_row)).astype(BF16)
        kd[c] = (k * jnp.exp(ref_row - cum[c])).astype(BF16)
        ke[c] = (k * jnp.exp(tot_row - cum[c])).astype(BF16)
        col_de = jnp.broadcast_to(jnp.exp(tot_row), (LANES, GLA_DK_HEAD)).T
        de[c] = jnp.concatenate([col_de] * (GLA_DV_HEAD // LANES), axis=1)
    a = {c: jnp.where(keep, lax.dot_general(qd[c], kd[c], nt_dims, preferred_element_type=F32), 0.0).astype(BF16)
         for c in order}
    o_intra = {c: jnp.dot(a[c], v_ref[rows[c], :], preferred_element_type=F32) for c in order}
    kv = {c: lax.dot_general(ke[c], v_ref[rows[c], :], tn_dims, preferred_element_type=F32) for c in order}

    s = s_ref[...]
    for c in order:
        o = o_intra[c] + jnp.dot(qs[c], s.astype(BF16), preferred_element_type=F32)
        s = de[c] * s + kv[c]
        if reverse:
            o = o + of_ref[rows[c], :]
            o = _rms(o, on_ref[...], GLA_DV_HEAD)
            r = r_ref[rows[c], :].astype(F32)
            o = o * (r * (1.0 / (1.0 + jnp.exp(-r))))
        o_ref[rows[c], :] = o.astype(o_ref.dtype)
    s_ref[...] = s


def _gla_direction(proj, low, wg, bg, o_fwd, o_norm, reverse):
    b, t, _ = proj.shape
    tb = min(2048, t)
    assert t % tb == 0 and tb % GLA_BLOCK == 0
    nblk = t // tb
    blk = (lambda i: nblk - 1 - i) if reverse else (lambda i: i)
    k_col0 = GLA_DK // GLA_DK_HEAD
    v_col0 = 2 * GLA_DK // GLA_DV_HEAD
    r_col0 = (2 * GLA_DK + GLA_DV) // GLA_DV_HEAD
    in_specs = [
        pl.BlockSpec((None, tb, GLA_DK_HEAD), lambda bi, h, i: (bi, blk(i), h)),
        pl.BlockSpec((None, tb, GLA_DK_HEAD), lambda bi, h, i: (bi, blk(i), k_col0 + h)),
        pl.BlockSpec((None, tb, GLA_DV_HEAD), lambda bi, h, i: (bi, blk(i), v_col0 + h)),
        pl.BlockSpec((None, tb, LANES), lambda bi, h, i: (bi, blk(i), 0)),
        pl.BlockSpec((LANES, GLA_DK_HEAD), lambda bi, h, i: (0, h)),
        pl.BlockSpec((1, GLA_DK_HEAD), lambda bi, h, i: (0, h)),
    ]
    args = [proj, proj, proj, low, wg, bg]
    if reverse:
        in_specs += [
            pl.BlockSpec((None, tb, GLA_DV_HEAD), lambda bi, h, i: (bi, blk(i), h)),
            pl.BlockSpec((None, tb, GLA_DV_HEAD), lambda bi, h, i: (bi, blk(i), r_col0 + h)),
            pl.BlockSpec((1, GLA_DV_HEAD), lambda bi, h, i: (0, 0)),
        ]
        args += [o_fwd, proj, o_norm.reshape(1, GLA_DV_HEAD)]
    return pl.pallas_call(
        functools.partial(_gla_kernel, reverse=reverse, n_chunks=tb // GLA_BLOCK),
        grid=(b, GLA_HEADS, nblk),
        in_specs=in_specs,
        out_specs=pl.BlockSpec((None, tb, GLA_DV_HEAD), lambda bi, h, i: (bi, blk(i), h)),
        out_shape=jax.ShapeDtypeStruct((b, t, GLA_DV), BF16 if reverse else F32),
        scratch_shapes=[pltpu.VMEM((GLA_DK_HEAD, GLA_DV_HEAD), F32)],
        compiler_params=_cparams("parallel", "parallel", "arbitrary"),
    )(*args)


def _prep_gqa(w_qkv, q_norm, k_norm):
    n_rot = GQA_Q_HEADS + GQA_KV_HEADS
    cols = (np.arange(n_rot)[:, None] * GQA_HEAD_DIM + _HALF_SPLIT_128[None, :]).reshape(-1)
    cols = np.concatenate([cols, np.arange(n_rot * GQA_HEAD_DIM, w_qkv.shape[1])])
    return w_qkv[:, cols].astype(BF16), q_norm[_HALF_SPLIT_128], k_norm[_HALF_SPLIT_128]


def _prep_mla(w_down, w_uq, q_norm, k_rope_norm):
    qidx, qmsk = _mla_q_layout()
    cols = (np.arange(MLA_HEADS)[:, None] * MLA_QK_DIM + qidx[None, :]).reshape(-1)
    w_uq_p = (w_uq[:, cols] * jnp.asarray(np.tile(qmsk, MLA_HEADS))).astype(BF16)
    q_gain = q_norm[qidx] * jnp.asarray(qmsk)
    ridx, rmsk = _mla_rot_layout(0)
    kr0 = MLA_Q_RANK + MLA_KV_RANK
    w_kr = w_down[:, kr0 + ridx] * jnp.asarray(rmsk)
    w_down_p = jnp.concatenate([w_down[:, :kr0], w_kr], axis=1).astype(BF16)
    kr_gain = k_rope_norm[ridx] * jnp.asarray(rmsk)
    return w_down_p, w_uq_p, q_gain, kr_gain


def _prep_gla_gates(w_g1, w_g2):
    r = GLA_GATE_RANK
    d = w_g1.shape[1]
    w1 = jnp.zeros((d, LANES), F32).at[:, 0:r].set(w_g1[0]).at[:, r:2 * r].set(w_g1[1])
    w2 = jnp.zeros((2, LANES, GLA_DK), F32).at[0, 0:r].set(w_g2[0]).at[1, r:2 * r].set(w_g2[1])
    return w1.astype(BF16), w2.astype(BF16)


def _trunk(x3, mod, p):
    b, t, d = x3.shape
    n = b * t
    x = x3.reshape(n, d)
    for i in range(DEPTH):
        m = mod[i].reshape(b, 6, 1, d)
        sh1, sc1, g1, sh2, sc2, g2 = (m[:, s] for s in range(6))
        kind, j = i % N_MIXERS, i // N_MIXERS
        if kind == 0:
            qkv = _norm_mod_matmul(x, p['norm1_g'][i], sc1, sh1, p['gqa_w_qkv'][j], t, BF16)
            cos, sin = _gqa_rope_tables(t)
            a = _dispatch_on_score_bound(_gqa_score_bound(p['gqa_q_norm'][j], p['gqa_k_norm'][j]), _gqa_attention,
                                         qkv.reshape(b, t, -1), cos, sin, p['gqa_q_norm'][j], p['gqa_k_norm'][j])
            x = _oproj_residual(a.reshape(n, -1), p['gqa_w_o'][j], x, g1, t)
        elif kind == 1:
            proj = _norm_mod_matmul(x, p['norm1_g'][i], sc1, sh1, p['gla_w_in'][j], t, BF16, tn=2048)
            low = _norm_mod_matmul(x, p['norm1_g'][i], sc1, sh1, p['gla_w_g1'][j], t, F32)
            proj3, low3 = proj.reshape(b, t, -1), low.reshape(b, t, -1)
            bg = p['gla_b_g'][j]
            o_f = _gla_direction(proj3, low3, p['gla_w_g2'][j][0], bg[0:1], None, None, reverse=False)
            a = _gla_direction(proj3, low3, p['gla_w_g2'][j][1], bg[1:2], o_f, p['gla_o_norm'][j], reverse=True)
            x = _oproj_residual(a.reshape(n, -1), p['gla_w_o'][j], x, g1, t)
        else:
            down = _norm_mod_matmul(x, p['norm1_g'][i], sc1, sh1, p['mla_w_down'][j], t, F32)
            q = _norm_matmul(down, 0, MLA_Q_RANK, p['mla_q_a_norm'][j], p['mla_w_uq'][j], t, BF16)
            kv = _norm_matmul(down, MLA_Q_RANK // MLA_KV_RANK, MLA_KV_RANK, p['mla_kv_a_norm'][j],
                              p['mla_w_ukv'][j], t, BF16)
            cos, sin = _mla_rope_tables(t)
            gains = (p['mla_q_gain'][j], p['mla_k_nope_norm'][j], p['mla_kr_gain'][j])
            a = _dispatch_on_score_bound(_mla_score_bound(*gains), _mla_attention, q.reshape(b, t, -1),
                                         kv.reshape(b, t, -1), down.reshape(b, t, -1), cos, sin, *gains)
            x = _oproj_residual(a.reshape(n, -1), p['mla_w_o'][j], x, g1, t)
        x = _mlp_residual(x, p['norm2_g'][i], sc2, sh2, g2, p['mlp_w1'][i], p['mlp_w2'][i], t)
    return x.reshape(b, t, d)


def kernel(x_prompt, x_sample, c_prompt, c_sample, norm1_g, norm2_g, ada_w, ada_b, gqa_w_qkv, gqa_q_norm, gqa_k_norm, gqa_w_o, gla_w_in, gla_w_g1, gla_w_g2, gla_b_g, gla_o_norm, gla_w_o, mla_w_down, mla_q_a_norm, mla_kv_a_norm, mla_w_uq, mla_w_ukv, mla_q_norm, mla_k_nope_norm, mla_k_rope_norm, mla_w_o, mlp_w1, mlp_w2):
    nA, nB, nC = gqa_w_qkv.shape[0], gla_w_in.shape[0], mla_w_down.shape[0]
    gqa = [_prep_gqa(gqa_w_qkv[j], gqa_q_norm[j], gqa_k_norm[j]) for j in range(nA)]
    mla = [_prep_mla(mla_w_down[j], mla_w_uq[j], mla_q_norm[j], mla_k_rope_norm[j]) for j in range(nC)]
    gates = [_prep_gla_gates(gla_w_g1[j], gla_w_g2[j]) for j in range(nB)]
    p = dict(
        norm1_g=norm1_g, norm2_g=norm2_g,
        gqa_w_qkv=[g[0] for g in gqa], gqa_q_norm=[g[1] for g in gqa], gqa_k_norm=[g[2] for g in gqa],
        gqa_w_o=gqa_w_o.astype(BF16),
        gla_w_in=gla_w_in.astype(BF16), gla_w_g1=[g[0] for g in gates], gla_w_g2=[g[1] for g in gates],
        gla_b_g=gla_b_g, gla_o_norm=gla_o_norm, gla_w_o=gla_w_o.astype(BF16),
        mla_w_down=[m[0] for m in mla], mla_w_uq=[m[1] for m in mla], mla_q_gain=[m[2] for m in mla],
        mla_kr_gain=[m[3] for m in mla], mla_q_a_norm=mla_q_a_norm, mla_kv_a_norm=mla_kv_a_norm,
        mla_w_ukv=mla_w_ukv.astype(BF16), mla_k_nope_norm=mla_k_nope_norm, mla_w_o=mla_w_o.astype(BF16),
        mlp_w1=mlp_w1.astype(BF16), mlp_w2=mlp_w2.astype(BF16),
    )
    bp, bs = c_prompt.shape[0], c_sample.shape[0]
    nb_pad = -(-(bp + bs) // 16) * 16
    c_all = jnp.concatenate([c_prompt, c_sample, jnp.zeros((nb_pad - bp - bs, c_prompt.shape[1]), F32)], axis=0)
    mod = _ada_modulation(c_all, ada_w, ada_b)
    y_prompt = _trunk(x_prompt, mod[:, :bp], p)
    y_sample = _trunk(x_sample, mod[:, bp:bp + bs], p)
    return (y_prompt, y_sample)
```

```python
import functools
import math

import numpy as np
import jax
import jax.numpy as jnp
from jax import lax
from jax.experimental import pallas as pl
from jax.experimental.pallas import tpu as pltpu

F32 = jnp.float32
BF16 = jnp.bfloat16

D_MODEL = 2048
DEPTH = 4
GRID_W = 64
ROPE_THETA = 10000.0
NORM_EPS = 1e-6
N_MIXERS = 3

GQA_HEAD_DIM = 128
GQA_Q_HEADS = 16
GQA_KV_HEADS = 4
GQA_GROUP = 4

GLA_HEADS = 4
GLA_DK = 1024
GLA_DV = 2048
GLA_DK_HEAD = 256
GLA_DV_HEAD = 512
GLA_GATE_RANK = 16
GLA_GATE_TAU = 16.0
GLA_BLOCK = 128

MLA_HEADS = 16
MLA_Q_RANK = 512
MLA_KV_RANK = 256
MLA_NOPE_DIM = 128
MLA_ROPE_DIM = 64
MLA_QK_DIM = 192
MLA_V_DIM = 128
MLA_QPAD = 256
MLA_DOWN_PAD = 896

D_FF = 4 * D_MODEL

LANES = 128
VMEM_LIMIT_BYTES = 56 * 1024 * 1024


def _tiles(n_rows, seq):
    tm = min(512, seq)
    assert seq % tm == 0 and n_rows % tm == 0
    return tm


SHORT_SEQ = 2048


def _attention_tiles(t, heads_per_step, long_seq_score_bytes):
    short = t <= SHORT_SEQ
    m_cols = 2048 if short else 1024
    score_bytes = 8 * 1024 * 1024 if short else long_seq_score_bytes
    tq = min(m_cols // heads_per_step, t)
    tk = min(score_bytes // (4 * m_cols), t)
    assert t % tq == 0 and t % tk == 0
    return tq, tk


def _cparams(*sem):
    return pltpu.CompilerParams(dimension_semantics=sem, vmem_limit_bytes=VMEM_LIMIT_BYTES)


def _ada_kernel(c_ref, w_ref, b_ref, o_ref):
    c = c_ref[...]
    cond = (c * (1.0 / (1.0 + jnp.exp(-c)))).astype(BF16)
    o_ref[...] = jnp.dot(cond, w_ref[...].astype(BF16), preferred_element_type=F32) + b_ref[...]


def _ada_modulation(c_all, ada_w, ada_b):
    nb, d = c_all.shape
    depth, _, n6 = ada_w.shape
    tn = 1024
    return pl.pallas_call(
        _ada_kernel,
        grid=(depth, n6 // tn),
        in_specs=[
            pl.BlockSpec((nb, d), lambda i, j: (0, 0)),
            pl.BlockSpec((None, d, tn), lambda i, j: (i, 0, j)),
            pl.BlockSpec((None, 1, tn), lambda i, j: (i, 0, j)),
        ],
        out_specs=pl.BlockSpec((None, nb, tn), lambda i, j: (i, 0, j)),
        out_shape=jax.ShapeDtypeStruct((depth, nb, n6), F32),
        compiler_params=_cparams("parallel", "parallel"),
    )(c_all, ada_w, ada_b.reshape(depth, 1, n6))


def _rms(x, g, n):
    ms = jnp.sum(x * x, axis=-1, keepdims=True) * (1.0 / n)
    return x * lax.rsqrt(ms + NORM_EPS) * g


NORM_ROWS = 16


def _store_norm(h_ref, x_ref, gain, shift):
    d = x_ref.shape[-1]
    blocks = [slice(r, r + NORM_ROWS) for r in range(0, x_ref.shape[0], NORM_ROWS)]
    inv = []
    for rows in blocks:
        x = x_ref[rows, :].astype(F32)
        inv.append(lax.rsqrt(jnp.sum(x * x, axis=-1, keepdims=True) * (1.0 / d) + NORM_EPS))
    for rows, inv_r in zip(blocks, inv):
        y = x_ref[rows, :].astype(F32) * inv_r * gain
        h_ref[rows, :] = (y if shift is None else y + shift).astype(BF16)


def _norm_mod_matmul_kernel(x_ref, g_ref, sc_ref, sh_ref, w_ref, o_ref, h_ref):
    @pl.when(pl.program_id(1) == 0)
    def _():
        _store_norm(h_ref, x_ref, g_ref[...] * (1.0 + sc_ref[...]), sh_ref[...])

    o_ref[...] = jnp.dot(h_ref[...], w_ref[...], preferred_element_type=F32).astype(o_ref.dtype)


def _norm_matmul_kernel(x_ref, g_ref, w_ref, o_ref, h_ref):
    @pl.when(pl.program_id(1) == 0)
    def _():
        _store_norm(h_ref, x_ref, g_ref[...], None)

    o_ref[...] = jnp.dot(h_ref[...], w_ref[...], preferred_element_type=F32).astype(o_ref.dtype)


def _norm_mod_matmul(x, g, sc, sh, w, seq, out_dtype, tn=None):
    n, d = x.shape
    nout = w.shape[1]
    tm = _tiles(n, seq)
    tn = nout if tn is None else tn
    tpb = seq // tm
    n_j = nout // tn
    return pl.pallas_call(
        _norm_mod_matmul_kernel,
        grid=(n // tm, n_j),
        in_specs=[
            pl.BlockSpec((tm, d), lambda i, j: (i, 0)),
            pl.BlockSpec((1, d), lambda i, j: (0, 0)),
            pl.BlockSpec((None, 1, d), lambda i, j: (i // tpb, 0, 0)),
            pl.BlockSpec((None, 1, d), lambda i, j: (i // tpb, 0, 0)),
            pl.BlockSpec((d, tn), lambda i, j: (0, _serpentine(i, j, n_j))),
        ],
        out_specs=pl.BlockSpec((tm, tn), lambda i, j: (i, _serpentine(i, j, n_j))),
        out_shape=jax.ShapeDtypeStruct((n, nout), out_dtype),
        scratch_shapes=[pltpu.VMEM((tm, d), BF16)],
        compiler_params=_cparams("parallel", "arbitrary"),
    )(x, g.reshape(1, d), sc, sh, w)


def _norm_matmul(x, col_block, width, g, w, seq, out_dtype):
    n = x.shape[0]
    nout = w.shape[1]
    tm = _tiles(n, seq)
    return pl.pallas_call(
        _norm_matmul_kernel,
        grid=(n // tm, 1),
        in_specs=[
            pl.BlockSpec((tm, width), lambda i, j: (i, col_block)),
            pl.BlockSpec((1, width), lambda i, j: (0, 0)),
            pl.BlockSpec((width, nout), lambda i, j: (0, 0)),
        ],
        out_specs=pl.BlockSpec((tm, nout), lambda i, j: (i, 0)),
        out_shape=jax.ShapeDtypeStruct((n, nout), out_dtype),
        scratch_shapes=[pltpu.VMEM((tm, width), BF16)],
        compiler_params=_cparams("parallel", "arbitrary"),
    )(x, g.reshape(1, width), w)


def _oproj_kernel(a_ref, w_ref, x_ref, gate_ref, o_ref):
    y = jnp.dot(a_ref[...], w_ref[...], preferred_element_type=F32)
    o_ref[...] = x_ref[...] + gate_ref[...] * y


def _oproj_residual(a, w, x, gate, seq):
    n, k = a.shape
    d = w.shape[1]
    tm = _tiles(n, seq)
    tpb = seq // tm
    return pl.pallas_call(
        _oproj_kernel,
        grid=(n // tm,),
        in_specs=[
            pl.BlockSpec((tm, k), lambda i: (i, 0)),
            pl.BlockSpec((k, d), lambda i: (0, 0)),
            pl.BlockSpec((tm, d), lambda i: (i, 0)),
            pl.BlockSpec((None, 1, d), lambda i: (i // tpb, 0, 0)),
        ],
        out_specs=pl.BlockSpec((tm, d), lambda i: (i, 0)),
        out_shape=jax.ShapeDtypeStruct((n, d), F32),
        compiler_params=_cparams("parallel"),
    )(a, w, x, gate)


def _mlp_kernel(x_ref, g_ref, sc_ref, sh_ref, gate_ref, w1_ref, w2_ref, o_ref, h_ref):
    j = pl.program_id(1)
    last = pl.num_programs(1) - 1

    def hidden():
        a = jnp.maximum(jnp.dot(h_ref[...], w1_ref[...], preferred_element_type=F32), 0.0)
        return jnp.dot((a * a).astype(BF16), w2_ref[...], preferred_element_type=F32)

    @pl.when(j == 0)
    def _():
        _store_norm(h_ref, x_ref, g_ref[...] * (1.0 + sc_ref[...]), sh_ref[...])
        o_ref[...] = hidden()

    @pl.when((j > 0) & (j < last))
    def _():
        o_ref[...] += hidden()

    @pl.when(j == last)
    def _():
        o_ref[...] = x_ref[...] + gate_ref[...] * (o_ref[...] + hidden())


def _serpentine(i, j, n_j):
    return jnp.where(lax.rem(i, 2) == 0, j, n_j - 1 - j)


def _mlp_residual(x, g, sc, sh, gate, w1, w2, seq):
    n, d = x.shape
    dff = w1.shape[1]
    tm = _tiles(n, seq)
    tf = min(2048, dff)
    assert dff % tf == 0 and dff // tf >= 2
    n_j = dff // tf
    tpb = seq // tm
    mod_spec = pl.BlockSpec((None, 1, d), lambda i, j: (i // tpb, 0, 0))
    return pl.pallas_call(
        _mlp_kernel,
        grid=(n // tm, n_j),
        in_specs=[
            pl.BlockSpec((tm, d), lambda i, j: (i, 0)),
            pl.BlockSpec((1, d), lambda i, j: (0, 0)),
            mod_spec, mod_spec, mod_spec,
            pl.BlockSpec((d, tf), lambda i, j: (0, _serpentine(i, j, n_j))),
            pl.BlockSpec((tf, d), lambda i, j: (_serpentine(i, j, n_j), 0)),
        ],
        out_specs=pl.BlockSpec((tm, d), lambda i, j: (i, 0)),
        out_shape=jax.ShapeDtypeStruct((n, d), F32),
        scratch_shapes=[pltpu.VMEM((tm, d), BF16)],
        compiler_params=_cparams("parallel", "arbitrary"),
    )(x, g.reshape(1, d), sc, sh, gate, w1, w2)


def _axial_angles(n_tok, rot_dim):
    rows = n_tok // GRID_W
    row = jnp.repeat(jnp.arange(rows, dtype=F32), GRID_W)
    col = jnp.tile(jnp.arange(GRID_W, dtype=F32), rows)
    n_freq = rot_dim // 4
    inv = ROPE_THETA ** (-jnp.arange(n_freq, dtype=F32) / n_freq)
    ang = jnp.concatenate([row[:, None] * inv, col[:, None] * inv], axis=-1)
    return jnp.cos(ang), jnp.sin(ang)


def _gqa_rope_tables(n_tok):
    cos, sin = _axial_angles(n_tok, GQA_HEAD_DIM)
    return jnp.concatenate([cos, cos], -1), jnp.concatenate([-sin, sin], -1)


def _mla_rope_tables(n_tok):
    cos, sin = _axial_angles(n_tok, MLA_ROPE_DIM)
    z = jnp.zeros_like(cos)
    return jnp.concatenate([cos, z, cos, z], -1), jnp.concatenate([-sin, z, sin, z], -1)


_HALF_SPLIT_128 = np.concatenate([np.arange(0, 128, 2), np.arange(1, 128, 2)])


def _mla_rot_layout(base):
    idx = np.zeros(128, np.int32)
    msk = np.zeros(128, np.float32)
    idx[0:32] = base + np.arange(0, 64, 2)
    idx[64:96] = base + np.arange(1, 64, 2)
    msk[0:32] = 1.0
    msk[64:96] = 1.0
    return idx, msk


def _mla_q_layout():
    ridx, rmsk = _mla_rot_layout(MLA_NOPE_DIM)
    idx = np.concatenate([np.arange(MLA_NOPE_DIM), ridx])
    msk = np.concatenate([np.ones(MLA_NOPE_DIM, np.float32), rmsk])
    return idx, msk


def _rope(x, cos, sin):
    return x * cos + pltpu.roll(x, 64, 1) * sin


LOG2E = math.log2(math.e)


MAX_UNSHIFTED_SCORE = 40.0


def _flash_t(q_t, kn_ref, vt_ref, n_keys, tk, bounded):
    m_cols = q_t.shape[1]
    m = jnp.full((1, m_cols), -jnp.inf, F32)
    l8 = jnp.zeros((8, m_cols), F32)
    acc = jnp.zeros((vt_ref.shape[0], m_cols), F32)
    for c in range(n_keys // tk):
        s = jnp.dot(kn_ref[c * tk:(c + 1) * tk, :], q_t, preferred_element_type=F32)
        if bounded:
            p = jnp.exp2(s)
            l8 = l8 + jnp.sum(p.reshape(tk // 8, 8, m_cols), axis=0)
            acc = acc + jnp.dot(vt_ref[:, c * tk:(c + 1) * tk], p.astype(BF16), preferred_element_type=F32)
        else:
            m_new = jnp.maximum(m, jnp.max(s, axis=0, keepdims=True))
            alpha = jnp.exp2(m - m_new)
            p = jnp.exp2(s - m_new)
            l8 = alpha * l8 + jnp.sum(p.reshape(tk // 8, 8, m_cols), axis=0)
            acc = alpha * acc + jnp.dot(vt_ref[:, c * tk:(c + 1) * tk], p.astype(BF16),
                                        preferred_element_type=F32)
            m = m_new
    return acc * (1.0 / jnp.sum(l8, axis=0, keepdims=True))


def _store_vt(vt_ref, v_load, n_keys, blk):
    for c in range(n_keys // blk):
        vt_ref[:, c * blk:(c + 1) * blk] = v_load(c * blk, blk).astype(F32).T.astype(BF16)


def _gqa_attn_kernel(q_ref, k_ref, v_ref, cq_ref, sq_ref, ck_ref, sk_ref, qg_ref, kg_ref, o_ref, kn_ref, vt_ref,
                     *, tk, kprep, bounded):
    n_keys = k_ref.shape[0]
    tq = q_ref.shape[0]

    @pl.when(pl.program_id(2) == 0)
    def _():
        def prep(c, _):
            rows = pl.ds(pl.multiple_of(c * kprep, kprep), kprep)
            k = _rms(k_ref[rows, :].astype(F32), kg_ref[...], GQA_HEAD_DIM)
            kn_ref[rows, :] = _rope(k, ck_ref[rows, :], sk_ref[rows, :]).astype(BF16)
            return 0
        lax.fori_loop(0, n_keys // kprep, prep, 0)
        _store_vt(vt_ref, lambda r0, nr: v_ref[r0:r0 + nr, :], n_keys, kprep)

    scale = GQA_HEAD_DIM ** -0.5 * LOG2E
    qs = []
    for h in range(GQA_GROUP):
        q = _rms(q_ref[:, h * GQA_HEAD_DIM:(h + 1) * GQA_HEAD_DIM].astype(F32), qg_ref[...], GQA_HEAD_DIM)
        qs.append((_rope(q, cq_ref[...], sq_ref[...]) * scale).T.astype(BF16))
    q_t = jnp.concatenate(qs, axis=1)
    out_t = _flash_t(q_t, kn_ref, vt_ref, n_keys, tk, bounded)
    for h in range(GQA_GROUP):
        o_ref[:, h * GQA_HEAD_DIM:(h + 1) * GQA_HEAD_DIM] = out_t[:, h * tq:(h + 1) * tq].T.astype(o_ref.dtype)


def _dispatch_on_score_bound(bound, attention, *args):
    return lax.cond(bound <= MAX_UNSHIFTED_SCORE,
                    functools.partial(attention, bounded=True),
                    functools.partial(attention, bounded=False), *args)


def _gqa_score_bound(q_gain, k_gain):
    return (GQA_HEAD_DIM * jnp.max(jnp.abs(q_gain)) * jnp.max(jnp.abs(k_gain))) * (GQA_HEAD_DIM ** -0.5 * LOG2E)


def _mla_score_bound(q_gain, kn_gain, kr_gain):
    q_norm = math.sqrt(MLA_QK_DIM) * jnp.max(jnp.abs(q_gain))
    k_norm = jnp.sqrt(MLA_NOPE_DIM * jnp.max(kn_gain * kn_gain) + MLA_ROPE_DIM * jnp.max(kr_gain * kr_gain))
    return q_norm * k_norm * (MLA_QK_DIM ** -0.5 * LOG2E)


def _gqa_attention(qkv, cos, sin, q_gain, k_gain, *, bounded):
    b, t, _ = qkv.shape
    tq, tk = _attention_tiles(t, GQA_GROUP, long_seq_score_bytes=16 * 1024 * 1024)
    kprep = min(512, t)
    gw = GQA_GROUP * GQA_HEAD_DIM
    k_col0 = GQA_Q_HEADS
    v_col0 = GQA_Q_HEADS + GQA_KV_HEADS
    tab_q = pl.BlockSpec((tq, LANES), lambda bi, g, qi: (qi, 0))
    tab_k = pl.BlockSpec((t, LANES), lambda bi, g, qi: (0, 0), pipeline_mode=pl.Buffered(1))
    gain = pl.BlockSpec((1, LANES), lambda bi, g, qi: (0, 0))
    return pl.pallas_call(
        functools.partial(_gqa_attn_kernel, tk=tk, kprep=kprep, bounded=bounded),
        grid=(b, GQA_KV_HEADS, t // tq),
        in_specs=[
            pl.BlockSpec((None, tq, gw), lambda bi, g, qi: (bi, qi, g)),
            pl.BlockSpec((None, t, GQA_HEAD_DIM), lambda bi, g, qi: (bi, 0, k_col0 + g)),
            pl.BlockSpec((None, t, GQA_HEAD_DIM), lambda bi, g, qi: (bi, 0, v_col0 + g)),
            tab_q, tab_q, tab_k, tab_k, gain, gain,
        ],
        out_specs=pl.BlockSpec((None, tq, gw), lambda bi, g, qi: (bi, qi, g)),
        out_shape=jax.ShapeDtypeStruct((b, t, GQA_Q_HEADS * GQA_HEAD_DIM), BF16),
        scratch_shapes=[pltpu.VMEM((t, GQA_HEAD_DIM), BF16), pltpu.VMEM((GQA_HEAD_DIM, t), BF16)],
        compiler_params=_cparams("parallel", "parallel", "arbitrary"),
    )(qkv, qkv, qkv, cos, sin, cos, sin, q_gain.reshape(1, LANES), k_gain.reshape(1, LANES))


def _mla_attn_kernel(q_ref, kv_ref, kr_ref, cq_ref, sq_ref, ck_ref, sk_ref, qg_ref, kng_ref, krg_ref,
                     o_ref, kn_ref, vt_ref, *, tk, kprep, bounded):
    n_keys = kv_ref.shape[0]

    @pl.when(pl.program_id(2) == 0)
    def _():
        def prep(c, _):
            rows = pl.ds(pl.multiple_of(c * kprep, kprep), kprep)
            kn = _rms(kv_ref[rows, 0:MLA_NOPE_DIM].astype(F32), kng_ref[...], MLA_NOPE_DIM)
            kn_ref[rows, 0:MLA_NOPE_DIM] = kn.astype(BF16)
            kr = _rms(kr_ref[rows, :].astype(F32), krg_ref[...], MLA_ROPE_DIM)
            kn_ref[rows, MLA_NOPE_DIM:MLA_QPAD] = _rope(kr, ck_ref[rows, :], sk_ref[rows, :]).astype(BF16)
            return 0
        lax.fori_loop(0, n_keys // kprep, prep, 0)
        _store_vt(vt_ref, lambda r0, nr: kv_ref[r0:r0 + nr, MLA_NOPE_DIM:MLA_QPAD], n_keys, kprep)

    scale = MLA_QK_DIM ** -0.5 * LOG2E
    q = q_ref[...].astype(F32)
    g = qg_ref[...]
    inv = lax.rsqrt(jnp.sum(q * q, axis=-1, keepdims=True) * (1.0 / MLA_QK_DIM) + NORM_EPS)
    q_nope = q[:, 0:MLA_NOPE_DIM] * inv * g[:, 0:MLA_NOPE_DIM]
    q_rot = _rope(q[:, MLA_NOPE_DIM:MLA_QPAD] * inv * g[:, MLA_NOPE_DIM:MLA_QPAD], cq_ref[...], sq_ref[...])
    q_t = jnp.concatenate([(q_nope * scale).T, (q_rot * scale).T], axis=0).astype(BF16)
    out_t = _flash_t(q_t, kn_ref, vt_ref, n_keys, tk, bounded)
    o_ref[...] = out_t.T.astype(o_ref.dtype)


def _mla_attention(q, kv, down, cos, sin, q_gain, kn_gain, kr_gain, *, bounded):
    b, t, _ = q.shape
    tq, tk = _attention_tiles(t, 1, long_seq_score_bytes=8 * 1024 * 1024)
    kprep = min(512, t)
    kr_col = (MLA_Q_RANK + MLA_KV_RANK) // LANES
    tab_q = pl.BlockSpec((tq, LANES), lambda bi, h, qi: (qi, 0))
    tab_k = pl.BlockSpec((t, LANES), lambda bi, h, qi: (0, 0), pipeline_mode=pl.Buffered(1))
    return pl.pallas_call(
        functools.partial(_mla_attn_kernel, tk=tk, kprep=kprep, bounded=bounded),
        grid=(b, MLA_HEADS, t // tq),
        in_specs=[
            pl.BlockSpec((None, tq, MLA_QPAD), lambda bi, h, qi: (bi, qi, h)),
            pl.BlockSpec((None, t, MLA_QPAD), lambda bi, h, qi: (bi, 0, h)),
            pl.BlockSpec((None, t, LANES), lambda bi, h, qi: (bi, 0, kr_col)),
            tab_q, tab_q, tab_k, tab_k,
            pl.BlockSpec((1, MLA_QPAD), lambda bi, h, qi: (0, 0)),
            pl.BlockSpec((1, LANES), lambda bi, h, qi: (0, 0)),
            pl.BlockSpec((1, LANES), lambda bi, h, qi: (0, 0)),
        ],
        out_specs=pl.BlockSpec((None, tq, MLA_V_DIM), lambda bi, h, qi: (bi, qi, h)),
        out_shape=jax.ShapeDtypeStruct((b, t, MLA_HEADS * MLA_V_DIM), BF16),
        scratch_shapes=[pltpu.VMEM((t, MLA_QPAD), BF16), pltpu.VMEM((MLA_V_DIM, t), BF16)],
        compiler_params=_cparams("parallel", "parallel", "arbitrary"),
    )(q, kv, down, cos, sin, cos, sin, q_gain.reshape(1, MLA_QPAD), kn_gain.reshape(1, LANES),
      kr_gain.reshape(1, LANES))


def _split3(x):
    hi = x.astype(BF16)
    r1 = x - hi.astype(F32)
    mid = r1.astype(BF16)
    lo = (r1 - mid.astype(F32)).astype(BF16)
    return hi, mid, lo


def _gla_kernel(*refs, reverse, n_chunks):
    if reverse:
        (q_ref, k_ref, v_ref, low_ref, wg_ref, bg_ref, of_ref, r_ref, on_ref, o_ref, s_ref) = refs
    else:
        (q_ref, k_ref, v_ref, low_ref, wg_ref, bg_ref, o_ref, s_ref) = refs
    L = GLA_BLOCK

    @pl.when(pl.program_id(2) == 0)
    def _():
        s_ref[...] = jnp.zeros_like(s_ref)

    row = lax.broadcasted_iota(jnp.int32, (L, L), 0)
    col = lax.broadcasted_iota(jnp.int32, (L, L), 1)
    keep = (col >= row) if reverse else (col <= row)
    tri = jnp.where(keep, 1.0, 0.0).astype(BF16)
    tn_dims = (((0,), (0,)), ((), ()))
    nt_dims = (((1,), (1,)), ((), ()))
    mid = L // 2 if reverse else L // 2 - 1
    edge = 0 if reverse else L - 1

    order = list(range(n_chunks - 1, -1, -1) if reverse else range(n_chunks))
    rows = {c: slice(c * L, (c + 1) * L) for c in order}

    z = jnp.dot(low_ref[...].astype(BF16), wg_ref[...], preferred_element_type=F32) + bg_ref[...]
    cum = {}
    for c in order:
        zc = z[rows[c], :]
        la = (jnp.minimum(zc, 0.0) - jnp.log1p(jnp.exp(-jnp.abs(zc)))) * (1.0 / GLA_GATE_TAU)
        cum[c] = sum(jnp.dot(tri, p, preferred_element_type=F32) for p in _split3(la))
    qd, qs, kd, ke, de = {}, {}, {}, {}, {}
    for c in order:
        ref_row = cum[c][mid:mid + 1, :]
        tot_row = cum[c][edge:edge + 1, :]
        q = q_ref[rows[c], :].astype(F32) * (GLA_DK_HEAD ** -0.5)
        k = k_ref[rows[c], :].astype(F32)
        q_mid = q * jnp.exp(cum[c] - ref_row)
        qd[c] = q_mid.astype(BF16)
        qs[c] = (q_mid * jnp.exp(ref_row)).astype(BF16)
        kd[c] = (k * jnp.exp(ref_row - cum[c])).astype(BF16)
        ke[c] = (k * jnp.exp(tot_row - cum[c])).astype(BF16)
        col_de = jnp.broadcast_to(jnp.exp(tot_row), (LANES, GLA_DK_HEAD)).T
        de[c] = jnp.concatenate([col_de] * (GLA_DV_HEAD // LANES), axis=1)
    a = {c: jnp.where(keep, lax.dot_general(qd[c], kd[c], nt_dims, preferred_element_type=F32), 0.0).astype(BF16)
         for c in order}
    o_intra = {c: jnp.dot(a[c], v_ref[rows[c], :], preferred_element_type=F32) for c in order}
    kv = {c: lax.dot_general(ke[c], v_ref[rows[c], :], tn_dims, preferred_element_type=F32) for c in order}

    s = s_ref[...]
    for c in order:
        o = o_intra[c] + jnp.dot(qs[c], s.astype(BF16), preferred_element_type=F32)
        s = de[c] * s + kv[c]
        if reverse:
            o = o + of_ref[rows[c], :]
            o = _rms(o, on_ref[...], GLA_DV_HEAD)
            r = r_ref[rows[c], :].astype(F32)
            o = o * (r * (1.0 / (1.0 + jnp.exp(-r))))
        o_ref[rows[c], :] = o.astype(o_ref.dtype)
    s_ref[...] = s


def _gla_direction(proj, low, wg, bg, o_fwd, o_norm, reverse):
    b, t, _ = proj.shape
    tb = min(2048, t)
    assert t % tb == 0 and tb % GLA_BLOCK == 0
    nblk = t // tb
    blk = (lambda i: nblk - 1 - i) if reverse else (lambda i: i)
    k_col0 = GLA_DK // GLA_DK_HEAD
    v_col0 = 2 * GLA_DK // GLA_DV_HEAD
    r_col0 = (2 * GLA_DK + GLA_DV) // GLA_DV_HEAD
    in_specs = [
        pl.BlockSpec((None, tb, GLA_DK_HEAD), lambda bi, h, i: (bi, blk(i), h)),
        pl.BlockSpec((None, tb, GLA_DK_HEAD), lambda bi, h, i: (bi, blk(i), k_col0 + h)),
        pl.BlockSpec((None, tb, GLA_DV_HEAD), lambda bi, h, i: (bi, blk(i), v_col0 + h)),
        pl.BlockSpec((None, tb, LANES), lambda bi, h, i: (bi, blk(i), 0)),
        pl.BlockSpec((LANES, GLA_DK_HEAD), lambda bi, h, i: (0, h)),
        pl.BlockSpec((1, GLA_DK_HEAD), lambda bi, h, i: (0, h)),
    ]
    args = [proj, proj, proj, low, wg, bg]
    if reverse:
        in_specs += [
            pl.BlockSpec((None, tb, GLA_DV_HEAD), lambda bi, h, i: (bi, blk(i), h)),
            pl.BlockSpec((None, tb, GLA_DV_HEAD), lambda bi, h, i: (bi, blk(i), r_col0 + h)),
            pl.BlockSpec((1, GLA_DV_HEAD), lambda bi, h, i: (0, 0)),
        ]
        args += [o_fwd, proj, o_norm.reshape(1, GLA_DV_HEAD)]
    return pl.pallas_call(
        functools.partial(_gla_kernel, reverse=reverse, n_chunks=tb // GLA_BLOCK),
        grid=(b, GLA_HEADS, nblk),
        in_specs=in_specs,
        out_specs=pl.BlockSpec((None, tb, GLA_DV_HEAD), lambda bi, h, i: (bi, blk(i), h)),
        out_shape=jax.ShapeDtypeStruct((b, t, GLA_DV), BF16 if reverse else F32),
        scratch_shapes=[pltpu.VMEM((GLA_DK_HEAD, GLA_DV_HEAD), F32)],
        compiler_params=_cparams("parallel", "parallel", "arbitrary"),
    )(*args)


def _prep_gqa(w_qkv, q_norm, k_norm):
    n_rot = GQA_Q_HEADS + GQA_KV_HEADS
    cols = (np.arange(n_rot)[:, None] * GQA_HEAD_DIM + _HALF_SPLIT_128[None, :]).reshape(-1)
    cols = np.concatenate([cols, np.arange(n_rot * GQA_HEAD_DIM, w_qkv.shape[1])])
    return w_qkv[:, cols].astype(BF16), q_norm[_HALF_SPLIT_128], k_norm[_HALF_SPLIT_128]


def _prep_mla(w_down, w_uq, q_norm, k_rope_norm):
    qidx, qmsk = _mla_q_layout()
    cols = (np.arange(MLA_HEADS)[:, None] * MLA_QK_DIM + qidx[None, :]).reshape(-1)
    w_uq_p = (w_uq[:, cols] * jnp.asarray(np.tile(qmsk, MLA_HEADS))).astype(BF16)
    q_gain = q_norm[qidx] * jnp.asarray(qmsk)
    ridx, rmsk = _mla_rot_layout(0)
    kr0 = MLA_Q_RANK + MLA_KV_RANK
    w_kr = w_down[:, kr0 + ridx] * jnp.asarray(rmsk)
    w_down_p = jnp.concatenate([w_down[:, :kr0], w_kr], axis=1).astype(BF16)
    kr_gain = k_rope_norm[ridx] * jnp.asarray(rmsk)
    return w_down_p, w_uq_p, q_gain, kr_gain


def _prep_gla_gates(w_g1, w_g2):
    r = GLA_GATE_RANK
    d = w_g1.shape[1]
    w1 = jnp.zeros((d, LANES), F32).at[:, 0:r].set(w_g1[0]).at[:, r:2 * r].set(w_g1[1])
    w2 = jnp.zeros((2, LANES, GLA_DK), F32).at[0, 0:r].set(w_g2[0]).at[1, r:2 * r].set(w_g2[1])
    return w1.astype(BF16), w2.astype(BF16)


def _trunk(x3, mod, p):
    b, t, d = x3.shape
    n = b * t
    x = x3.reshape(n, d)
    for i in range(DEPTH):
        m = mod[i].reshape(b, 6, 1, d)
        sh1, sc1, g1, sh2, sc2, g2 = (m[:, s] for s in range(6))
        kind, j = i % N_MIXERS, i // N_MIXERS
        if kind == 0:
            qkv = _norm_mod_matmul(x, p['norm1_g'][i], sc1, sh1, p['gqa_w_qkv'][j], t, BF16)
            cos, sin = _gqa_rope_tables(t)
            a = _dispatch_on_score_bound(_gqa_score_bound(p['gqa_q_norm'][j], p['gqa_k_norm'][j]), _gqa_attention,
                                         qkv.reshape(b, t, -1), cos, sin, p['gqa_q_norm'][j], p['gqa_k_norm'][j])
            x = _oproj_residual(a.reshape(n, -1), p['gqa_w_o'][j], x, g1, t)
        elif kind == 1:
            proj = _norm_mod_matmul(x, p['norm1_g'][i], sc1, sh1, p['gla_w_in'][j], t, BF16, tn=3072)
            low = _norm_mod_matmul(x, p['norm1_g'][i], sc1, sh1, p['gla_w_g1'][j], t, F32)
            proj3, low3 = proj.reshape(b, t, -1), low.reshape(b, t, -1)
            bg = p['gla_b_g'][j]
            o_f = _gla_direction(proj3, low3, p['gla_w_g2'][j][0], bg[0:1], None, None, reverse=False)
            a = _gla_direction(proj3, low3, p['gla_w_g2'][j][1], bg[1:2], o_f, p['gla_o_norm'][j], reverse=True)
            x = _oproj_residual(a.reshape(n, -1), p['gla_w_o'][j], x, g1, t)
        else:
            down = _norm_mod_matmul(x, p['norm1_g'][i], sc1, sh1, p['mla_w_down'][j], t, F32)
            q = _norm_matmul(down, 0, MLA_Q_RANK, p['mla_q_a_norm'][j], p['mla_w_uq'][j], t, BF16)
            kv = _norm_matmul(down, MLA_Q_RANK // MLA_KV_RANK, MLA_KV_RANK, p['mla_kv_a_norm'][j],
                              p['mla_w_ukv'][j], t, BF16)
            cos, sin = _mla_rope_tables(t)
            gains = (p['mla_q_gain'][j], p['mla_k_nope_norm'][j], p['mla_kr_gain'][j])
            a = _dispatch_on_score_bound(_mla_score_bound(*gains), _mla_attention, q.reshape(b, t, -1),
                                         kv.reshape(b, t, -1), down.reshape(b, t, -1), cos, sin, *gains)
            x = _oproj_residual(a.reshape(n, -1), p['mla_w_o'][j], x, g1, t)
        x = _mlp_residual(x, p['norm2_g'][i], sc2, sh2, g2, p['mlp_w1'][i], p['mlp_w2'][i], t)
    return x.reshape(b, t, d)


def kernel(x_prompt, x_sample, c_prompt, c_sample, norm1_g, norm2_g, ada_w, ada_b, gqa_w_qkv, gqa_q_norm, gqa_k_norm, gqa_w_o, gla_w_in, gla_w_g1, gla_w_g2, gla_b_g, gla_o_norm, gla_w_o, mla_w_down, mla_q_a_norm, mla_kv_a_norm, mla_w_uq, mla_w_ukv, mla_q_norm, mla_k_nope_norm, mla_k_rope_norm, mla_w_o, mlp_w1, mlp_w2):
    nA, nB, nC = gqa_w_qkv.shape[0], gla_w_in.shape[0], mla_w_down.shape[0]
    gqa = [_prep_gqa(gqa_w_qkv[j], gqa_q_norm[j], gqa_k_norm[j]) for j in range(nA)]
    mla = [_prep_mla(mla_w_down[j], mla_w_uq[j], mla_q_norm[j], mla_k_rope_norm[j]) for j in range(nC)]
    gates = [_prep_gla_gates(gla_w_g1[j], gla_w_g2[j]) for j in range(nB)]
    p = dict(
        norm1_g=norm1_g, norm2_g=norm2_g,
        gqa_w_qkv=[g[0] for g in gqa], gqa_q_norm=[g[1] for g in gqa], gqa_k_norm=[g[2] for g in gqa],
        gqa_w_o=gqa_w_o.astype(BF16),
        gla_w_in=gla_w_in.astype(BF16), gla_w_g1=[g[0] for g in gates], gla_w_g2=[g[1] for g in gates],
        gla_b_g=gla_b_g, gla_o_norm=gla_o_norm, gla_w_o=gla_w_o.astype(BF16),
        mla_w_down=[m[0] for m in mla], mla_w_uq=[m[1] for m in mla], mla_q_gain=[m[2] for m in mla],
        mla_kr_gain=[m[3] for m in mla], mla_q_a_norm=mla_q_a_norm, mla_kv_a_norm=mla_kv_a_norm,
        mla_w_ukv=mla_w_ukv.astype(BF16), mla_k_nope_norm=mla_k_nope_norm, mla_w_o=mla_w_o.astype(BF16),
        mlp_w1=mlp_w1.astype(BF16), mlp_w2=mlp_w2.astype(BF16),
    )
    bp, bs = c_prompt.shape[0], c_sample.shape[0]
    nb_pad = -(-(bp + bs) // 16) * 16
    c_all = jnp.concatenate([c_prompt, c_sample, jnp.zeros((nb_pad - bp - bs, c_prompt.shape[1]), F32)], axis=0)
    mod = _ada_modulation(c_all, ada_w, ada_b)
    y_prompt = _trunk(x_prompt, mod[:, :bp], p)
    y_sample = _trunk(x_sample, mod[:, bp:bp + bs], p)
    return (y_prompt, y_sample)
```
